```python
import math
import jax, jax.numpy as jnp
from jax import lax
import numpy as np

D_MODEL = 1024
BATCH = 4
SEQ = 8192
DEPTH = 4

NSA_HEAD_DIM = 64
NSA_HEADS = D_MODEL // 128
NSA_KV_GROUPS = 2
GQA_RATIO = NSA_HEADS // NSA_KV_GROUPS
NSA_WIDTH = NSA_HEADS * NSA_HEAD_DIM
KV_WIDTH = NSA_KV_GROUPS * NSA_HEAD_DIM
CMP_BLOCK = 32
CMP_STRIDE = 16
CMP_HIDDEN = 128
SEL_BLOCK = 64
SEL_TOPN = 16
WINDOW = 512
Q_BLOCK = 64
FORCE_SCORE = 1.0e4
NEG_INF = -1.0e30
POOL_WINDOWS = (2, 4, 8, 16)
POOL_WIDTH = D_MODEL // 2
POOL_GROUP = POOL_WIDTH // len(POOL_WINDOWS)
SSM_WIDTH = D_MODEL // 2
SSM_GROUP = 16
SSM_GROUPS = SSM_WIDTH // SSM_GROUP
SSM_STATE = 64
N_BRANCH = 3
IN_WIDTH = NSA_WIDTH + 6 * KV_WIDTH + 3 * NSA_HEADS + POOL_WIDTH + SSM_WIDTH + N_BRANCH * D_MODEL
MOE_GROUPS = 4
MOE_EXPERTS_PER_GROUP = 8
MOE_EXPERTS = MOE_GROUPS * MOE_EXPERTS_PER_GROUP
MOE_TOPK = 2
MOE_FF = D_MODEL // 2
MOE_BLOCK = 128
DN_ALPHA = (2 * DEPTH) ** 0.25
DN_BETA = (8 * DEPTH) ** -0.25

kernel_name = "hybrid_nsa_pool_s5_hmoe_deepnorm"


def layer_norm(h, g, b, eps=1e-5):
    hf = h.astype(jnp.float32)
    mu = jnp.mean(hf, -1, keepdims=True)
    var = jnp.mean(jnp.square(hf - mu), -1, keepdims=True)
    return ((hf - mu) * lax.rsqrt(var + eps)).astype(h.dtype) * g + b


def masked_softmax(s, mask):
    s = jnp.where(mask, s.astype(jnp.float32), NEG_INF)
    m = jnp.max(s, -1, keepdims=True)
    e = jnp.where(mask, jnp.exp(s - m), 0.0)
    return e / jnp.maximum(jnp.sum(e, -1, keepdims=True), 1e-30)


def alibi_slopes(n):
    return jnp.exp2(-8.0 * jnp.arange(1, n + 1, dtype=jnp.float32) / n)


def block_overlap(nc, nsel):
    i0 = jnp.arange(nc)[:, None] * CMP_STRIDE
    j0 = jnp.arange(nsel)[None, :] * SEL_BLOCK
    return ((i0 < j0 + SEL_BLOCK) & (i0 + CMP_BLOCK > j0)).astype(jnp.float32)


def compress_blocks(k, pe, w1, w2):
    B_, S_, G, hd = k.shape
    nc = S_ // CMP_STRIDE
    kp = jnp.pad(k, ((0, 0), (0, CMP_BLOCK - CMP_STRIDE), (0, 0), (0, 0)))
    pieces = [kp[:, r * CMP_STRIDE: r * CMP_STRIDE + S_].reshape(B_, nc, CMP_STRIDE, G, hd)
              for r in range(CMP_BLOCK // CMP_STRIDE)]
    blk = jnp.concatenate(pieces, axis=2) + pe[None, None, :, None, :]
    blk = blk.transpose(0, 1, 3, 2, 4).reshape(B_, nc, G, CMP_BLOCK * hd)
    return jax.nn.gelu(blk @ w1) @ w2


def gather_blocks(blocks, idx):
    return jax.vmap(jax.vmap(lambda bl, ix: bl[ix]))(blocks, idx)


def nsa_attention(q, k_cmp, v_cmp, k_slc, v_slc, k_win, v_win, gate):
    B_, S_, G, R, hd = q.shape
    nc = k_cmp.shape[1]
    nsel = S_ // SEL_BLOCK
    topn = min(SEL_TOPN, nsel)
    scale = hd ** -0.5
    slopes = alibi_slopes(NSA_HEADS).reshape(G, R)
    cmp_end = jnp.arange(nc) * CMP_STRIDE + CMP_BLOCK - 1
    overlap = block_overlap(nc, nsel)
    k_blocks = k_slc.reshape(B_, nsel, SEL_BLOCK, G, hd).transpose(0, 3, 1, 2, 4)
    v_blocks = v_slc.reshape(B_, nsel, SEL_BLOCK, G, hd).transpose(0, 3, 1, 2, 4)
    kw_pad = jnp.pad(k_win, ((0, 0), (WINDOW, 0), (0, 0), (0, 0)))
    vw_pad = jnp.pad(v_win, ((0, 0), (WINDOW, 0), (0, 0), (0, 0)))
    jb = jnp.arange(nsel)

    def block(i):
        q0 = i * Q_BLOCK
        t = q0 + jnp.arange(Q_BLOCK)
        qb = lax.dynamic_slice_in_dim(q, q0, Q_BLOCK, axis=1)
        gb = lax.dynamic_slice_in_dim(gate, q0, Q_BLOCK, axis=1)
        dist_c = (t[:, None] - cmp_end[None, :])
        s_c = jnp.einsum('bqgrd,bcgd->bgrqc', qb, k_cmp).astype(jnp.float32) * scale \
            - slopes[:, :, None, None] * dist_c.astype(jnp.float32)
        p_c = masked_softmax(s_c, dist_c >= 0)
        o_c = jnp.einsum('bgrqc,bcgd->bqgrd', p_c.astype(v_cmp.dtype), v_cmp)
        imp = jnp.einsum('bgrqc,cj->bgqj', p_c, overlap)
        tb = t // SEL_BLOCK
        forced = (jb[None, :] == 0) | (jb[None, :] == tb[:, None]) | (jb[None, :] == tb[:, None] - 1)
        score = jnp.where(forced, FORCE_SCORE, jnp.where(jb[None, :] <= tb[:, None], imp, -1.0))
        _, idx = lax.top_k(score, topn)
        flat = idx.reshape(B_, G, Q_BLOCK * topn)
        kg = gather_blocks(k_blocks, flat).reshape(B_, G, Q_BLOCK, topn * SEL_BLOCK, hd)
        vg = gather_blocks(v_blocks, flat).reshape(B_, G, Q_BLOCK, topn * SEL_BLOCK, hd)
        pos = (idx[..., None] * SEL_BLOCK + jnp.arange(SEL_BLOCK)).reshape(B_, G, Q_BLOCK, topn * SEL_BLOCK)
        dist_s = t[None, None, :, None] - pos
        s_s = jnp.einsum('bqgrd,bgqkd->bgrqk', qb, kg).astype(jnp.float32) * scale \
            - slopes[None, :, :, None, None] * dist_s[:, :, None].astype(jnp.float32)
        p_s = masked_softmax(s_s, (dist_s >= 0)[:, :, None])
        o_s = jnp.einsum('bgrqk,bgqkd->bqgrd', p_s.astype(vg.dtype), vg)
        kwb = lax.dynamic_slice_in_dim(kw_pad, q0, WINDOW + Q_BLOCK, axis=1)
        vwb = lax.dynamic_slice_in_dim(vw_pad, q0, WINDOW + Q_BLOCK, axis=1)
        spos = q0 - WINDOW + jnp.arange(WINDOW + Q_BLOCK)
        dist_w = t[:, None] - spos[None, :]
        valid_w = (dist_w >= 0) & (dist_w < WINDOW) & (spos[None, :] >= 0)
        s_w = jnp.einsum('bqgrd,bkgd->bgrqk', qb, kwb).astype(jnp.float32) * scale \
            - slopes[:, :, None, None] * dist_w.astype(jnp.float32)
        p_w = masked_softmax(s_w, valid_w)
        o_w = jnp.einsum('bgrqk,bkgd->bqgrd', p_w.astype(vwb.dtype), vwb)
        return gb[..., 0:1] * o_c + gb[..., 1:2] * o_s + gb[..., 2:3] * o_w

    out = lax.map(block, jnp.arange(S_ // Q_BLOCK))
    return out.transpose(1, 0, 2, 3, 4, 5).reshape(B_, S_, G * R * hd)


def pool_mixer(u, pool_w, pool_scale):
    B_, S_, _ = u.shape
    cs = jnp.pad(jnp.cumsum(u.astype(jnp.float32), axis=1), ((0, 0), (1, 0), (0, 0)))
    t = jnp.arange(S_)
    outs = []
    for g, w in enumerate(POOL_WINDOWS):
        sl = slice(g * POOL_GROUP, (g + 1) * POOL_GROUP)
        csg = cs[..., sl]
        lo = jnp.maximum(t + 1 - w, 0)
        cnt = jnp.minimum(t + 1, w).astype(jnp.float32)
        pooled = ((csg[:, 1:] - csg[:, lo]) / cnt[None, :, None]).astype(u.dtype) - u[..., sl]
        outs.append(pooled @ pool_w[g])
    return jnp.concatenate(outs, -1) * pool_scale


def _ssm_combine(e1, e2):
    a1r, a1i, b1r, b1i = e1
    a2r, a2i, b2r, b2i = e2
    return (a2r * a1r - a2i * a1i, a2r * a1i + a2i * a1r,
            a2r * b1r - a2i * b1i + b2r, a2r * b1i + a2i * b1r + b2i)


def ssm_mixer(u, lam_re, lam_im, log_dt, b_re, b_im, c_re, c_im, d_skip, w_glu, b_glu):
    B_, S_, _ = u.shape
    f32 = jnp.float32
    lr, li = lam_re.astype(f32), lam_im.astype(f32)
    dt = jnp.exp(log_dt.astype(f32))[:, None]
    mag = jnp.exp(lr * dt)
    ar, ai = mag * jnp.cos(li * dt), mag * jnp.sin(li * dt)
    den = lr * lr + li * li
    cr = ((ar - 1.0) * lr + ai * li) / den
    ci = (ai * lr - (ar - 1.0) * li) / den
    bbr = cr[..., None] * b_re.astype(f32) - ci[..., None] * b_im.astype(f32)
    bbi = cr[..., None] * b_im.astype(f32) + ci[..., None] * b_re.astype(f32)
    cre, cim = c_re.astype(f32), c_im.astype(f32)

    def one(ub):
        ug = ub.reshape(S_, SSM_GROUPS, SSM_GROUP).astype(f32)
        xr = jnp.einsum('sgh,gph->sgp', ug, bbr)
        xi = jnp.einsum('sgh,gph->sgp', ug, bbi)
        a_r = jnp.broadcast_to(ar, xr.shape)
        a_i = jnp.broadcast_to(ai, xr.shape)
        _, _, hr, hi = lax.associative_scan(_ssm_combine, (a_r, a_i, xr, xi), axis=0)
        y = jnp.einsum('ghp,sgp->sgh', cre, hr) - jnp.einsum('ghp,sgp->sgh', cim, hi)
        return y.reshape(S_, SSM_WIDTH)

    y = lax.map(one, u).astype(u.dtype) + d_skip * u
    z = jax.nn.gelu(y)
    return z * jax.nn.sigmoid(z @ w_glu + b_glu)


def token_mixer(x, w_in, pe_k, pe_v, ck1, ck2, cv1, cv2, pool_w, pool_scale,
                lam_re, lam_im, log_dt, b_re, b_im, c_re, c_im, d_skip, w_glu, b_glu,
                w_up_nsa, w_up_pool, w_up_ssm, w_out):
    B_, S_, D = x.shape
    proj = jnp.einsum('bsd,dn->bsn', x, w_in)
    splits = np.cumsum([NSA_WIDTH] + [KV_WIDTH] * 6 + [3 * NSA_HEADS, POOL_WIDTH, SSM_WIDTH])
    q, kc, vc, ks, vs, kw, vw, ng, u_pool, u_ssm, bg = jnp.split(proj, splits, axis=-1)
    G, R, hd = NSA_KV_GROUPS, GQA_RATIO, NSA_HEAD_DIM
    kv = lambda a: a.reshape(B_, S_, G, hd)
    o_nsa = nsa_attention(q.reshape(B_, S_, G, R, hd),
                          compress_blocks(kv(kc), pe_k, ck1, ck2),
                          compress_blocks(kv(vc), pe_v, cv1, cv2),
                          kv(ks), kv(vs), kv(kw), kv(vw),
                          jax.nn.sigmoid(ng).reshape(B_, S_, G, R, 3))
    o_pool = pool_mixer(u_pool, pool_w, pool_scale)
    o_ssm = ssm_mixer(u_ssm, lam_re, lam_im, log_dt, b_re, b_im, c_re, c_im, d_skip, w_glu, b_glu)
    gates = jax.nn.sigmoid(bg).reshape(B_, S_, N_BRANCH, D)
    merged = (gates[:, :, 0] * (o_nsa @ w_up_nsa)
              + gates[:, :, 1] * (o_pool @ w_up_pool)
              + gates[:, :, 2] * (o_ssm @ w_up_ssm))
    return merged @ w_out


def moe_ffn(x, w_grp, b_grp, w_exp, b_exp, w_gate, w_up, w_down):
    B_, S_, D = x.shape
    T = B_ * S_
    xf = x.reshape(T, D)
    p_grp = jax.nn.softmax((xf @ w_grp + b_grp).astype(jnp.float32), -1)
    g_sel = jnp.argmax(p_grp, -1)
    p_g = jnp.max(p_grp, -1)
    logit_e = (xf @ w_exp + b_exp).astype(jnp.float32).reshape(T, MOE_GROUPS, MOE_EXPERTS_PER_GROUP)
    p_in = jax.nn.softmax(logit_e[jnp.arange(T), g_sel], -1)
    w_top, e_top = lax.top_k(p_in, MOE_TOPK)
    w_top = w_top / jnp.sum(w_top, -1, keepdims=True) * p_g[:, None]
    expert = (g_sel[:, None] * MOE_EXPERTS_PER_GROUP + e_top).astype(jnp.int32)
    N = T * MOE_TOPK
    flat_e = expert.reshape(N)
    flat_t = jnp.repeat(jnp.arange(T, dtype=jnp.int32), MOE_TOPK)
    flat_w = w_top.reshape(N)
    order = jnp.argsort(flat_e)
    se, st, sw = flat_e[order], flat_t[order], flat_w[order]
    counts = jax.ops.segment_sum(jnp.ones((N,), jnp.int32), flat_e, num_segments=MOE_EXPERTS)
    starts = jnp.cumsum(counts) - counts
    padded = (counts + MOE_BLOCK - 1) // MOE_BLOCK * MOE_BLOCK
    pend = jnp.cumsum(padded)
    pstart = pend - padded
    dest = pstart[se] + jnp.arange(N, dtype=jnp.int32) - starts[se]
    nblk = -(-N // MOE_BLOCK) + MOE_EXPERTS
    P = nblk * MOE_BLOCK
    buf_t = jnp.zeros((P,), jnp.int32).at[dest].set(st)
    buf_w = jnp.zeros((P,), jnp.float32).at[dest].set(sw)
    blk_e = jnp.minimum(jnp.searchsorted(pend, jnp.arange(nblk, dtype=jnp.int32) * MOE_BLOCK, side='right'),
                        MOE_EXPERTS - 1)
    xs = xf[buf_t].reshape(nblk, MOE_BLOCK, D)

    def expert_block(args):
        xb, e = args
        h = jax.nn.silu(xb @ w_gate[e]) * (xb @ w_up[e])
        return h @ w_down[e]

    ys = lax.map(expert_block, (xs, blk_e)).reshape(P, D)
    out = jax.ops.segment_sum(ys * buf_w[:, None].astype(ys.dtype), buf_t, num_segments=T)
    return out.reshape(B_, S_, D)


def setup_inputs(seed: int = 0) -> dict:
    key = jax.random.key(seed)
    ks = iter(jax.random.split(key, 40))
    f32 = jnp.float32

    def nrm(shape, scale):
        return jax.random.normal(next(ks), shape, f32) * scale

    L, D, hd = DEPTH, D_MODEL, NSA_HEAD_DIM
    E = MOE_EXPERTS
    return {
        "x": nrm((BATCH, SEQ, D), 1.0),
        "w_in": nrm((L, D, IN_WIDTH), D ** -0.5),
        "cmp_pe_k": nrm((L, CMP_BLOCK, hd), 0.1),
        "cmp_pe_v": nrm((L, CMP_BLOCK, hd), 0.1),
        "cmp_w1_k": nrm((L, CMP_BLOCK * hd, CMP_HIDDEN), (CMP_BLOCK * hd) ** -0.5),
        "cmp_w2_k": nrm((L, CMP_HIDDEN, hd), CMP_HIDDEN ** -0.5),
        "cmp_w1_v": nrm((L, CMP_BLOCK * hd, CMP_HIDDEN), (CMP_BLOCK * hd) ** -0.5),
        "cmp_w2_v": nrm((L, CMP_HIDDEN, hd), CMP_HIDDEN ** -0.5),
        "pool_w": nrm((L, len(POOL_WINDOWS), POOL_GROUP, POOL_GROUP), POOL_GROUP ** -0.5),
        "pool_scale": 1.0 + nrm((L, POOL_WIDTH), 0.1),
        "ssm_lam_re": -0.5 + nrm((L, SSM_GROUPS, SSM_STATE), 0.01),
        "ssm_lam_im": math.pi * jnp.arange(SSM_STATE, dtype=f32) + nrm((L, SSM_GROUPS, SSM_STATE), 0.01),
        "ssm_log_dt": jax.random.uniform(next(ks), (L, SSM_GROUPS), f32, math.log(1e-3), math.log(1e-1)),
        "ssm_b_re": nrm((L, SSM_GROUPS, SSM_STATE, SSM_GROUP), (2 * SSM_GROUP) ** -0.5),
        "ssm_b_im": nrm((L, SSM_GROUPS, SSM_STATE, SSM_GROUP), (2 * SSM_GROUP) ** -0.5),
        "ssm_c_re": nrm((L, SSM_GROUPS, SSM_GROUP, SSM_STATE), SSM_STATE ** -0.5),
        "ssm_c_im": nrm((L, SSM_GROUPS, SSM_GROUP, SSM_STATE), SSM_STATE ** -0.5),
        "ssm_d": nrm((L, SSM_WIDTH), 1.0),
        "ssm_w_glu": nrm((L, SSM_WIDTH, SSM_WIDTH), SSM_WIDTH ** -0.5),
        "ssm_b_glu": nrm((L, SSM_WIDTH), 0.02),
        "w_up_nsa": nrm((L, NSA_WIDTH, D), NSA_WIDTH ** -0.5),
        "w_up_pool": nrm((L, POOL_WIDTH, D), POOL_WIDTH ** -0.5),
        "w_up_ssm": nrm((L, SSM_WIDTH, D), SSM_WIDTH ** -0.5),
        "w_out": nrm((L, D, D), D ** -0.5 * DN_BETA),
        "ln1_g": 1.0 + nrm((L, D), 0.02),
        "ln1_b": nrm((L, D), 0.02),
        "router_w_grp": nrm((L, D, MOE_GROUPS), D ** -0.5),
        "router_b_grp": nrm((L, MOE_GROUPS), 0.01),
        "router_w_exp": nrm((L, D, E), D ** -0.5),
        "router_b_exp": nrm((L, E), 0.01),
        "moe_w_gate": nrm((L, E, D, MOE_FF), D ** -0.5),
        "moe_w_up": nrm((L, E, D, MOE_FF), D ** -0.5),
        "moe_w_down": nrm((L, E, MOE_FF, D), MOE_FF ** -0.5 * DN_BETA),
        "ln2_g": 1.0 + nrm((L, D), 0.02),
        "ln2_b": nrm((L, D), 0.02),
    }


def reference(x, w_in, cmp_pe_k, cmp_pe_v, cmp_w1_k, cmp_w2_k, cmp_w1_v, cmp_w2_v,
              pool_w, pool_scale, ssm_lam_re, ssm_lam_im, ssm_log_dt, ssm_b_re, ssm_b_im,
              ssm_c_re, ssm_c_im, ssm_d, ssm_w_glu, ssm_b_glu, w_up_nsa, w_up_pool, w_up_ssm,
              w_out, ln1_g, ln1_b, router_w_grp, router_b_grp, router_w_exp, router_b_exp,
              moe_w_gate, moe_w_up, moe_w_down, ln2_g, ln2_b):
    for l in range(DEPTH):
        mix = token_mixer(x, w_in[l], cmp_pe_k[l], cmp_pe_v[l], cmp_w1_k[l], cmp_w2_k[l],
                          cmp_w1_v[l], cmp_w2_v[l], pool_w[l], pool_scale[l],
                          ssm_lam_re[l], ssm_lam_im[l], ssm_log_dt[l], ssm_b_re[l], ssm_b_im[l],
                          ssm_c_re[l], ssm_c_im[l], ssm_d[l], ssm_w_glu[l], ssm_b_glu[l],
                          w_up_nsa[l], w_up_pool[l], w_up_ssm[l], w_out[l])
        x = layer_norm(DN_ALPHA * x + mix, ln1_g[l], ln1_b[l])
        ffn = moe_ffn(x, router_w_grp[l], router_b_grp[l], router_w_exp[l], router_b_exp[l],
                      moe_w_gate[l], moe_w_up[l], moe_w_down[l])
        x = layer_norm(DN_ALPHA * x + ffn, ln2_g[l], ln2_b[l])
    return x
```

```python
import functools
import math

import jax
import jax.numpy as jnp
from jax import lax
from jax.experimental import pallas as pl
from jax.experimental.pallas import tpu as pltpu

F32 = jnp.float32
BF16 = jnp.bfloat16

D_MODEL = 1024
DEPTH = 4
HEAD_DIM = 64
HEADS = 8
KV_GROUPS = 2
GQA = HEADS // KV_GROUPS
CMP_BLOCK = 32
CMP_STRIDE = 16
CMP_HIDDEN = 128
SEL_BLOCK = 64
SEL_TOPN = 16
WINDOW = 512
FORCE_SCORE = 1.0e4
NEG_INF = -1.0e30
POOL_WINDOWS = (2, 4, 8, 16)
POOL_HALO = 16
SSM_GROUP = 16
SSM_GROUPS = 32
SSM_STATE = 64
SSM_NS = SSM_GROUPS * SSM_STATE
MOE_GROUPS = 4
MOE_EPG = 8
MOE_EXPERTS = 32
MOE_FF = 512
DN_ALPHA = (2 * DEPTH) ** 0.25
LN_EPS = 1e-5

LANES = 128
SUBLANES = 8
VMEM_LIMIT = 48 * 1024 * 1024
Q_TILE = 128
SEL_CHUNK = 512
MOE_ROWS = 256
EXP_LANE0 = 32
SEL_SHIFT = SEL_BLOCK.bit_length() - 1
EPG_SHIFT = MOE_EPG.bit_length() - 1


def _cparams(*sem):
    return pltpu.CompilerParams(dimension_semantics=sem, vmem_limit_bytes=VMEM_LIMIT)


def _gelu(x):
    return 0.5 * x * (1.0 + jnp.tanh(math.sqrt(2.0 / math.pi) * (x + 0.044715 * x * x * x)))


def _sigmoid(x):
    return 1.0 / (1.0 + jnp.exp(-x))


def _dot(a, b):
    return jnp.dot(a, b, preferred_element_type=F32)


def _dot_nt(a, b):
    return lax.dot_general(a, b, (((1,), (1,)), ((), ())), preferred_element_type=F32)


def _layer_norm(h, g, b):
    mu = jnp.mean(h, -1, keepdims=True)
    c = h - mu
    var = jnp.mean(c * c, -1, keepdims=True)
    return c * lax.rsqrt(var + LN_EPS) * g + b


def _proj_kernel(x_ref, w_ref, *out_refs, offs):
    x = x_ref[...]
    for o_ref, (a, n) in zip(out_refs, offs):
        o_ref[...] = _dot(x, w_ref[:, a:a + n]).astype(o_ref.dtype)


def _proj(xb, w, outs, tm):
    T, K = xb.shape
    N = w.shape[1]
    offs, a = [], 0
    for n, _ in outs:
        offs.append((a, n))
        a += n
    assert a == N and T % tm == 0
    return pl.pallas_call(
        functools.partial(_proj_kernel, offs=tuple(offs)),
        grid=(T // tm,),
        in_specs=[pl.BlockSpec((tm, K), lambda i: (i, 0)), pl.BlockSpec((K, N), lambda i: (0, 0))],
        out_specs=[pl.BlockSpec((tm, n), lambda i: (i, 0)) for n, _ in outs],
        out_shape=[jax.ShapeDtypeStruct((T, n), dt) for n, dt in outs],
        compiler_params=_cparams("parallel"),
    )(xb, w)


def _proj_tm_kernel(x_ref, w_ref, o_ref):
    o_ref[...] = _dot(x_ref[...], w_ref[...])


def _proj_time_major(xb, w, B, S, tm):
    K = xb.shape[1]
    N = w.shape[1]
    nt = S // tm
    return pl.pallas_call(
        _proj_tm_kernel,
        grid=(B, nt),
        in_specs=[pl.BlockSpec((tm, K), lambda b, i: (b * nt + i, 0)), pl.BlockSpec((K, N), lambda b, i: (0, 0))],
        out_specs=pl.BlockSpec((tm, N), lambda b, i: (i, b)),
        out_shape=jax.ShapeDtypeStruct((S, B * N), F32),
        compiler_params=_cparams("parallel", "parallel"),
    )(xb, w)


def _compress_kernel(kr_ref, w1_ref, pe_ref, w2_ref, out_ref):
    nc = kr_ref.shape[0]
    hw = 4 * CMP_HIDDEN
    a = _dot(kr_ref[...], w1_ref[...])
    pb = _dot(pe_ref[...], w1_ref[...])
    second = pltpu.roll(a[:, hw:], nc - 1, 0)
    row = lax.broadcasted_iota(jnp.int32, (nc, hw), 0)
    second = jnp.where(row < nc - 1, second, 0.0)
    hid = _gelu(a[:, :hw] + second + pb[0:1, :hw] + pb[1:2, hw:])
    out_ref[0] = _dot(hid.astype(BF16), w2_ref[...]).astype(out_ref.dtype)


def _compress(kr, w1, pe_in, w2, B, nc):
    return pl.pallas_call(
        _compress_kernel,
        grid=(B,),
        in_specs=[pl.BlockSpec((nc, kr.shape[1]), lambda b: (b, 0)),
                  pl.BlockSpec(w1.shape, lambda b: (0, 0)),
                  pl.BlockSpec(pe_in.shape, lambda b: (0, 0)),
                  pl.BlockSpec(w2.shape, lambda b: (0, 0))],
        out_specs=pl.BlockSpec((1, nc, 2 * LANES), lambda b: (b, 0, 0)),
        out_shape=jax.ShapeDtypeStruct((B, nc, 2 * LANES), BF16),
        compiler_params=_cparams("parallel"),
    )(kr, w1, pe_in, w2)


def _slope(h):
    return 2.0 ** (-8.0 * (h + 1) / HEADS)


def _stack_heads(q, g):
    return jnp.concatenate([q[:, (g * GQA + r) * LANES:(g * GQA + r + 1) * LANES] for r in range(GQA)], axis=0)


def _split3_dot(a, b):
    hi = a.astype(BF16)
    r1 = a - hi.astype(F32)
    mid = r1.astype(BF16)
    lo = (r1 - mid.astype(F32)).astype(BF16)
    return _dot(hi, b) + _dot(mid, b) + _dot(lo, b)


def _nsa1_kernel(q_ref, kvc_ref, ng_ref, ov_ref, acc_ref, sel_ref, sc_ref):
    Q = q_ref.shape[0]
    nc = kvc_ref.shape[1]
    t0 = pl.program_id(1) * Q
    q = q_ref[...]
    kc = kvc_ref[0, :, 0:LANES]
    vc = kvc_ref[0, :, LANES:2 * LANES]
    gate = _sigmoid(ng_ref[...])
    qrow = lax.broadcasted_iota(jnp.int32, (Q, nc), 0)
    ccol = lax.broadcasted_iota(jnp.int32, (Q, nc), 1)
    dist = (t0 + qrow) - (ccol * CMP_STRIDE + (CMP_BLOCK - 1))
    valid = dist >= 0
    distf = dist.astype(F32)
    lane = lax.broadcasted_iota(jnp.int32, (Q, LANES), 1)
    tb = jnp.right_shift(t0 + lax.broadcasted_iota(jnp.int32, (Q, LANES), 0), SEL_SHIFT)
    jrow = lax.broadcasted_iota(jnp.int32, (LANES, Q), 0)
    nvalid = (t0 + Q - 1) // SEL_BLOCK + 1
    outs = []
    for g in range(KV_GROUPS):
        qs = _stack_heads(q, g)
        s = _dot_nt(qs, kc)
        psum = jnp.zeros((Q, nc), F32)
        og = []
        for r in range(GQA):
            h = g * GQA + r
            sr = jnp.where(valid, s[r * Q:(r + 1) * Q] - _slope(h) * distf, NEG_INF)
            m = jnp.max(sr, -1, keepdims=True)
            e = jnp.where(valid, jnp.exp(sr - m), 0.0)
            p = e / jnp.maximum(jnp.sum(e, -1, keepdims=True), 1e-30)
            psum = psum + p
            og.append(gate[:, 3 * h:3 * h + 1] * _dot(p.astype(BF16), vc))
        outs.append(og)
        imp = _split3_dot(psum, ov_ref[...])
        forced = (lane == 0) | (lane == tb) | (lane == tb - 1)
        score = jnp.where(forced, FORCE_SCORE, jnp.where(lane <= tb, imp, -1.0))
        sc_ref[...] = score.T
        st = sc_ref[...]

        def rank_step(jp, cnt, st=st):
            rowb = jnp.broadcast_to(sc_ref[pl.ds(jp, 1), :], st.shape)
            ge = jnp.where(rowb >= st, 1.0, 0.0)
            gt = jnp.where(rowb > st, 1.0, 0.0)
            return cnt + jnp.where(jrow > jp, ge, gt)

        cnt = lax.fori_loop(0, nvalid, rank_step, jnp.zeros((LANES, Q), F32))
        sel = jnp.where(cnt < SEL_TOPN, 1.0, 0.0).T
        sel = jnp.where(lane <= tb, sel, 0.0)
        sel_ref[:, g * LANES:(g + 1) * LANES] = sel.astype(sel_ref.dtype)
    for r in range(GQA):
        acc_ref[:, r * LANES:(r + 1) * LANES] = jnp.where(lane < HEAD_DIM, outs[0][r], outs[1][r])


def _nsa1(qw, kvc, ng, ov, B, S):
    nq = S // Q_TILE
    nc = kvc.shape[1]
    T = B * S
    return pl.pallas_call(
        _nsa1_kernel,
        grid=(B, nq),
        in_specs=[pl.BlockSpec((Q_TILE, HEADS * LANES), lambda b, i: (b * nq + i, 0)),
                  pl.BlockSpec((1, nc, 2 * LANES), lambda b, i: (b, 0, 0)),
                  pl.BlockSpec((Q_TILE, LANES), lambda b, i: (b * nq + i, 0)),
                  pl.BlockSpec((nc, LANES), lambda b, i: (0, 0))],
        out_specs=[pl.BlockSpec((Q_TILE, GQA * LANES), lambda b, i: (b * nq + i, 0)),
                   pl.BlockSpec((Q_TILE, KV_GROUPS * LANES), lambda b, i: (b * nq + i, 0))],
        out_shape=[jax.ShapeDtypeStruct((T, GQA * LANES), F32),
                   jax.ShapeDtypeStruct((T, KV_GROUPS * LANES), BF16)],
        scratch_shapes=[pltpu.VMEM((LANES, Q_TILE), F32)],
        compiler_params=_cparams("parallel", "parallel"),
    )(qw, kvc, ng, ov)


def _nsa2_kernel(q_ref, ks_ref, vs_ref, kw_ref, vw_ref, sel_ref, ng_ref, acc_ref, o_ref, m_sc, l_sc, a_sc):
    Q = q_ref.shape[0]
    C = SEL_CHUNK
    WK = WINDOW + Q
    t0 = pl.program_id(1) * Q
    q = q_ref[...]
    gate = _sigmoid(ng_ref[...])
    lane = lax.broadcasted_iota(jnp.int32, (Q, LANES), 1)
    nchunks = (t0 + Q + C - 1) // C
    start = pl.multiple_of(jnp.maximum(t0 - WINDOW, 0), Q)
    wrow = lax.broadcasted_iota(jnp.int32, (Q, WK), 0)
    wcol = lax.broadcasted_iota(jnp.int32, (Q, WK), 1)
    wdist = (t0 + wrow) - (start + wcol)
    wvalid = (wdist >= 0) & (wdist < WINDOW)
    wdistf = wdist.astype(F32)
    outs = []
    for g in range(KV_GROUPS):
        qs = _stack_heads(q, g)
        selg = sel_ref[:, g * LANES:(g + 1) * LANES]
        m_sc[...] = jnp.full(m_sc.shape, NEG_INF, F32)
        l_sc[...] = jnp.zeros(l_sc.shape, F32)
        a_sc[...] = jnp.zeros(a_sc.shape, F32)

        def chunk(c, carry, qs=qs, selg=selg, g=g):
            k0 = pl.multiple_of(c * C, C)
            kk = ks_ref[pl.ds(k0, C), :]
            vv = vs_ref[pl.ds(k0, C), :]
            s = _dot_nt(qs, kk)
            jr = lax.broadcasted_iota(jnp.int32, (LANES, C), 0)
            kcol = lax.broadcasted_iota(jnp.int32, (LANES, C), 1)
            expand = jnp.where(jr == jnp.right_shift(k0 + kcol, SEL_SHIFT), 1.0, 0.0).astype(BF16)
            selx = _dot(selg, expand)
            qrow = lax.broadcasted_iota(jnp.int32, (Q, C), 0)
            kc = lax.broadcasted_iota(jnp.int32, (Q, C), 1)
            dist = (t0 + qrow) - (k0 + kc)
            valid = (selx > 0.5) & (dist >= 0)
            distf = dist.astype(F32)
            for r in range(GQA):
                rows = slice(r * Q, (r + 1) * Q)
                sr = jnp.where(valid, s[rows] - _slope(g * GQA + r) * distf, NEG_INF)
                m_old = m_sc[rows]
                m_new = jnp.maximum(m_old, jnp.max(sr, -1, keepdims=True))
                alpha = jnp.exp(m_old - m_new)
                e = jnp.where(valid, jnp.exp(sr - m_new[:, 0:1]), 0.0)
                l_sc[rows] = alpha * l_sc[rows] + jnp.sum(e, -1, keepdims=True)
                a_sc[rows] = alpha * a_sc[rows] + _dot(e.astype(BF16), vv)
                m_sc[rows] = m_new
            return carry

        lax.fori_loop(0, nchunks, chunk, 0)
        kw = kw_ref[pl.ds(start, WK), :]
        vw = vw_ref[pl.ds(start, WK), :]
        sw = _dot_nt(qs, kw)
        og = []
        for r in range(GQA):
            h = g * GQA + r
            rows = slice(r * Q, (r + 1) * Q)
            o_s = a_sc[rows] / jnp.maximum(l_sc[rows], 1e-30)
            sr = jnp.where(wvalid, sw[rows] - _slope(h) * wdistf, NEG_INF)
            m = jnp.max(sr, -1, keepdims=True)
            e = jnp.where(wvalid, jnp.exp(sr - m), 0.0)
            o_w = _dot(e.astype(BF16), vw) / jnp.maximum(jnp.sum(e, -1, keepdims=True), 1e-30)
            og.append(gate[:, 3 * h + 1:3 * h + 2] * o_s + gate[:, 3 * h + 2:3 * h + 3] * o_w)
        outs.append(og)
    for r in range(GQA):
        cols = slice(r * LANES, (r + 1) * LANES)
        o_ref[:, cols] = (acc_ref[:, cols] + jnp.where(lane < HEAD_DIM, outs[0][r], outs[1][r])).astype(o_ref.dtype)


def _nsa2(qw, kv, sel, ng, acc, B, S):
    nq = S // Q_TILE
    T = B * S
    rows = lambda b, i: (b * nq + i, 0)
    kvspec = lambda col: pl.BlockSpec((S, LANES), lambda b, i, col=col: (b, col))
    return pl.pallas_call(
        _nsa2_kernel,
        grid=(B, nq),
        in_specs=[pl.BlockSpec((Q_TILE, HEADS * LANES), rows),
                  kvspec(2), kvspec(3), kvspec(4), kvspec(5),
                  pl.BlockSpec((Q_TILE, KV_GROUPS * LANES), rows),
                  pl.BlockSpec((Q_TILE, LANES), rows),
                  pl.BlockSpec((Q_TILE, GQA * LANES), rows)],
        out_specs=pl.BlockSpec((Q_TILE, GQA * LANES), rows),
        out_shape=jax.ShapeDtypeStruct((T, GQA * LANES), BF16),
        scratch_shapes=[pltpu.VMEM((GQA * Q_TILE, LANES), F32),
                        pltpu.VMEM((GQA * Q_TILE, LANES), F32),
                        pltpu.VMEM((GQA * Q_TILE, LANES), F32)],
        compiler_params=_cparams("parallel", "parallel"),
    )(qw, kv, kv, kv, kv, sel, ng, acc)


def _pool_kernel(u_ref, halo_ref, w_ref, sc_ref, o_ref, x_sc):
    ts = u_ref.shape[0]
    i = pl.program_id(1)
    x_sc[0:POOL_HALO] = jnp.where(i > 0, halo_ref[...], 0.0)
    x_sc[POOL_HALO:POOL_HALO + ts] = u_ref[...]
    t = i * ts + lax.broadcasted_iota(jnp.int32, (ts, LANES), 0)
    for g, w in enumerate(POOL_WINDOWS):
        cols = slice(g * LANES, (g + 1) * LANES)
        cur = x_sc[POOL_HALO:POOL_HALO + ts, cols]
        acc = cur
        for k in range(1, w):
            acc = acc + x_sc[POOL_HALO - k:POOL_HALO - k + ts, cols]
        cnt = jnp.minimum(t + 1, w).astype(F32)
        pooled = acc / cnt - cur
        o_ref[:, cols] = (_dot(pooled.astype(BF16), w_ref[g]) * sc_ref[:, cols]).astype(o_ref.dtype)


def _pool(u, pool_w, pool_scale, B, S, ts):
    nt = S // ts
    hb = ts // POOL_HALO
    C = u.shape[1]
    return pl.pallas_call(
        _pool_kernel,
        grid=(B, nt),
        in_specs=[pl.BlockSpec((ts, C), lambda b, i: (b * nt + i, 0)),
                  pl.BlockSpec((POOL_HALO, C), lambda b, i: (jnp.maximum((b * nt + i) * hb - 1, 0), 0)),
                  pl.BlockSpec(pool_w.shape, lambda b, i: (0, 0, 0)),
                  pl.BlockSpec((1, C), lambda b, i: (0, 0))],
        out_specs=pl.BlockSpec((ts, C), lambda b, i: (b * nt + i, 0)),
        out_shape=jax.ShapeDtypeStruct((B * S, C), BF16),
        scratch_shapes=[pltpu.VMEM((POOL_HALO + ts, C), F32)],
        compiler_params=_cparams("parallel", "parallel"),
    )(u, u, pool_w, pool_scale)


SSM_PASS = 8


def _ssm_kernel(u_ref, bt_ref, a_ref, c_ref, d_ref, wg_ref, bg_ref, o_ref, xh, hst):
    rows = u_ref.shape[0]

    @pl.when(pl.program_id(0) == 0)
    def _():
        hst[...] = jnp.zeros(hst.shape, F32)

    u = u_ref[...]
    xh[...] = _dot(u.astype(BF16), bt_ref[...])
    low = lax.broadcasted_iota(jnp.int32, (SUBLANES, LANES), 0) < 4
    nchunk = SSM_NS // LANES
    for p in range(nchunk // SSM_PASS):
        cols = [p * SSM_PASS + j for j in range(SSM_PASS)]
        re = [slice(c * LANES, (c + 1) * LANES) for c in cols]
        im = [slice(SSM_NS + c * LANES, SSM_NS + (c + 1) * LANES) for c in cols]
        ar = [a_ref[:, s] for s in re]
        ai = [a_ref[:, s] for s in im]

        def tile(k, carry, re=re, im=im, ar=ar, ai=ai):
            cr, ci = carry
            r0 = pl.multiple_of(k * SUBLANES, SUBLANES)
            ncr, nci = [], []
            for j in range(SSM_PASS):
                vr = xh[pl.ds(r0, SUBLANES), re[j]]
                vi = xh[pl.ds(r0, SUBLANES), im[j]]
                t1r = ar[j] * cr[j] - ai[j] * ci[j] + vr
                t1i = ar[j] * ci[j] + ai[j] * cr[j] + vi
                pr = pltpu.roll(t1r, 4, 0)
                pi = pltpu.roll(t1i, 4, 0)
                t2r = ar[j] * pr - ai[j] * pi + vr
                t2i = ar[j] * pi + ai[j] * pr + vi
                xh[pl.ds(r0, SUBLANES), re[j]] = jnp.where(low, t1r, t2r)
                xh[pl.ds(r0, SUBLANES), im[j]] = jnp.where(low, t1i, t2i)
                ncr.append(pltpu.roll(t2r, 4, 0))
                nci.append(pltpu.roll(t2i, 4, 0))
            return tuple(ncr), tuple(nci)

        cr0 = tuple(hst[:, s] for s in re)
        ci0 = tuple(hst[:, s] for s in im)
        cr, ci = lax.fori_loop(0, rows // SUBLANES, tile, (cr0, ci0))
        for j in range(SSM_PASS):
            hst[:, re[j]] = cr[j]
            hst[:, im[j]] = ci[j]
    y = _dot(xh[...].astype(BF16), c_ref[...]) + d_ref[...] * u
    z = _gelu(y)
    o_ref[...] = (z * _sigmoid(_dot(z.astype(BF16), wg_ref[...]) + bg_ref[...])).astype(o_ref.dtype)


def _ssm(u_tm, bt, a_b, cmat, d_skip, w_glu, b_glu, rows):
    n, C = u_tm.shape
    const = lambda shp: pl.BlockSpec(shp, lambda i: tuple(0 for _ in shp))
    return pl.pallas_call(
        _ssm_kernel,
        grid=(n // rows,),
        in_specs=[pl.BlockSpec((rows, C), lambda i: (i, 0)),
                  const(bt.shape), const(a_b.shape), const(cmat.shape), const(d_skip.shape),
                  const(w_glu.shape), const(b_glu.shape)],
        out_specs=pl.BlockSpec((rows, C), lambda i: (i, 0)),
        out_shape=jax.ShapeDtypeStruct((n, C), BF16),
        scratch_shapes=[pltpu.VMEM((rows, 2 * SSM_NS), F32), pltpu.VMEM((SUBLANES, 2 * SSM_NS), F32)],
        compiler_params=_cparams("arbitrary"),
    )(u_tm, bt, a_b, cmat, d_skip, w_glu, b_glu)


def _merge_kernel(on_ref, op_ref, os_ref, bg_ref, x_ref, wn_ref, wp_ref, ws_ref, wo_ref, g_ref, b_ref, xo_ref, xb_ref):
    D = x_ref.shape[1]
    gates = _sigmoid(bg_ref[...])
    m = (gates[:, 0:D] * _dot(on_ref[...], wn_ref[...])
         + gates[:, D:2 * D] * _dot(op_ref[...], wp_ref[...])
         + gates[:, 2 * D:3 * D] * _dot(os_ref[...], ws_ref[...]))
    h = DN_ALPHA * x_ref[...] + _dot(m.astype(BF16), wo_ref[...])
    y = _layer_norm(h, g_ref[...], b_ref[...])
    xo_ref[...] = y
    xb_ref[...] = y.astype(BF16)


def _merge(o_nsa, o_pool, o_ssm_tm, bg, x, wn, wp, ws, wo, g, b, B, S, tm):
    nt = S // tm
    T, D = x.shape
    W = o_nsa.shape[1]
    rows = lambda bb, i: (bb * nt + i, 0)
    const = lambda shp: pl.BlockSpec(shp, lambda bb, i: tuple(0 for _ in shp))
    return pl.pallas_call(
        _merge_kernel,
        grid=(B, nt),
        in_specs=[pl.BlockSpec((tm, W), rows), pl.BlockSpec((tm, W), rows),
                  pl.BlockSpec((tm, W), lambda bb, i: (i, bb)),
                  pl.BlockSpec((tm, 3 * D), rows), pl.BlockSpec((tm, D), rows),
                  const(wn.shape), const(wp.shape), const(ws.shape), const(wo.shape),
                  const(g.shape), const(b.shape)],
        out_specs=[pl.BlockSpec((tm, D), rows), pl.BlockSpec((tm, D), rows)],
        out_shape=[jax.ShapeDtypeStruct((T, D), F32), jax.ShapeDtypeStruct((T, D), BF16)],
        compiler_params=_cparams("parallel", "parallel"),
    )(o_nsa, o_pool, o_ssm_tm, bg, x, wn, wp, ws, wo, g, b)


def _router_kernel(x_ref, wr_ref, br_ref, ri_ref, rw_ref, cnt_ref, carry):
    tm = x_ref.shape[0]

    @pl.when(pl.program_id(0) == 0)
    def _():
        carry[...] = jnp.zeros(carry.shape, F32)

    logits = jnp.dot(x_ref[...], wr_ref[...], preferred_element_type=F32,
                     precision=lax.Precision.HIGHEST) + br_ref[...]
    lane = lax.broadcasted_iota(jnp.int32, (tm, LANES), 1)
    big = jnp.int32(LANES)
    isg = lane < MOE_GROUPS
    lg = jnp.where(isg, logits, NEG_INF)
    eg = jnp.where(isg, jnp.exp(lg - jnp.max(lg, -1, keepdims=True)), 0.0)
    pgv = jnp.where(isg, eg / jnp.sum(eg, -1, keepdims=True), -1.0)
    p_g = jnp.max(pgv, -1, keepdims=True)
    g_sel = jnp.min(jnp.where(pgv == p_g, lane, big), -1, keepdims=True)
    ine = ((lane >= EXP_LANE0) & (lane < EXP_LANE0 + MOE_EXPERTS)
           & (jnp.right_shift(lane - EXP_LANE0, EPG_SHIFT) == g_sel))
    le = jnp.where(ine, logits, NEG_INF)
    ee = jnp.where(ine, jnp.exp(le - jnp.max(le, -1, keepdims=True)), 0.0)
    p = jnp.where(ine, ee / jnp.sum(ee, -1, keepdims=True), -1.0)
    p1 = jnp.max(p, -1, keepdims=True)
    i1 = jnp.min(jnp.where(p == p1, lane, big), -1, keepdims=True)
    prest = jnp.where(lane == i1, -1.0, p)
    p2 = jnp.max(prest, -1, keepdims=True)
    i2 = jnp.min(jnp.where(prest == p2, lane, big), -1, keepdims=True)
    den = p1 + p2
    w0 = p1 / den * p_g
    w1 = p2 / den * p_g
    e0 = i1 - EXP_LANE0
    e1 = i2 - EXP_LANE0
    is0 = lane == e0
    is1 = lane == e1
    onehot = jnp.where(is0 | is1, 1.0, 0.0)
    r = lax.broadcasted_iota(jnp.int32, (tm, tm), 0)
    c = lax.broadcasted_iota(jnp.int32, (tm, tm), 1)
    tri = jnp.where(r > c, 1.0, 0.0).astype(BF16)
    cum = _dot(tri, onehot.astype(BF16)) + carry[0:1, :]
    rank0 = jnp.sum(jnp.where(is0, cum, 0.0), -1, keepdims=True).astype(jnp.int32)
    rank1 = jnp.sum(jnp.where(is1, cum, 0.0), -1, keepdims=True).astype(jnp.int32)
    carry[...] = carry[...] + jnp.sum(onehot, 0, keepdims=True)
    cnt_ref[...] = carry[...]
    zi = jnp.zeros((tm, LANES), jnp.int32)
    ri_ref[...] = jnp.where(lane == 0, e0, jnp.where(lane == 1, e1, jnp.where(lane == 2, rank0, jnp.where(lane == 3, rank1, zi))))
    rw_ref[...] = jnp.where(lane == 0, w0, jnp.where(lane == 1, w1, 0.0))


def _router(x, wr, br, tm):
    T, D = x.shape
    return pl.pallas_call(
        _router_kernel,
        grid=(T // tm,),
        in_specs=[pl.BlockSpec((tm, D), lambda i: (i, 0)),
                  pl.BlockSpec(wr.shape, lambda i: (0, 0)), pl.BlockSpec(br.shape, lambda i: (0, 0))],
        out_specs=[pl.BlockSpec((tm, LANES), lambda i: (i, 0)), pl.BlockSpec((tm, LANES), lambda i: (i, 0)),
                   pl.BlockSpec((SUBLANES, LANES), lambda i: (0, 0))],
        out_shape=[jax.ShapeDtypeStruct((T, LANES), jnp.int32), jax.ShapeDtypeStruct((T, LANES), F32),
                   jax.ShapeDtypeStruct((SUBLANES, LANES), F32)],
        scratch_shapes=[pltpu.VMEM((SUBLANES, LANES), F32)],
        compiler_params=_cparams("arbitrary"),
    )(x, wr, br)


def _row_copy(src, s, dst, d, sem):
    return pltpu.make_async_copy(src.at[pl.ds(s, 1)], dst.at[pl.ds(d, 1)], sem)


def _dispatch_kernel(dest_ref, x_hbm, xs_in, xs_hbm, sem, *, tm):
    del xs_in
    t0 = pl.program_id(0) * tm

    def issue(r, c):
        for k in range(2):
            _row_copy(x_hbm, t0 + r, xs_hbm, dest_ref[0, 0, 2 * r + k], sem).start()
        return c

    lax.fori_loop(0, tm, issue, 0)

    def drain(r, c):
        _row_copy(x_hbm, 0, xs_hbm, 0, sem).wait()
        return c

    lax.fori_loop(0, 2 * tm, drain, 0)


def _dispatch(dest3, x, xs0, tm):
    T = x.shape[0]
    return pl.pallas_call(
        functools.partial(_dispatch_kernel, tm=tm),
        grid=(T // tm,),
        in_specs=[pl.BlockSpec((1, 1, 2 * tm), lambda i: (i, 0, 0), memory_space=pltpu.SMEM),
                  pl.BlockSpec(memory_space=pl.ANY), pl.BlockSpec(memory_space=pl.ANY)],
        out_specs=pl.BlockSpec(memory_space=pl.ANY),
        out_shape=jax.ShapeDtypeStruct(xs0.shape, xs0.dtype),
        scratch_shapes=[pltpu.SemaphoreType.DMA(())],
        input_output_aliases={2: 0},
        compiler_params=_cparams("arbitrary"),
    )(dest3, x, xs0)


def _expert_kernel(be_ref, nu_ref, xs_ref, wg_ref, wu_ref, wd_ref, ys_ref):
    i = pl.program_id(0)

    @pl.when(i < nu_ref[0])
    def _():
        x = xs_ref[...].astype(BF16)
        a = _dot(x, wg_ref[0])
        h = a * _sigmoid(a) * _dot(x, wu_ref[0])
        ys_ref[...] = _dot(h.astype(BF16), wd_ref[0])

    @pl.when(i >= nu_ref[0])
    def _():
        ys_ref[...] = jnp.zeros(ys_ref.shape, F32)


def _experts(blk_e, nused, xs, wg, wu, wd):
    P, D = xs.shape
    F = wg.shape[2]
    nblk = P // MOE_ROWS
    return pl.pallas_call(
        _expert_kernel,
        grid_spec=pltpu.PrefetchScalarGridSpec(
            num_scalar_prefetch=2, grid=(nblk,),
            in_specs=[pl.BlockSpec((MOE_ROWS, D), lambda i, be, nu: (i, 0)),
                      pl.BlockSpec((1, D, F), lambda i, be, nu: (be[i], 0, 0)),
                      pl.BlockSpec((1, D, F), lambda i, be, nu: (be[i], 0, 0)),
                      pl.BlockSpec((1, F, D), lambda i, be, nu: (be[i], 0, 0))],
            out_specs=pl.BlockSpec((MOE_ROWS, D), lambda i, be, nu: (i, 0))),
        out_shape=jax.ShapeDtypeStruct((P, D), F32),
        compiler_params=_cparams("arbitrary"),
    )(blk_e, nused, xs, wg, wu, wd)


def _combine_kernel(dest_ref, ys_hbm, x_ref, rw_ref, g_ref, b_ref, xo_ref, xb_ref, buf, sem, *, tm):
    def issue(r, c):
        for k in range(2):
            _row_copy(ys_hbm, dest_ref[0, 0, 2 * r + k], buf.at[k], r, sem).start()
        return c

    lax.fori_loop(0, tm, issue, 0)

    def drain(r, c):
        _row_copy(ys_hbm, 0, buf.at[0], 0, sem).wait()
        return c

    lax.fori_loop(0, 2 * tm, drain, 0)
    w = rw_ref[...]
    h = DN_ALPHA * x_ref[...] + w[:, 0:1] * buf[0] + w[:, 1:2] * buf[1]
    y = _layer_norm(h, g_ref[...], b_ref[...])
    xo_ref[...] = y
    xb_ref[...] = y.astype(BF16)


def _combine(dest3, ys, x, rw, g, b, tm):
    T, D = x.shape
    return pl.pallas_call(
        functools.partial(_combine_kernel, tm=tm),
        grid=(T // tm,),
        in_specs=[pl.BlockSpec((1, 1, 2 * tm), lambda i: (i, 0, 0), memory_space=pltpu.SMEM),
                  pl.BlockSpec(memory_space=pl.ANY),
                  pl.BlockSpec((tm, D), lambda i: (i, 0)), pl.BlockSpec((tm, LANES), lambda i: (i, 0)),
                  pl.BlockSpec(g.shape, lambda i: (0, 0)), pl.BlockSpec(b.shape, lambda i: (0, 0))],
        out_specs=[pl.BlockSpec((tm, D), lambda i: (i, 0)), pl.BlockSpec((tm, D), lambda i: (i, 0))],
        out_shape=[jax.ShapeDtypeStruct((T, D), F32), jax.ShapeDtypeStruct((T, D), BF16)],
        scratch_shapes=[pltpu.VMEM((2, tm, D), F32), pltpu.SemaphoreType.DMA(())],
        compiler_params=_cparams("arbitrary"),
    )(dest3, ys, x, rw, g, b)


def _prep_in_proj(w_in):
    L, D, _ = w_in.shape
    nw = HEADS * HEAD_DIM
    kvw = 6 * KV_GROUPS * HEAD_DIM
    o = 0
    wq = w_in[..., o:o + nw]; o += nw
    wkv = w_in[..., o:o + kvw]; o += kvw
    wng = w_in[..., o:o + 3 * HEADS]; o += 3 * HEADS
    wpool = w_in[..., o:o + D // 2]; o += D // 2
    wssm = w_in[..., o:o + D // 2]; o += D // 2
    wbg = w_in[..., o:]
    wq = wq.reshape(L, D, KV_GROUPS, GQA, HEAD_DIM) * (HEAD_DIM ** -0.5)
    z = jnp.zeros_like(wq[:, :, 0])
    wq_wide = jnp.stack([jnp.concatenate([wq[:, :, 0], z], -1), jnp.concatenate([z, wq[:, :, 1]], -1)], 2)
    wq_wide = wq_wide.reshape(L, D, HEADS * LANES)
    wng = jnp.pad(wng, ((0, 0), (0, 0), (0, LANES - 3 * HEADS)))
    w_a = jnp.concatenate([wq_wide, wkv], -1).astype(BF16)
    w_b = jnp.concatenate([wng, wpool, wbg], -1).astype(BF16)
    return w_a, w_b, wssm.astype(BF16)


def _prep_compress(pe_k, pe_v, w1k, w2k, w1v, w2v):
    L = pe_k.shape[0]
    half = CMP_BLOCK // 2
    cw = 2 * KV_GROUPS * HEAD_DIM
    w1 = jnp.zeros((L, half, 2, KV_GROUPS, HEAD_DIM, 2, 2, KV_GROUPS, CMP_HIDDEN), F32)
    pe = jnp.zeros((L, SUBLANES, half, 2, KV_GROUPS, HEAD_DIM), F32)
    w2 = jnp.zeros((L, 2, KV_GROUPS, CMP_HIDDEN, 2, KV_GROUPS, HEAD_DIM), F32)
    for s, (w1s, w2s, pes) in enumerate(((w1k, w2k, pe_k), (w1v, w2v, pe_v))):
        w1r = w1s.reshape(L, 2, half, HEAD_DIM, CMP_HIDDEN)
        per = pes.reshape(L, 2, half, HEAD_DIM)
        for g in range(KV_GROUPS):
            for part in range(2):
                w1 = w1.at[:, :, s, g, :, part, s, g, :].set(w1r[:, part])
                pe = pe.at[:, part, :, s, g, :].set(per[:, part])
            w2 = w2.at[:, s, g, :, s, g, :].set(w2s)
    w1 = w1.reshape(L, half * cw, 2 * 2 * KV_GROUPS * CMP_HIDDEN)
    pe = pe.reshape(L, SUBLANES, half * cw)
    w2 = w2.reshape(L, 2 * KV_GROUPS * CMP_HIDDEN, 2 * KV_GROUPS * HEAD_DIM)
    return w1.astype(BF16), pe.astype(BF16), w2.astype(BF16)


def _overlap(nc):
    i0 = jnp.arange(nc)[:, None] * CMP_STRIDE
    j0 = jnp.arange(LANES)[None, :] * SEL_BLOCK
    return ((i0 < j0 + SEL_BLOCK) & (i0 + CMP_BLOCK > j0)).astype(BF16)


def _prep_ssm(lam_re, lam_im, log_dt, b_re, b_im, c_re, c_im):
    L = lam_re.shape[0]
    dt = jnp.exp(log_dt)[..., None]
    mag = jnp.exp(lam_re * dt)
    ar, ai = mag * jnp.cos(lam_im * dt), mag * jnp.sin(lam_im * dt)
    den = lam_re * lam_re + lam_im * lam_im
    cr = ((ar - 1.0) * lam_re + ai * lam_im) / den
    ci = (ai * lam_re - (ar - 1.0) * lam_im) / den
    bbr = cr[..., None] * b_re - ci[..., None] * b_im
    bbi = cr[..., None] * b_im + ci[..., None] * b_re
    eye = jnp.eye(SSM_GROUPS, dtype=F32)
    bt = jnp.concatenate([jnp.einsum('lgph,gk->lghkp', m, eye).reshape(L, SSM_GROUPS * SSM_GROUP, SSM_NS)
                          for m in (bbr, bbi)], -1)
    cm = jnp.concatenate([jnp.einsum('lghp,gk->lgpkh', m, eye).reshape(L, SSM_NS, SSM_GROUPS * SSM_GROUP)
                          for m in (c_re, -c_im)], 1)
    a = jnp.concatenate([ar.reshape(L, 1, SSM_NS), ai.reshape(L, 1, SSM_NS)], -1)
    a = jnp.broadcast_to(a, (L, SUBLANES, 2 * SSM_NS))
    return bt.astype(BF16), a, cm.astype(BF16)


def _prep_router(w_grp, b_grp, w_exp, b_exp):
    L, D, _ = w_grp.shape
    wr = jnp.zeros((L, D, LANES), F32).at[:, :, 0:MOE_GROUPS].set(w_grp)
    wr = wr.at[:, :, EXP_LANE0:EXP_LANE0 + MOE_EXPERTS].set(w_exp)
    br = jnp.zeros((L, 1, LANES), F32).at[:, 0, 0:MOE_GROUPS].set(b_grp)
    br = br.at[:, 0, EXP_LANE0:EXP_LANE0 + MOE_EXPERTS].set(b_exp)
    return wr, br


def _layer(B, S, carry, w):
    x, xb = carry
    T, D = x.shape
    nc = S // CMP_STRIDE
    qw, kv = _proj(xb, w["w_a"], ((HEADS * LANES, BF16), (6 * LANES, BF16)), 512)
    ng, u_pool, bg = _proj(xb, w["w_b"], ((LANES, F32), (D // 2, F32), (3 * D, F32)), 256)
    u_ssm = _proj_time_major(xb, w["w_c"], B, S, 512).reshape(S * B, D // 2)
    kr = kv[:, 0:2 * LANES].reshape(T // CMP_STRIDE, CMP_STRIDE * 2 * LANES)
    kvc = _compress(kr, w["cw1"], w["cpe"], w["cw2"], B, nc)
    acc, sel = _nsa1(qw, kvc, ng, w["ov"], B, S)
    o_nsa = _nsa2(qw, kv, sel, ng, acc, B, S)
    o_pool = _pool(u_pool, w["pool_w"], w["pool_scale"], B, S, 512)
    o_ssm = _ssm(u_ssm, w["bt"], w["a"], w["cm"], w["d"], w["w_glu"], w["b_glu"], 128 * B)
    o_ssm = o_ssm.reshape(S, B * (D // 2))
    x, xb = _merge(o_nsa, o_pool, o_ssm, bg, x, w["wn"], w["wp"], w["ws"], w["wo"], w["ln1_g"], w["ln1_b"], B, S, 256)
    tm = 256
    ri, rw, cnt = _router(x, w["wr"], w["br"], 512)
    counts = cnt[0, 0:MOE_EXPERTS].astype(jnp.int32)
    padded = (counts + MOE_ROWS - 1) // MOE_ROWS * MOE_ROWS
    pend = jnp.cumsum(padded)
    pstart = pend - padded
    nblk = (2 * T) // MOE_ROWS + MOE_EXPERTS
    dest = pstart[ri[:, 0:2]] + ri[:, 2:4]
    dest3 = dest.reshape(T // tm, 1, 2 * tm)
    blk_e = jnp.minimum(jnp.searchsorted(pend, jnp.arange(nblk, dtype=jnp.int32) * MOE_ROWS, side='right'),
                        MOE_EXPERTS - 1).astype(jnp.int32)
    nused = (pend[-1:] // MOE_ROWS).astype(jnp.int32)
    xs = _dispatch(dest3, x, jnp.zeros((nblk * MOE_ROWS, D), F32), tm)
    ys = _experts(blk_e, nused, xs, w["wg"], w["wu"], w["wd"])
    x, xb = _combine(dest3, ys, x, rw, w["ln2_g"], w["ln2_b"], tm)
    return (x, xb), None


def kernel(x, w_in, cmp_pe_k, cmp_pe_v, cmp_w1_k, cmp_w2_k, cmp_w1_v, cmp_w2_v, pool_w, pool_scale, ssm_lam_re, ssm_lam_im, ssm_log_dt, ssm_b_re, ssm_b_im, ssm_c_re, ssm_c_im, ssm_d, ssm_w_glu, ssm_b_glu, w_up_nsa, w_up_pool, w_up_ssm, w_out, ln1_g, ln1_b, router_w_grp, router_b_grp, router_w_exp, router_b_exp, moe_w_gate, moe_w_up, moe_w_down, ln2_g, ln2_b):
    B, S, D = x.shape
    L = w_in.shape[0]
    assert B == 4 and D == D_MODEL and S % SEL_CHUNK == 0 and S // SEL_BLOCK <= LANES
    w_a, w_b, w_c = _prep_in_proj(w_in)
    cw1, cpe, cw2 = _prep_compress(cmp_pe_k, cmp_pe_v, cmp_w1_k, cmp_w2_k, cmp_w1_v, cmp_w2_v)
    bt, a, cm = _prep_ssm(ssm_lam_re, ssm_lam_im, ssm_log_dt, ssm_b_re, ssm_b_im, ssm_c_re, ssm_c_im)
    wr, br = _prep_router(router_w_grp, router_b_grp, router_w_exp, router_b_exp)
    wn = w_up_nsa.reshape(L, KV_GROUPS, GQA, HEAD_DIM, D).transpose(0, 2, 1, 3, 4).reshape(L, HEADS * HEAD_DIM, D)
    row = lambda v: v.reshape(L, 1, -1)
    ws = dict(
        w_a=w_a, w_b=w_b, w_c=w_c, cw1=cw1, cpe=cpe, cw2=cw2,
        ov=jnp.broadcast_to(_overlap(S // CMP_STRIDE), (L, S // CMP_STRIDE, LANES)),
        pool_w=pool_w.astype(BF16), pool_scale=row(pool_scale),
        bt=bt, a=a, cm=cm, d=row(ssm_d), w_glu=ssm_w_glu.astype(BF16), b_glu=row(ssm_b_glu),
        wn=wn.astype(BF16), wp=w_up_pool.astype(BF16), ws=w_up_ssm.astype(BF16), wo=w_out.astype(BF16),
        ln1_g=row(ln1_g), ln1_b=row(ln1_b), wr=wr, br=br,
        wg=moe_w_gate.astype(BF16), wu=moe_w_up.astype(BF16), wd=moe_w_down.astype(BF16),
        ln2_g=row(ln2_g), ln2_b=row(ln2_b))
    xf = x.reshape(B * S, D)
    (xf, _), _ = lax.scan(functools.partial(_layer, B, S), (xf, xf.astype(BF16)), ws)
    return xf.reshape(B, S, D)
```

```python
import functools
import math

import jax
import jax.numpy as jnp
from jax import lax
from jax.experimental import pallas as pl
from jax.experimental.pallas import tpu as pltpu

F32 = jnp.float32
BF16 = jnp.bfloat16

D_MODEL = 1024
DEPTH = 4
HEAD_DIM = 64
HEADS = 8
KV_GROUPS = 2
GQA = HEADS // KV_GROUPS
CMP_BLOCK = 32
CMP_STRIDE = 16
CMP_HIDDEN = 128
SEL_BLOCK = 64
SEL_TOPN = 16
WINDOW = 512
FORCE_SCORE = 1.0e4
NEG_INF = -1.0e30
POOL_WINDOWS = (2, 4, 8, 16)
POOL_HALO = 16
SSM_GROUP = 16
SSM_GROUPS = 32
SSM_STATE = 64
SSM_NS = SSM_GROUPS * SSM_STATE
MOE_GROUPS = 4
MOE_EPG = 8
MOE_EXPERTS = 32
MOE_FF = 512
DN_ALPHA = (2 * DEPTH) ** 0.25
LN_EPS = 1e-5

LANES = 128
SUBLANES = 8
VMEM_LIMIT = 48 * 1024 * 1024
Q_TILE = 128
SEL_CHUNK = 512
MOE_ROWS = 256
EXP_LANE0 = 32
SEL_SHIFT = SEL_BLOCK.bit_length() - 1
EPG_SHIFT = MOE_EPG.bit_length() - 1


def _cparams(*sem):
    return pltpu.CompilerParams(dimension_semantics=sem, vmem_limit_bytes=VMEM_LIMIT)


def _gelu(x):
    return 0.5 * x * (1.0 + jnp.tanh(math.sqrt(2.0 / math.pi) * (x + 0.044715 * x * x * x)))


def _sigmoid(x):
    return 1.0 / (1.0 + jnp.exp(-x))


def _dot(a, b):
    return jnp.dot(a, b, preferred_element_type=F32)


def _dot_nt(a, b):
    return lax.dot_general(a, b, (((1,), (1,)), ((), ())), preferred_element_type=F32)


def _layer_norm(h, g, b):
    mu = jnp.mean(h, -1, keepdims=True)
    c = h - mu
    var = jnp.mean(c * c, -1, keepdims=True)
    return c * lax.rsqrt(var + LN_EPS) * g + b


def _proj_kernel(x_ref, w_ref, *out_refs, offs):
    x = x_ref[...]
    for o_ref, (a, n) in zip(out_refs, offs):
        o_ref[...] = _dot(x, w_ref[:, a:a + n]).astype(o_ref.dtype)


def _proj(xb, w, outs, tm):
    T, K = xb.shape
    N = w.shape[1]
    offs, a = [], 0
    for n, _ in outs:
        offs.append((a, n))
        a += n
    assert a == N and T % tm == 0
    return pl.pallas_call(
        functools.partial(_proj_kernel, offs=tuple(offs)),
        grid=(T // tm,),
        in_specs=[pl.BlockSpec((tm, K), lambda i: (i, 0)), pl.BlockSpec((K, N), lambda i: (0, 0))],
        out_specs=[pl.BlockSpec((tm, n), lambda i: (i, 0)) for n, _ in outs],
        out_shape=[jax.ShapeDtypeStruct((T, n), dt) for n, dt in outs],
        compiler_params=_cparams("parallel"),
    )(xb, w)


def _proj_tm_kernel(x_ref, w_ref, o_ref):
    o_ref[...] = _dot(x_ref[...], w_ref[...])


def _proj_time_major(xb, w, B, S, tm):
    K = xb.shape[1]
    N = w.shape[1]
    nt = S // tm
    return pl.pallas_call(
        _proj_tm_kernel,
        grid=(B, nt),
        in_specs=[pl.BlockSpec((tm, K), lambda b, i: (b * nt + i, 0)), pl.BlockSpec((K, N), lambda b, i: (0, 0))],
        out_specs=pl.BlockSpec((tm, N), lambda b, i: (i, b)),
        out_shape=jax.ShapeDtypeStruct((S, B * N), F32),
        compiler_params=_cparams("parallel", "parallel"),
    )(xb, w)


def _compress_kernel(kr_ref, w1_ref, pe_ref, w2_ref, out_ref):
    nc = kr_ref.shape[0]
    hw = 4 * CMP_HIDDEN
    a = _dot(kr_ref[...], w1_ref[...])
    pb = _dot(pe_ref[...], w1_ref[...])
    second = pltpu.roll(a[:, hw:], nc - 1, 0)
    row = lax.broadcasted_iota(jnp.int32, (nc, hw), 0)
    second = jnp.where(row < nc - 1, second, 0.0)
    hid = _gelu(a[:, :hw] + second + pb[0:1, :hw] + pb[1:2, hw:])
    out_ref[0] = _dot(hid.astype(BF16), w2_ref[...]).astype(out_ref.dtype)


def _compress(kr, w1, pe_in, w2, B, nc):
    return pl.pallas_call(
        _compress_kernel,
        grid=(B,),
        in_specs=[pl.BlockSpec((nc, kr.shape[1]), lambda b: (b, 0)),
                  pl.BlockSpec(w1.shape, lambda b: (0, 0)),
                  pl.BlockSpec(pe_in.shape, lambda b: (0, 0)),
                  pl.BlockSpec(w2.shape, lambda b: (0, 0))],
        out_specs=pl.BlockSpec((1, nc, 2 * LANES), lambda b: (b, 0, 0)),
        out_shape=jax.ShapeDtypeStruct((B, nc, 2 * LANES), BF16),
        compiler_params=_cparams("parallel"),
    )(kr, w1, pe_in, w2)


def _slope(h):
    return 2.0 ** (-8.0 * (h + 1) / HEADS)


def _stack_heads(q, g):
    return jnp.concatenate([q[:, (g * GQA + r) * LANES:(g * GQA + r + 1) * LANES] for r in range(GQA)], axis=0)


def _split3_dot(a, b):
    hi = a.astype(BF16)
    r1 = a - hi.astype(F32)
    mid = r1.astype(BF16)
    lo = (r1 - mid.astype(F32)).astype(BF16)
    return _dot(hi, b) + _dot(mid, b) + _dot(lo, b)


def _nsa1_kernel(q_ref, kvc_ref, ng_ref, ov_ref, qaux_ref, caux_ref, acc_ref, sel_ref, un_ref, sc_ref):
    Q = q_ref.shape[0]
    nc = kvc_ref.shape[1]
    t0 = pl.program_id(1) * Q
    q = q_ref[...]
    kc = jnp.concatenate([kvc_ref[0, :, 0:LANES], caux_ref[...]], axis=1)
    vc = kvc_ref[0, :, LANES:2 * LANES]
    gate = _sigmoid(ng_ref[...])
    qrow = lax.broadcasted_iota(jnp.int32, (Q, nc), 0)
    ccol = lax.broadcasted_iota(jnp.int32, (Q, nc), 1)
    bias = jnp.where((t0 + qrow) >= (ccol * CMP_STRIDE + (CMP_BLOCK - 1)), 0.0, NEG_INF)
    bias4 = jnp.concatenate([bias] * GQA, axis=0)
    lane = lax.broadcasted_iota(jnp.int32, (Q, LANES), 1)
    tb = jnp.right_shift(t0 + lax.broadcasted_iota(jnp.int32, (Q, LANES), 0), SEL_SHIFT)
    jrow = lax.broadcasted_iota(jnp.int32, (LANES, Q), 0)
    nvalid = (t0 + Q - 1) // SEL_BLOCK + 1
    outs = []
    for g in range(KV_GROUPS):
        qs = jnp.concatenate([_stack_heads(q, g), qaux_ref[g]], axis=1)
        s = _dot_nt(qs, kc) + bias4
        m = jnp.max(s, -1, keepdims=True)
        e = jnp.exp(s - m)
        inv = jnp.where(m > 0.5 * NEG_INF, 1.0 / jnp.maximum(jnp.sum(e, -1, keepdims=True), 1e-30), 0.0)
        p = e * inv
        oc = _dot(p.astype(BF16), vc)
        psum = p[0:Q]
        for r in range(1, GQA):
            psum = psum + p[r * Q:(r + 1) * Q]
        outs.append([gate[:, 3 * (g * GQA + r):3 * (g * GQA + r) + 1] * oc[r * Q:(r + 1) * Q] for r in range(GQA)])
        imp = _split3_dot(psum, ov_ref[...])
        forced = (lane == 0) | (lane == tb) | (lane == tb - 1)
        score = jnp.where(forced, FORCE_SCORE, jnp.where(lane <= tb, imp, -1.0))
        sc_ref[...] = score.T
        st = sc_ref[...]

        def rank_step(i, cnt, st=st):
            for jp in (2 * i, 2 * i + 1):
                rowb = jnp.broadcast_to(sc_ref[pl.ds(jp, 1), :], st.shape)
                ge = jnp.where(rowb >= st, 1.0, 0.0)
                gt = jnp.where(rowb > st, 1.0, 0.0)
                cnt = cnt + jnp.where(jrow > jp, ge, gt)
            return cnt

        cnt = lax.fori_loop(0, nvalid // 2, rank_step, jnp.zeros((LANES, Q), F32))
        sel = jnp.where(cnt < SEL_TOPN, 1.0, 0.0).T
        sel = jnp.where(lane <= tb, sel, 0.0)
        sel_ref[:, g * LANES:(g + 1) * LANES] = sel.astype(sel_ref.dtype)
        un_ref[:, g * LANES:(g + 1) * LANES] = jnp.broadcast_to(jnp.max(sel, 0, keepdims=True), (SUBLANES, LANES))
    for r in range(GQA):
        acc_ref[:, r * LANES:(r + 1) * LANES] = jnp.where(lane < HEAD_DIM, outs[0][r], outs[1][r])


def _nsa1(qw, kvc, ng, ov, qaux, caux, B, S):
    nq = S // Q_TILE
    nc = kvc.shape[1]
    T = B * S
    rows = lambda b, i: (b * nq + i, 0)
    return pl.pallas_call(
        _nsa1_kernel,
        grid=(B, nq),
        in_specs=[pl.BlockSpec((Q_TILE, HEADS * LANES), rows),
                  pl.BlockSpec((1, nc, 2 * LANES), lambda b, i: (b, 0, 0)),
                  pl.BlockSpec((Q_TILE, LANES), rows),
                  pl.BlockSpec((nc, LANES), lambda b, i: (0, 0)),
                  pl.BlockSpec(qaux.shape, lambda b, i: (0, 0, 0)),
                  pl.BlockSpec((nc, LANES), lambda b, i: (0, 0))],
        out_specs=[pl.BlockSpec((Q_TILE, GQA * LANES), rows),
                   pl.BlockSpec((Q_TILE, KV_GROUPS * LANES), rows),
                   pl.BlockSpec((SUBLANES, KV_GROUPS * LANES), rows)],
        out_shape=[jax.ShapeDtypeStruct((T, GQA * LANES), F32),
                   jax.ShapeDtypeStruct((T, KV_GROUPS * LANES), BF16),
                   jax.ShapeDtypeStruct((B * nq * SUBLANES, KV_GROUPS * LANES), F32)],
        scratch_shapes=[pltpu.VMEM((LANES, Q_TILE), F32)],
        compiler_params=_cparams("parallel", "parallel"),
    )(qw, kvc, ng, ov, qaux, caux)


def _nsa2_kernel(lst_ref, cnt_ref, q_ref, ks_ref, vs_ref, kw_ref, vw_ref, sel_ref, ng_ref, acc_ref,
                 qaux_ref, kaux_ref, ex_ref, o_ref, m_sc, l_sc, a_sc, ow_sc, s0_sc, s1_sc, sd_sc):
    Q = q_ref.shape[0]
    C = SEL_CHUNK
    WK = WINDOW + Q
    maxch = ex_ref.shape[1] // C
    tile = pl.program_id(0) * pl.num_programs(1) + pl.program_id(1)
    t0 = pl.program_id(1) * Q
    q = q_ref[...]
    gate = _sigmoid(ng_ref[...])
    lane = lax.broadcasted_iota(jnp.int32, (Q, LANES), 1)
    cdiag = t0 // C
    start = pl.multiple_of(jnp.maximum(t0 - WINDOW, 0), Q)
    wrow = lax.broadcasted_iota(jnp.int32, (Q, WK), 0)
    wcol = lax.broadcasted_iota(jnp.int32, (Q, WK), 1)
    wdist = (t0 + wrow) - (start + wcol)
    wbias = jnp.where((wdist >= 0) & (wdist < WINDOW), 0.0, NEG_INF)
    wbias4 = jnp.concatenate([wbias] * GQA, axis=0)
    qrow = lax.broadcasted_iota(jnp.int32, (Q, C), 0)
    kcol = lax.broadcasted_iota(jnp.int32, (Q, C), 1)
    outs = []
    for g in range(KV_GROUPS):
        qs = jnp.concatenate([_stack_heads(q, g), qaux_ref[g]], axis=1)
        selg = sel_ref[:, g * LANES:(g + 1) * LANES]
        m_sc[...] = jnp.full(m_sc.shape, NEG_INF, F32)
        l_sc[...] = jnp.zeros(l_sc.shape, F32)
        a_sc[...] = jnp.zeros(a_sc.shape, F32)

        def scores(c, s_out, qs=qs, selg=selg):
            k0 = pl.multiple_of(c * C, C)
            kk = jnp.concatenate([ks_ref[pl.ds(k0, C), :], kaux_ref[pl.ds(k0, C), :]], axis=1)
            valid = (_dot(selg, ex_ref[:, pl.ds(k0, C)]) > 0.5) & ((t0 + qrow) >= (k0 + kcol))
            bias = jnp.where(valid, 0.0, NEG_INF)
            s_out[...] = _dot_nt(qs, kk) + jnp.concatenate([bias] * GQA, axis=0)

        def accumulate(c, s_in):
            k0 = pl.multiple_of(c * C, C)
            s = s_in[...]
            m_old = m_sc[...]
            m_new = jnp.maximum(m_old, jnp.max(s, -1, keepdims=True))
            alpha = jnp.exp(m_old - m_new)
            e = jnp.exp(s - m_new[:, 0:1])
            l_sc[...] = alpha * l_sc[...] + jnp.sum(e, -1, keepdims=True)
            a_sc[...] = alpha * a_sc[...] + _dot(e.astype(BF16), vs_ref[pl.ds(k0, C), :])
            m_sc[...] = m_new

        base = (tile * KV_GROUPS + g) * maxch
        scores(cdiag, sd_sc)
        scores(lst_ref[base], s0_sc)
        kw = jnp.concatenate([kw_ref[pl.ds(start, WK), :], kaux_ref[pl.ds(start, WK), :]], axis=1)
        sw = _dot_nt(qs, kw) + wbias4
        e = jnp.exp(sw - jnp.max(sw, -1, keepdims=True))
        ow_sc[...] = _dot(e.astype(BF16), vw_ref[pl.ds(start, WK), :]) / jnp.maximum(jnp.sum(e, -1, keepdims=True), 1e-30)

        def step(j, carry, scores=scores, accumulate=accumulate, base=base):
            cur = lst_ref[base + j]
            nxt = lst_ref[base + j + 1]

            @pl.when(j % 2 == 0)
            def _():
                scores(nxt, s1_sc)
                accumulate(cur, s0_sc)

            @pl.when(j % 2 == 1)
            def _():
                scores(nxt, s0_sc)
                accumulate(cur, s1_sc)

            return carry

        lax.fori_loop(0, cnt_ref[tile * KV_GROUPS + g], step, 0)
        accumulate(cdiag, sd_sc)
        o_w = ow_sc[...]
        o_s = a_sc[...] / jnp.maximum(l_sc[...], 1e-30)
        og = []
        for r in range(GQA):
            h = g * GQA + r
            rows = slice(r * Q, (r + 1) * Q)
            og.append(gate[:, 3 * h + 1:3 * h + 2] * o_s[rows] + gate[:, 3 * h + 2:3 * h + 3] * o_w[rows])
        outs.append(og)
    for r in range(GQA):
        cols = slice(r * LANES, (r + 1) * LANES)
        o_ref[:, cols] = (acc_ref[:, cols] + jnp.where(lane < HEAD_DIM, outs[0][r], outs[1][r])).astype(o_ref.dtype)


def _chunk_lists(un, B, S):
    nq = S // Q_TILE
    maxch = S // SEL_CHUNK
    per = SEL_CHUNK // SEL_BLOCK
    un = un.reshape(B * nq, SUBLANES, KV_GROUPS, LANES)[:, 0, :, :maxch * per]
    flags = un.reshape(B * nq, KV_GROUPS, maxch, per).max(-1) > 0.5
    cdiag = (jnp.arange(B * nq, dtype=jnp.int32) % nq) * Q_TILE // SEL_CHUNK
    flags = flags & (jnp.arange(maxch, dtype=jnp.int32)[None, None, :] < cdiag[:, None, None])
    order = jnp.argsort(jnp.logical_not(flags), axis=-1, stable=True)
    return order.astype(jnp.int32).reshape(-1), flags.sum(-1).astype(jnp.int32).reshape(-1)


def _nsa2(lst, cnt, qw, kv, sel, ng, acc, qaux, kaux, ex, B, S):
    nq = S // Q_TILE
    T = B * S
    rows = lambda b, i, *_: (b * nq + i, 0)
    kvspec = lambda col: pl.BlockSpec((S, LANES), lambda b, i, *_, col=col: (b, col))
    const = lambda shp: pl.BlockSpec(shp, lambda b, i, *_: tuple(0 for _ in shp))
    return pl.pallas_call(
        _nsa2_kernel,
        grid_spec=pltpu.PrefetchScalarGridSpec(
            num_scalar_prefetch=2, grid=(B, nq),
            in_specs=[pl.BlockSpec((Q_TILE, HEADS * LANES), rows),
                      kvspec(2), kvspec(3), kvspec(4), kvspec(5),
                      pl.BlockSpec((Q_TILE, KV_GROUPS * LANES), rows),
                      pl.BlockSpec((Q_TILE, LANES), rows),
                      pl.BlockSpec((Q_TILE, GQA * LANES), rows),
                      const(qaux.shape), const(kaux.shape), const(ex.shape)],
            out_specs=pl.BlockSpec((Q_TILE, GQA * LANES), rows),
            scratch_shapes=[pltpu.VMEM((GQA * Q_TILE, LANES), F32)] * 4
            + [pltpu.VMEM((GQA * Q_TILE, SEL_CHUNK), F32)] * 3),
        out_shape=jax.ShapeDtypeStruct((T, GQA * LANES), BF16),
        compiler_params=_cparams("parallel", "parallel"),
    )(lst, cnt, qw, kv, kv, kv, kv, sel, ng, acc, qaux, kaux, ex)


def _nsa_consts(S):
    h = jnp.arange(HEADS, dtype=F32).reshape(KV_GROUPS, GQA, 1, 1)
    slope = jnp.exp2(-8.0 * (h + 1.0) / HEADS)
    lane = jnp.arange(LANES)[None, None, None, :]
    qaux = jnp.where(lane == 0, slope * SEL_BLOCK, jnp.where(lane == 1, slope, 0.0))
    qaux = jnp.broadcast_to(qaux, (KV_GROUPS, GQA, Q_TILE, LANES)).reshape(KV_GROUPS, GQA * Q_TILE, LANES)

    def pos_cols(pos):
        l2 = jnp.arange(LANES)[None, :]
        return jnp.where(l2 == 0, pos[:, None] // SEL_BLOCK, jnp.where(l2 == 1, pos[:, None] % SEL_BLOCK, 0))

    kaux = pos_cols(jnp.arange(S))
    caux = pos_cols(jnp.arange(S // CMP_STRIDE) * CMP_STRIDE + (CMP_BLOCK - 1))
    ex = (jnp.arange(LANES)[:, None] == (jnp.arange(S)[None, :] // SEL_BLOCK))
    return qaux.astype(BF16), kaux.astype(BF16), caux.astype(BF16), ex.astype(BF16)


def _pool_kernel(u_ref, halo_ref, w_ref, sc_ref, o_ref, x_sc):
    ts = u_ref.shape[0]
    i = pl.program_id(1)
    x_sc[0:POOL_HALO] = jnp.where(i > 0, halo_ref[...], 0.0)
    x_sc[POOL_HALO:POOL_HALO + ts] = u_ref[...]
    t = i * ts + lax.broadcasted_iota(jnp.int32, (ts, LANES), 0)
    for g, w in enumerate(POOL_WINDOWS):
        cols = slice(g * LANES, (g + 1) * LANES)
        cur = x_sc[POOL_HALO:POOL_HALO + ts, cols]
        acc = cur
        for k in range(1, w):
            acc = acc + x_sc[POOL_HALO - k:POOL_HALO - k + ts, cols]
        cnt = jnp.minimum(t + 1, w).astype(F32)
        pooled = acc / cnt - cur
        o_ref[:, cols] = (_dot(pooled.astype(BF16), w_ref[g]) * sc_ref[:, cols]).astype(o_ref.dtype)


def _pool(u, pool_w, pool_scale, B, S, ts):
    nt = S // ts
    hb = ts // POOL_HALO
    C = u.shape[1]
    return pl.pallas_call(
        _pool_kernel,
        grid=(B, nt),
        in_specs=[pl.BlockSpec((ts, C), lambda b, i: (b * nt + i, 0)),
                  pl.BlockSpec((POOL_HALO, C), lambda b, i: (jnp.maximum((b * nt + i) * hb - 1, 0), 0)),
                  pl.BlockSpec(pool_w.shape, lambda b, i: (0, 0, 0)),
                  pl.BlockSpec((1, C), lambda b, i: (0, 0))],
        out_specs=pl.BlockSpec((ts, C), lambda b, i: (b * nt + i, 0)),
        out_shape=jax.ShapeDtypeStruct((B * S, C), BF16),
        scratch_shapes=[pltpu.VMEM((POOL_HALO + ts, C), F32)],
        compiler_params=_cparams("parallel", "parallel"),
    )(u, u, pool_w, pool_scale)


SSM_PASS = 8


def _ssm_kernel(u_ref, bt_ref, a_ref, c_ref, d_ref, wg_ref, bg_ref, o_ref, xh, hst):
    rows = u_ref.shape[0]

    @pl.when(pl.program_id(0) == 0)
    def _():
        hst[...] = jnp.zeros(hst.shape, F32)

    u = u_ref[...]
    xh[...] = _dot(u.astype(BF16), bt_ref[...])
    low = lax.broadcasted_iota(jnp.int32, (SUBLANES, LANES), 0) < 4
    nchunk = SSM_NS // LANES
    for p in range(nchunk // SSM_PASS):
        cols = [p * SSM_PASS + j for j in range(SSM_PASS)]
        re = [slice(c * LANES, (c + 1) * LANES) for c in cols]
        im = [slice(SSM_NS + c * LANES, SSM_NS + (c + 1) * LANES) for c in cols]
        ar = [a_ref[:, s] for s in re]
        ai = [a_ref[:, s] for s in im]

        def tile(k, carry, re=re, im=im, ar=ar, ai=ai):
            cr, ci = carry
            r0 = pl.multiple_of(k * SUBLANES, SUBLANES)
            ncr, nci = [], []
            for j in range(SSM_PASS):
                vr = xh[pl.ds(r0, SUBLANES), re[j]]
                vi = xh[pl.ds(r0, SUBLANES), im[j]]
                t1r = ar[j] * cr[j] - ai[j] * ci[j] + vr
                t1i = ar[j] * ci[j] + ai[j] * cr[j] + vi
                pr = pltpu.roll(t1r, 4, 0)
                pi = pltpu.roll(t1i, 4, 0)
                t2r = ar[j] * pr - ai[j] * pi + vr
                t2i = ar[j] * pi + ai[j] * pr + vi
                xh[pl.ds(r0, SUBLANES), re[j]] = jnp.where(low, t1r, t2r)
                xh[pl.ds(r0, SUBLANES), im[j]] = jnp.where(low, t1i, t2i)
                ncr.append(pltpu.roll(t2r, 4, 0))
                nci.append(pltpu.roll(t2i, 4, 0))
            return tuple(ncr), tuple(nci)

        cr0 = tuple(hst[:, s] for s in re)
        ci0 = tuple(hst[:, s] for s in im)
        cr, ci = lax.fori_loop(0, rows // SUBLANES, tile, (cr0, ci0))
        for j in range(SSM_PASS):
            hst[:, re[j]] = cr[j]
            hst[:, im[j]] = ci[j]
    y = _dot(xh[...].astype(BF16), c_ref[...]) + d_ref[...] * u
    z = _gelu(y)
    o_ref[...] = (z * _sigmoid(_dot(z.astype(BF16), wg_ref[...]) + bg_ref[...])).astype(o_ref.dtype)


def _ssm(u_tm, bt, a_b, cmat, d_skip, w_glu, b_glu, rows):
    n, C = u_tm.shape
    const = lambda shp: pl.BlockSpec(shp, lambda i: tuple(0 for _ in shp))
    return pl.pallas_call(
        _ssm_kernel,
        grid=(n // rows,),
        in_specs=[pl.BlockSpec((rows, C), lambda i: (i, 0)),
                  const(bt.shape), const(a_b.shape), const(cmat.shape), const(d_skip.shape),
                  const(w_glu.shape), const(b_glu.shape)],
        out_specs=pl.BlockSpec((rows, C), lambda i: (i, 0)),
        out_shape=jax.ShapeDtypeStruct((n, C), BF16),
        scratch_shapes=[pltpu.VMEM((rows, 2 * SSM_NS), F32), pltpu.VMEM((SUBLANES, 2 * SSM_NS), F32)],
        compiler_params=_cparams("arbitrary"),
    )(u_tm, bt, a_b, cmat, d_skip, w_glu, b_glu)


def _merge_kernel(on_ref, op_ref, os_ref, bg_ref, x_ref, wn_ref, wp_ref, ws_ref, wo_ref, g_ref, b_ref, xo_ref, xb_ref):
    D = x_ref.shape[1]
    gates = _sigmoid(bg_ref[...])
    m = (gates[:, 0:D] * _dot(on_ref[...], wn_ref[...])
         + gates[:, D:2 * D] * _dot(op_ref[...], wp_ref[...])
         + gates[:, 2 * D:3 * D] * _dot(os_ref[...], ws_ref[...]))
    h = DN_ALPHA * x_ref[...] + _dot(m.astype(BF16), wo_ref[...])
    y = _layer_norm(h, g_ref[...], b_ref[...])
    xo_ref[...] = y
    xb_ref[...] = y.astype(BF16)


def _merge(o_nsa, o_pool, o_ssm_tm, bg, x, wn, wp, ws, wo, g, b, B, S, tm):
    nt = S // tm
    T, D = x.shape
    W = o_nsa.shape[1]
    rows = lambda bb, i: (bb * nt + i, 0)
    const = lambda shp: pl.BlockSpec(shp, lambda bb, i: tuple(0 for _ in shp))
    return pl.pallas_call(
        _merge_kernel,
        grid=(B, nt),
        in_specs=[pl.BlockSpec((tm, W), rows), pl.BlockSpec((tm, W), rows),
                  pl.BlockSpec((tm, W), lambda bb, i: (i, bb)),
                  pl.BlockSpec((tm, 3 * D), rows), pl.BlockSpec((tm, D), rows),
                  const(wn.shape), const(wp.shape), const(ws.shape), const(wo.shape),
                  const(g.shape), const(b.shape)],
        out_specs=[pl.BlockSpec((tm, D), rows), pl.BlockSpec((tm, D), rows)],
        out_shape=[jax.ShapeDtypeStruct((T, D), F32), jax.ShapeDtypeStruct((T, D), BF16)],
        compiler_params=_cparams("parallel", "parallel"),
    )(o_nsa, o_pool, o_ssm_tm, bg, x, wn, wp, ws, wo, g, b)


def _router_kernel(x_ref, wr_ref, br_ref, ri_ref, rw_ref, cnt_ref, carry):
    tm = x_ref.shape[0]

    @pl.when(pl.program_id(0) == 0)
    def _():
        carry[...] = jnp.zeros(carry.shape, F32)

    logits = jnp.dot(x_ref[...], wr_ref[...], preferred_element_type=F32,
                     precision=lax.Precision.HIGHEST) + br_ref[...]
    lane = lax.broadcasted_iota(jnp.int32, (tm, LANES), 1)
    big = jnp.int32(LANES)
    isg = lane < MOE_GROUPS
    lg = jnp.where(isg, logits, NEG_INF)
    eg = jnp.where(isg, jnp.exp(lg - jnp.max(lg, -1, keepdims=True)), 0.0)
    pgv = jnp.where(isg, eg / jnp.sum(eg, -1, keepdims=True), -1.0)
    p_g = jnp.max(pgv, -1, keepdims=True)
    g_sel = jnp.min(jnp.where(pgv == p_g, lane, big), -1, keepdims=True)
    ine = ((lane >= EXP_LANE0) & (lane < EXP_LANE0 + MOE_EXPERTS)
           & (jnp.right_shift(lane - EXP_LANE0, EPG_SHIFT) == g_sel))
    le = jnp.where(ine, logits, NEG_INF)
    ee = jnp.where(ine, jnp.exp(le - jnp.max(le, -1, keepdims=True)), 0.0)
    p = jnp.where(ine, ee / jnp.sum(ee, -1, keepdims=True), -1.0)
    p1 = jnp.max(p, -1, keepdims=True)
    i1 = jnp.min(jnp.where(p == p1, lane, big), -1, keepdims=True)
    prest = jnp.where(lane == i1, -1.0, p)
    p2 = jnp.max(prest, -1, keepdims=True)
    i2 = jnp.min(jnp.where(prest == p2, lane, big), -1, keepdims=True)
    den = p1 + p2
    w0 = p1 / den * p_g
    w1 = p2 / den * p_g
    e0 = i1 - EXP_LANE0
    e1 = i2 - EXP_LANE0
    is0 = lane == e0
    is1 = lane == e1
    onehot = jnp.where(is0 | is1, 1.0, 0.0)
    r = lax.broadcasted_iota(jnp.int32, (tm, tm), 0)
    c = lax.broadcasted_iota(jnp.int32, (tm, tm), 1)
    tri = jnp.where(r > c, 1.0, 0.0).astype(BF16)
    cum = _dot(tri, onehot.astype(BF16)) + carry[0:1, :]
    rank0 = jnp.sum(jnp.where(is0, cum, 0.0), -1, keepdims=True).astype(jnp.int32)
    rank1 = jnp.sum(jnp.where(is1, cum, 0.0), -1, keepdims=True).astype(jnp.int32)
    carry[...] = carry[...] + jnp.sum(onehot, 0, keepdims=True)
    cnt_ref[...] = carry[...]
    zi = jnp.zeros((tm, LANES), jnp.int32)
    ri_ref[...] = jnp.where(lane == 0, e0, jnp.where(lane == 1, e1, jnp.where(lane == 2, rank0, jnp.where(lane == 3, rank1, zi))))
    rw_ref[...] = jnp.where(lane == 0, w0, jnp.where(lane == 1, w1, 0.0))


def _router(x, wr, br, tm):
    T, D = x.shape
    return pl.pallas_call(
        _router_kernel,
        grid=(T // tm,),
        in_specs=[pl.BlockSpec((tm, D), lambda i: (i, 0)),
                  pl.BlockSpec(wr.shape, lambda i: (0, 0)), pl.BlockSpec(br.shape, lambda i: (0, 0))],
        out_specs=[pl.BlockSpec((tm, LANES), lambda i: (i, 0)), pl.BlockSpec((tm, LANES), lambda i: (i, 0)),
                   pl.BlockSpec((SUBLANES, LANES), lambda i: (0, 0))],
        out_shape=[jax.ShapeDtypeStruct((T, LANES), jnp.int32), jax.ShapeDtypeStruct((T, LANES), F32),
                   jax.ShapeDtypeStruct((SUBLANES, LANES), F32)],
        scratch_shapes=[pltpu.VMEM((SUBLANES, LANES), F32)],
        compiler_params=_cparams("arbitrary"),
    )(x, wr, br)


def _row_copy(src, s, dst, d, sem):
    return pltpu.make_async_copy(src.at[pl.ds(s, 1)], dst.at[pl.ds(d, 1)], sem)


def _dispatch_kernel(dest_ref, x_ref, xs_in, xs_hbm, sem, *, tm):
    del xs_in

    def issue(r, c):
        for k in range(2):
            _row_copy(x_ref, r, xs_hbm, dest_ref[0, 0, 2 * r + k], sem).start()
        return c

    lax.fori_loop(0, tm, issue, 0)

    def drain(r, c):
        _row_copy(x_ref, 0, xs_hbm, 0, sem).wait()
        return c

    lax.fori_loop(0, 2 * tm, drain, 0)


def _dispatch(dest3, x, xs0, tm):
    T = x.shape[0]
    return pl.pallas_call(
        functools.partial(_dispatch_kernel, tm=tm),
        grid=(T // tm,),
        in_specs=[pl.BlockSpec((1, 1, 2 * tm), lambda i: (i, 0, 0), memory_space=pltpu.SMEM),
                  pl.BlockSpec((tm, x.shape[1]), lambda i: (i, 0)), pl.BlockSpec(memory_space=pl.ANY)],
        out_specs=pl.BlockSpec(memory_space=pl.ANY),
        out_shape=jax.ShapeDtypeStruct(xs0.shape, xs0.dtype),
        scratch_shapes=[pltpu.SemaphoreType.DMA(())],
        input_output_aliases={2: 0},
        compiler_params=_cparams("arbitrary"),
    )(dest3, x, xs0)


def _expert_kernel(be_ref, nu_ref, xs_ref, wg_ref, wu_ref, wd_ref, ys_ref):
    i = pl.program_id(0)

    @pl.when(i < nu_ref[0])
    def _():
        x = xs_ref[...].astype(BF16)
        a = _dot(x, wg_ref[0])
        h = a * _sigmoid(a) * _dot(x, wu_ref[0])
        ys_ref[...] = _dot(h.astype(BF16), wd_ref[0])

    @pl.when(i >= nu_ref[0])
    def _():
        ys_ref[...] = jnp.zeros(ys_ref.shape, F32)


def _experts(blk_e, nused, xs, wg, wu, wd):
    P, D = xs.shape
    F = wg.shape[2]
    nblk = P // MOE_ROWS
    return pl.pallas_call(
        _expert_kernel,
        grid_spec=pltpu.PrefetchScalarGridSpec(
            num_scalar_prefetch=2, grid=(nblk,),
            in_specs=[pl.BlockSpec((MOE_ROWS, D), lambda i, be, nu: (i, 0)),
                      pl.BlockSpec((1, D, F), lambda i, be, nu: (be[i], 0, 0)),
                      pl.BlockSpec((1, D, F), lambda i, be, nu: (be[i], 0, 0)),
                      pl.BlockSpec((1, F, D), lambda i, be, nu: (be[i], 0, 0))],
            out_specs=pl.BlockSpec((MOE_ROWS, D), lambda i, be, nu: (i, 0))),
        out_shape=jax.ShapeDtypeStruct((P, D), F32),
        compiler_params=_cparams("arbitrary"),
    )(blk_e, nused, xs, wg, wu, wd)


def _combine_kernel(dest_ref, ys_hbm, x_ref, rw_ref, g_ref, b_ref, xo_ref, xb_ref, buf, sem, *, tm):
    def issue(r, c):
        for k in range(2):
            _row_copy(ys_hbm, dest_ref[0, 0, 2 * r + k], buf.at[k], r, sem).start()
        return c

    lax.fori_loop(0, tm, issue, 0)

    def drain(r, c):
        _row_copy(ys_hbm, 0, buf.at[0], 0, sem).wait()
        return c

    lax.fori_loop(0, 2 * tm, drain, 0)
    w = rw_ref[...]
    h = DN_ALPHA * x_ref[...] + w[:, 0:1] * buf[0] + w[:, 1:2] * buf[1]
    y = _layer_norm(h, g_ref[...], b_ref[...])
    xo_ref[...] = y
    xb_ref[...] = y.astype(BF16)


def _combine(dest3, ys, x, rw, g, b, tm):
    T, D = x.shape
    return pl.pallas_call(
        functools.partial(_combine_kernel, tm=tm),
        grid=(T // tm,),
        in_specs=[pl.BlockSpec((1, 1, 2 * tm), lambda i: (i, 0, 0), memory_space=pltpu.SMEM),
                  pl.BlockSpec(memory_space=pl.ANY),
                  pl.BlockSpec((tm, D), lambda i: (i, 0)), pl.BlockSpec((tm, LANES), lambda i: (i, 0)),
                  pl.BlockSpec(g.shape, lambda i: (0, 0)), pl.BlockSpec(b.shape, lambda i: (0, 0))],
        out_specs=[pl.BlockSpec((tm, D), lambda i: (i, 0)), pl.BlockSpec((tm, D), lambda i: (i, 0))],
        out_shape=[jax.ShapeDtypeStruct((T, D), F32), jax.ShapeDtypeStruct((T, D), BF16)],
        scratch_shapes=[pltpu.VMEM((2, tm, D), F32), pltpu.SemaphoreType.DMA(())],
        compiler_params=_cparams("arbitrary"),
    )(dest3, ys, x, rw, g, b)


def _prep_in_proj(w_in):
    L, D, _ = w_in.shape
    nw = HEADS * HEAD_DIM
    kvw = 6 * KV_GROUPS * HEAD_DIM
    o = 0
    wq = w_in[..., o:o + nw]; o += nw
    wkv = w_in[..., o:o + kvw]; o += kvw
    wng = w_in[..., o:o + 3 * HEADS]; o += 3 * HEADS
    wpool = w_in[..., o:o + D // 2]; o += D // 2
    wssm = w_in[..., o:o + D // 2]; o += D // 2
    wbg = w_in[..., o:]
    wq = wq.reshape(L, D, KV_GROUPS, GQA, HEAD_DIM) * (HEAD_DIM ** -0.5)
    z = jnp.zeros_like(wq[:, :, 0])
    wq_wide = jnp.stack([jnp.concatenate([wq[:, :, 0], z], -1), jnp.concatenate([z, wq[:, :, 1]], -1)], 2)
    wq_wide = wq_wide.reshape(L, D, HEADS * LANES)
    wng = jnp.pad(wng, ((0, 0), (0, 0), (0, LANES - 3 * HEADS)))
    w_a = jnp.concatenate([wq_wide, wkv], -1).astype(BF16)
    w_b = jnp.concatenate([wng, wpool, wbg], -1).astype(BF16)
    return w_a, w_b, wssm.astype(BF16)


def _prep_compress(pe_k, pe_v, w1k, w2k, w1v, w2v):
    L = pe_k.shape[0]
    half = CMP_BLOCK // 2
    cw = 2 * KV_GROUPS * HEAD_DIM
    w1 = jnp.zeros((L, half, 2, KV_GROUPS, HEAD_DIM, 2, 2, KV_GROUPS, CMP_HIDDEN), F32)
    pe = jnp.zeros((L, SUBLANES, half, 2, KV_GROUPS, HEAD_DIM), F32)
    w2 = jnp.zeros((L, 2, KV_GROUPS, CMP_HIDDEN, 2, KV_GROUPS, HEAD_DIM), F32)
    for s, (w1s, w2s, pes) in enumerate(((w1k, w2k, pe_k), (w1v, w2v, pe_v))):
        w1r = w1s.reshape(L, 2, half, HEAD_DIM, CMP_HIDDEN)
        per = pes.reshape(L, 2, half, HEAD_DIM)
        for g in range(KV_GROUPS):
            for part in range(2):
                w1 = w1.at[:, :, s, g, :, part, s, g, :].set(w1r[:, part])
                pe = pe.at[:, part, :, s, g, :].set(per[:, part])
            w2 = w2.at[:, s, g, :, s, g, :].set(w2s)
    w1 = w1.reshape(L, half * cw, 2 * 2 * KV_GROUPS * CMP_HIDDEN)
    pe = pe.reshape(L, SUBLANES, half * cw)
    w2 = w2.reshape(L, 2 * KV_GROUPS * CMP_HIDDEN, 2 * KV_GROUPS * HEAD_DIM)
    return w1.astype(BF16), pe.astype(BF16), w2.astype(BF16)


def _overlap(nc):
    i0 = jnp.arange(nc)[:, None] * CMP_STRIDE
    j0 = jnp.arange(LANES)[None, :] * SEL_BLOCK
    return ((i0 < j0 + SEL_BLOCK) & (i0 + CMP_BLOCK > j0)).astype(BF16)


def _prep_ssm(lam_re, lam_im, log_dt, b_re, b_im, c_re, c_im):
    L = lam_re.shape[0]
    dt = jnp.exp(log_dt)[..., None]
    mag = jnp.exp(lam_re * dt)
    ar, ai = mag * jnp.cos(lam_im * dt), mag * jnp.sin(lam_im * dt)
    den = lam_re * lam_re + lam_im * lam_im
    cr = ((ar - 1.0) * lam_re + ai * lam_im) / den
    ci = (ai * lam_re - (ar - 1.0) * lam_im) / den
    bbr = cr[..., None] * b_re - ci[..., None] * b_im
    bbi = cr[..., None] * b_im + ci[..., None] * b_re
    eye = jnp.eye(SSM_GROUPS, dtype=F32)
    bt = jnp.concatenate([jnp.einsum('lgph,gk->lghkp', m, eye).reshape(L, SSM_GROUPS * SSM_GROUP, SSM_NS)
                          for m in (bbr, bbi)], -1)
    cm = jnp.concatenate([jnp.einsum('lghp,gk->lgpkh', m, eye).reshape(L, SSM_NS, SSM_GROUPS * SSM_GROUP)
                          for m in (c_re, -c_im)], 1)
    a = jnp.concatenate([ar.reshape(L, 1, SSM_NS), ai.reshape(L, 1, SSM_NS)], -1)
    a = jnp.broadcast_to(a, (L, SUBLANES, 2 * SSM_NS))
    return bt.astype(BF16), a, cm.astype(BF16)


def _prep_router(w_grp, b_grp, w_exp, b_exp):
    L, D, _ = w_grp.shape
    wr = jnp.zeros((L, D, LANES), F32).at[:, :, 0:MOE_GROUPS].set(w_grp)
    wr = wr.at[:, :, EXP_LANE0:EXP_LANE0 + MOE_EXPERTS].set(w_exp)
    br = jnp.zeros((L, 1, LANES), F32).at[:, 0, 0:MOE_GROUPS].set(b_grp)
    br = br.at[:, 0, EXP_LANE0:EXP_LANE0 + MOE_EXPERTS].set(b_exp)
    return wr, br


def _layer(B, S, consts, carry, w):
    x, xb = carry
    T, D = x.shape
    nc = S // CMP_STRIDE
    qw, kv = _proj(xb, w["w_a"], ((HEADS * LANES, BF16), (6 * LANES, BF16)), 512)
    ng, u_pool, bg = _proj(xb, w["w_b"], ((LANES, F32), (D // 2, F32), (3 * D, F32)), 256)
    u_ssm = _proj_time_major(xb, w["w_c"], B, S, 512).reshape(S * B, D // 2)
    kr = kv[:, 0:2 * LANES].reshape(T // CMP_STRIDE, CMP_STRIDE * 2 * LANES)
    kvc = _compress(kr, w["cw1"], w["cpe"], w["cw2"], B, nc)
    qaux, kaux, caux, ex = consts
    acc, sel, un = _nsa1(qw, kvc, ng, w["ov"], qaux, caux, B, S)
    lst, cnt = _chunk_lists(un, B, S)
    o_nsa = _nsa2(lst, cnt, qw, kv, sel, ng, acc, qaux, kaux, ex, B, S)
    o_pool = _pool(u_pool, w["pool_w"], w["pool_scale"], B, S, 512)
    o_ssm = _ssm(u_ssm, w["bt"], w["a"], w["cm"], w["d"], w["w_glu"], w["b_glu"], 128 * B)
    o_ssm = o_ssm.reshape(S, B * (D // 2))
    x, xb = _merge(o_nsa, o_pool, o_ssm, bg, x, w["wn"], w["wp"], w["ws"], w["wo"], w["ln1_g"], w["ln1_b"], B, S, 256)
    tm = 256
    ri, rw, cnt = _router(x, w["wr"], w["br"], 512)
    counts = cnt[0, 0:MOE_EXPERTS].astype(jnp.int32)
    padded = (counts + MOE_ROWS - 1) // MOE_ROWS * MOE_ROWS
    pend = jnp.cumsum(padded)
    pstart = pend - padded
    nblk = (2 * T) // MOE_ROWS + MOE_EXPERTS
    dest = pstart[ri[:, 0:2]] + ri[:, 2:4]
    dest3 = dest.reshape(T // tm, 1, 2 * tm)
    blk_e = jnp.minimum(jnp.searchsorted(pend, jnp.arange(nblk, dtype=jnp.int32) * MOE_ROWS, side='right'),
                        MOE_EXPERTS - 1).astype(jnp.int32)
    nused = (pend[-1:] // MOE_ROWS).astype(jnp.int32)
    xs = _dispatch(dest3, x, jnp.zeros((nblk * MOE_ROWS, D), F32), tm)
    ys = _experts(blk_e, nused, xs, w["wg"], w["wu"], w["wd"])
    x, xb = _combine(dest3, ys, x, rw, w["ln2_g"], w["ln2_b"], tm)
    return (x, xb), None


def kernel(x, w_in, cmp_pe_k, cmp_pe_v, cmp_w1_k, cmp_w2_k, cmp_w1_v, cmp_w2_v, pool_w, pool_scale, ssm_lam_re, ssm_lam_im, ssm_log_dt, ssm_b_re, ssm_b_im, ssm_c_re, ssm_c_im, ssm_d, ssm_w_glu, ssm_b_glu, w_up_nsa, w_up_pool, w_up_ssm, w_out, ln1_g, ln1_b, router_w_grp, router_b_grp, router_w_exp, router_b_exp, moe_w_gate, moe_w_up, moe_w_down, ln2_g, ln2_b):
    B, S, D = x.shape
    L = w_in.shape[0]
    assert B == 4 and D == D_MODEL and S % SEL_CHUNK == 0 and S // SEL_BLOCK <= LANES
    assert Q_TILE % (2 * SEL_BLOCK) == 0 and SEL_CHUNK % Q_TILE == 0
    w_a, w_b, w_c = _prep_in_proj(w_in)
    cw1, cpe, cw2 = _prep_compress(cmp_pe_k, cmp_pe_v, cmp_w1_k, cmp_w2_k, cmp_w1_v, cmp_w2_v)
    bt, a, cm = _prep_ssm(ssm_lam_re, ssm_lam_im, ssm_log_dt, ssm_b_re, ssm_b_im, ssm_c_re, ssm_c_im)
    wr, br = _prep_router(router_w_grp, router_b_grp, router_w_exp, router_b_exp)
    wn = w_up_nsa.reshape(L, KV_GROUPS, GQA, HEAD_DIM, D).transpose(0, 2, 1, 3, 4).reshape(L, HEADS * HEAD_DIM, D)
    row = lambda v: v.reshape(L, 1, -1)
    ws = dict(
        w_a=w_a, w_b=w_b, w_c=w_c, cw1=cw1, cpe=cpe, cw2=cw2,
        ov=jnp.broadcast_to(_overlap(S // CMP_STRIDE), (L, S // CMP_STRIDE, LANES)),
        pool_w=pool_w.astype(BF16), pool_scale=row(pool_scale),
        bt=bt, a=a, cm=cm, d=row(ssm_d), w_glu=ssm_w_glu.astype(BF16), b_glu=row(ssm_b_glu),
        wn=wn.astype(BF16), wp=w_up_pool.astype(BF16), ws=w_up_ssm.astype(BF16), wo=w_out.astype(BF16),
        ln1_g=row(ln1_g), ln1_b=row(ln1_b), wr=wr, br=br,
        wg=moe_w_gate.astype(BF16), wu=moe_w_up.astype(BF16), wd=moe_w_down.astype(BF16),
        ln2_g=row(ln2_g), ln2_b=row(ln2_b))
    xf = x.reshape(B * S, D)
    (xf, _), _ = lax.scan(functools.partial(_layer, B, S, _nsa_consts(S)), (xf, xf.astype(BF16)), ws)
    return xf.reshape(B, S, D)
```

```python
import functools
import math

import jax
import jax.numpy as jnp
from jax import lax
from jax.experimental import pallas as pl
from jax.experimental.pallas import tpu as pltpu

F32 = jnp.float32
BF16 = jnp.bfloat16

D_MODEL = 1024
DEPTH = 4
HEAD_DIM = 64
HEADS = 8
KV_GROUPS = 2
GQA = HEADS // KV_GROUPS
CMP_BLOCK = 32
CMP_STRIDE = 16
CMP_HIDDEN = 128
SEL_BLOCK = 64
SEL_TOPN = 16
WINDOW = 512
FORCE_SCORE = 1.0e4
NEG_INF = -1.0e30
POOL_WINDOWS = (2, 4, 8, 16)
POOL_HALO = 16
SSM_GROUP = 16
SSM_GROUPS = 32
SSM_STATE = 64
SSM_NS = SSM_GROUPS * SSM_STATE
MOE_GROUPS = 4
MOE_EPG = 8
MOE_EXPERTS = 32
MOE_FF = 512
DN_ALPHA = (2 * DEPTH) ** 0.25
LN_EPS = 1e-5

LANES = 128
SUBLANES = 8
VMEM_LIMIT = 48 * 1024 * 1024
Q_TILE = 128
SEL_CHUNK = 512
MOE_ROWS = 256
EXP_LANE0 = 32
DMA_UNROLL = 4
SEL_SHIFT = SEL_BLOCK.bit_length() - 1
EPG_SHIFT = MOE_EPG.bit_length() - 1


def _cparams(*sem):
    return pltpu.CompilerParams(dimension_semantics=sem, vmem_limit_bytes=VMEM_LIMIT)


def _gelu(x):
    return 0.5 * x * (1.0 + jnp.tanh(math.sqrt(2.0 / math.pi) * (x + 0.044715 * x * x * x)))


def _sigmoid(x):
    return 1.0 / (1.0 + jnp.exp(-x))


def _dot(a, b):
    return jnp.dot(a, b, preferred_element_type=F32)


def _dot_nt(a, b):
    return lax.dot_general(a, b, (((1,), (1,)), ((), ())), preferred_element_type=F32)


def _layer_norm(h, g, b):
    mu = jnp.mean(h, -1, keepdims=True)
    c = h - mu
    var = jnp.mean(c * c, -1, keepdims=True)
    return c * lax.rsqrt(var + LN_EPS) * g + b


def _proj_kernel(x_ref, w_ref, *out_refs, offs):
    x = x_ref[...]
    for o_ref, (a, n) in zip(out_refs, offs):
        o_ref[...] = _dot(x, w_ref[:, a:a + n]).astype(o_ref.dtype)


def _proj(xb, w, outs, tm):
    T, K = xb.shape
    N = w.shape[1]
    offs, a = [], 0
    for n, _ in outs:
        offs.append((a, n))
        a += n
    assert a == N and T % tm == 0
    return pl.pallas_call(
        functools.partial(_proj_kernel, offs=tuple(offs)),
        grid=(T // tm,),
        in_specs=[pl.BlockSpec((tm, K), lambda i: (i, 0)), pl.BlockSpec((K, N), lambda i: (0, 0))],
        out_specs=[pl.BlockSpec((tm, n), lambda i: (i, 0)) for n, _ in outs],
        out_shape=[jax.ShapeDtypeStruct((T, n), dt) for n, dt in outs],
        compiler_params=_cparams("parallel"),
    )(xb, w)


def _proj_tm_kernel(x_ref, w_ref, o_ref):
    o_ref[...] = _dot(x_ref[...], w_ref[...])


def _proj_time_major(xb, w, B, S, tm):
    K = xb.shape[1]
    N = w.shape[1]
    nt = S // tm
    return pl.pallas_call(
        _proj_tm_kernel,
        grid=(B, nt),
        in_specs=[pl.BlockSpec((tm, K), lambda b, i: (b * nt + i, 0)), pl.BlockSpec((K, N), lambda b, i: (0, 0))],
        out_specs=pl.BlockSpec((tm, N), lambda b, i: (i, b)),
        out_shape=jax.ShapeDtypeStruct((S, B * N), F32),
        compiler_params=_cparams("parallel", "parallel"),
    )(xb, w)


def _compress_kernel(kr_ref, w1_ref, pe_ref, w2_ref, out_ref):
    nc = kr_ref.shape[0]
    hw = 4 * CMP_HIDDEN
    a = _dot(kr_ref[...], w1_ref[...])
    pb = _dot(pe_ref[...], w1_ref[...])
    second = pltpu.roll(a[:, hw:], nc - 1, 0)
    row = lax.broadcasted_iota(jnp.int32, (nc, hw), 0)
    second = jnp.where(row < nc - 1, second, 0.0)
    hid = _gelu(a[:, :hw] + second + pb[0:1, :hw] + pb[1:2, hw:])
    out_ref[0] = _dot(hid.astype(BF16), w2_ref[...]).astype(out_ref.dtype)


def _compress(kr, w1, pe_in, w2, B, nc):
    return pl.pallas_call(
        _compress_kernel,
        grid=(B,),
        in_specs=[pl.BlockSpec((nc, kr.shape[1]), lambda b: (b, 0)),
                  pl.BlockSpec(w1.shape, lambda b: (0, 0)),
                  pl.BlockSpec(pe_in.shape, lambda b: (0, 0)),
                  pl.BlockSpec(w2.shape, lambda b: (0, 0))],
        out_specs=pl.BlockSpec((1, nc, 2 * LANES), lambda b: (b, 0, 0)),
        out_shape=jax.ShapeDtypeStruct((B, nc, 2 * LANES), BF16),
        compiler_params=_cparams("parallel"),
    )(kr, w1, pe_in, w2)


def _slope(h):
    return 2.0 ** (-8.0 * (h + 1) / HEADS)


def _stack_heads(q, g):
    return jnp.concatenate([q[:, (g * GQA + r) * LANES:(g * GQA + r + 1) * LANES] for r in range(GQA)], axis=0)


def _split3_dot(a, b):
    hi = a.astype(BF16)
    r1 = a - hi.astype(F32)
    mid = r1.astype(BF16)
    lo = (r1 - mid.astype(F32)).astype(BF16)
    return _dot(hi, b) + _dot(mid, b) + _dot(lo, b)


def _nsa1_kernel(q_ref, kvc_ref, ng_ref, ov_ref, qaux_ref, caux_ref, acc_ref, sel_ref, un_ref, sc_ref):
    Q = q_ref.shape[0]
    nc = kvc_ref.shape[1]
    t0 = pl.program_id(1) * Q
    q = q_ref[...]
    kc = jnp.concatenate([kvc_ref[0, :, 0:LANES], caux_ref[...]], axis=1)
    vc = kvc_ref[0, :, LANES:2 * LANES]
    gate = _sigmoid(ng_ref[...])
    qrow = lax.broadcasted_iota(jnp.int32, (Q, nc), 0)
    ccol = lax.broadcasted_iota(jnp.int32, (Q, nc), 1)
    bias = jnp.where((t0 + qrow) >= (ccol * CMP_STRIDE + (CMP_BLOCK - 1)), 0.0, NEG_INF)
    bias4 = jnp.concatenate([bias] * GQA, axis=0)
    lane = lax.broadcasted_iota(jnp.int32, (Q, LANES), 1)
    tb = jnp.right_shift(t0 + lax.broadcasted_iota(jnp.int32, (Q, LANES), 0), SEL_SHIFT)
    jrow = lax.broadcasted_iota(jnp.int32, (LANES, KV_GROUPS * Q), 0).astype(F32)
    outs = []
    for g in range(KV_GROUPS):
        qs = jnp.concatenate([_stack_heads(q, g), qaux_ref[g]], axis=1)
        s = _dot_nt(qs, kc) + bias4
        m = jnp.max(s, -1, keepdims=True)
        e = jnp.exp(s - m)
        inv = jnp.where(m > 0.5 * NEG_INF, 1.0 / jnp.maximum(jnp.sum(e, -1, keepdims=True), 1e-30), 0.0)
        p = e * inv
        oc = _dot(p.astype(BF16), vc)
        psum = p[0:Q]
        for r in range(1, GQA):
            psum = psum + p[r * Q:(r + 1) * Q]
        outs.append([gate[:, 3 * (g * GQA + r):3 * (g * GQA + r) + 1] * oc[r * Q:(r + 1) * Q] for r in range(GQA)])
        imp = _split3_dot(psum, ov_ref[...])
        forced = (lane == 0) | (lane == tb) | (lane == tb - 1)
        score = jnp.where(forced, FORCE_SCORE, jnp.where(lane <= tb, imp, -1.0))
        sc_ref[:, g * Q:(g + 1) * Q] = score.T
    work = sc_ref[...]
    taken = jnp.zeros(work.shape, F32)
    for _ in range(SEL_TOPN):
        best = jnp.max(work, 0, keepdims=True)
        first = jnp.min(jnp.where(work == best, jrow, float(LANES)), 0, keepdims=True)
        hit = jrow == first
        taken = jnp.where(hit, 1.0, taken)
        work = jnp.where(hit, -2.0, work)
    for g in range(KV_GROUPS):
        sel = taken[:, g * Q:(g + 1) * Q].T
        sel = jnp.where(lane <= tb, sel, 0.0)
        sel_ref[:, g * LANES:(g + 1) * LANES] = sel.astype(sel_ref.dtype)
        un_ref[:, g * LANES:(g + 1) * LANES] = jnp.broadcast_to(jnp.max(sel, 0, keepdims=True), (SUBLANES, LANES))
    for r in range(GQA):
        acc_ref[:, r * LANES:(r + 1) * LANES] = jnp.where(lane < HEAD_DIM, outs[0][r], outs[1][r])


def _nsa1(qw, kvc, ng, ov, qaux, caux, B, S):
    nq = S // Q_TILE
    nc = kvc.shape[1]
    T = B * S
    rows = lambda b, i: (b * nq + i, 0)
    return pl.pallas_call(
        _nsa1_kernel,
        grid=(B, nq),
        in_specs=[pl.BlockSpec((Q_TILE, HEADS * LANES), rows),
                  pl.BlockSpec((1, nc, 2 * LANES), lambda b, i: (b, 0, 0)),
                  pl.BlockSpec((Q_TILE, LANES), rows),
                  pl.BlockSpec((nc, LANES), lambda b, i: (0, 0)),
                  pl.BlockSpec(qaux.shape, lambda b, i: (0, 0, 0)),
                  pl.BlockSpec((nc, LANES), lambda b, i: (0, 0))],
        out_specs=[pl.BlockSpec((Q_TILE, GQA * LANES), rows),
                   pl.BlockSpec((Q_TILE, KV_GROUPS * LANES), rows),
                   pl.BlockSpec((SUBLANES, KV_GROUPS * LANES), rows)],
        out_shape=[jax.ShapeDtypeStruct((T, GQA * LANES), F32),
                   jax.ShapeDtypeStruct((T, KV_GROUPS * LANES), BF16),
                   jax.ShapeDtypeStruct((B * nq * SUBLANES, KV_GROUPS * LANES), F32)],
        scratch_shapes=[pltpu.VMEM((LANES, KV_GROUPS * Q_TILE), F32)],
        compiler_params=_cparams("parallel", "parallel"),
    )(qw, kvc, ng, ov, qaux, caux)


def _nsa2_kernel(lst_ref, cnt_ref, q_ref, ks_ref, vs_ref, kw_ref, vw_ref, sel_ref, ng_ref, acc_ref,
                 qaux_ref, kaux_ref, ex_ref, o_ref, m_sc, l_sc, a_sc, ow_sc, s0_sc, s1_sc, sd_sc):
    Q = q_ref.shape[0]
    C = SEL_CHUNK
    WK = WINDOW + Q
    maxch = ex_ref.shape[1] // C
    tile = pl.program_id(0) * pl.num_programs(1) + pl.program_id(1)
    t0 = pl.program_id(1) * Q
    q = q_ref[...]
    gate = _sigmoid(ng_ref[...])
    lane = lax.broadcasted_iota(jnp.int32, (Q, LANES), 1)
    cdiag = t0 // C
    start = pl.multiple_of(jnp.maximum(t0 - WINDOW, 0), Q)
    wrow = lax.broadcasted_iota(jnp.int32, (Q, WK), 0)
    wcol = lax.broadcasted_iota(jnp.int32, (Q, WK), 1)
    wdist = (t0 + wrow) - (start + wcol)
    wbias = jnp.where((wdist >= 0) & (wdist < WINDOW), 0.0, NEG_INF)
    wbias4 = jnp.concatenate([wbias] * GQA, axis=0)
    qrow = lax.broadcasted_iota(jnp.int32, (Q, C), 0)
    kcol = lax.broadcasted_iota(jnp.int32, (Q, C), 1)
    outs = []
    for g in range(KV_GROUPS):
        qs = jnp.concatenate([_stack_heads(q, g), qaux_ref[g]], axis=1)
        selg = sel_ref[:, g * LANES:(g + 1) * LANES]
        m_sc[...] = jnp.full(m_sc.shape, NEG_INF, F32)
        l_sc[...] = jnp.zeros(l_sc.shape, F32)
        a_sc[...] = jnp.zeros(a_sc.shape, F32)

        def scores(c, s_out, qs=qs, selg=selg):
            k0 = pl.multiple_of(c * C, C)
            kk = jnp.concatenate([ks_ref[pl.ds(k0, C), :], kaux_ref[pl.ds(k0, C), :]], axis=1)
            valid = (_dot(selg, ex_ref[:, pl.ds(k0, C)]) > 0.5) & ((t0 + qrow) >= (k0 + kcol))
            bias = jnp.where(valid, 0.0, NEG_INF)
            s_out[...] = _dot_nt(qs, kk) + jnp.concatenate([bias] * GQA, axis=0)

        def accumulate(c, s_in):
            k0 = pl.multiple_of(c * C, C)
            s = s_in[...]
            m_old = m_sc[...]
            m_new = jnp.maximum(m_old, jnp.max(s, -1, keepdims=True))
            alpha = jnp.exp(m_old - m_new)
            e = jnp.exp(s - m_new[:, 0:1])
            l_sc[...] = alpha * l_sc[...] + jnp.sum(e, -1, keepdims=True)
            a_sc[...] = alpha * a_sc[...] + _dot(e.astype(BF16), vs_ref[pl.ds(k0, C), :])
            m_sc[...] = m_new

        base = (tile * KV_GROUPS + g) * maxch
        scores(cdiag, sd_sc)
        scores(lst_ref[base], s0_sc)
        kw = jnp.concatenate([kw_ref[pl.ds(start, WK), :], kaux_ref[pl.ds(start, WK), :]], axis=1)
        sw = _dot_nt(qs, kw) + wbias4
        e = jnp.exp(sw - jnp.max(sw, -1, keepdims=True))
        ow_sc[...] = _dot(e.astype(BF16), vw_ref[pl.ds(start, WK), :]) / jnp.maximum(jnp.sum(e, -1, keepdims=True), 1e-30)

        def step(j, carry, scores=scores, accumulate=accumulate, base=base):
            cur = lst_ref[base + j]
            nxt = lst_ref[base + j + 1]

            @pl.when(j % 2 == 0)
            def _():
                scores(nxt, s1_sc)
                accumulate(cur, s0_sc)

            @pl.when(j % 2 == 1)
            def _():
                scores(nxt, s0_sc)
                accumulate(cur, s1_sc)

            return carry

        lax.fori_loop(0, cnt_ref[tile * KV_GROUPS + g], step, 0)
        accumulate(cdiag, sd_sc)
        o_w = ow_sc[...]
        o_s = a_sc[...] / jnp.maximum(l_sc[...], 1e-30)
        og = []
        for r in range(GQA):
            h = g * GQA + r
            rows = slice(r * Q, (r + 1) * Q)
            og.append(gate[:, 3 * h + 1:3 * h + 2] * o_s[rows] + gate[:, 3 * h + 2:3 * h + 3] * o_w[rows])
        outs.append(og)
    for r in range(GQA):
        cols = slice(r * LANES, (r + 1) * LANES)
        o_ref[:, cols] = (acc_ref[:, cols] + jnp.where(lane < HEAD_DIM, outs[0][r], outs[1][r])).astype(o_ref.dtype)


def _chunk_lists(un, B, S):
    nq = S // Q_TILE
    maxch = S // SEL_CHUNK
    per = SEL_CHUNK // SEL_BLOCK
    un = un.reshape(B * nq, SUBLANES, KV_GROUPS, LANES)[:, 0, :, :maxch * per]
    flags = un.reshape(B * nq, KV_GROUPS, maxch, per).max(-1) > 0.5
    cdiag = (jnp.arange(B * nq, dtype=jnp.int32) % nq) * Q_TILE // SEL_CHUNK
    flags = flags & (jnp.arange(maxch, dtype=jnp.int32)[None, None, :] < cdiag[:, None, None])
    pos = jnp.cumsum(flags.astype(jnp.int32), -1)
    hit = flags[..., None, :] & (pos[..., None, :] == jnp.arange(1, maxch + 1, dtype=jnp.int32)[:, None])
    lst = jnp.sum(jnp.where(hit, jnp.arange(maxch, dtype=jnp.int32), 0), -1)
    return lst.astype(jnp.int32).reshape(-1), pos[..., -1].reshape(-1)


def _nsa2(lst, cnt, qw, kv, sel, ng, acc, qaux, kaux, ex, B, S):
    nq = S // Q_TILE
    T = B * S
    rows = lambda b, i, *_: (b * nq + i, 0)
    kvspec = lambda col: pl.BlockSpec((S, LANES), lambda b, i, *_, col=col: (b, col))
    const = lambda shp: pl.BlockSpec(shp, lambda b, i, *_: tuple(0 for _ in shp))
    return pl.pallas_call(
        _nsa2_kernel,
        grid_spec=pltpu.PrefetchScalarGridSpec(
            num_scalar_prefetch=2, grid=(B, nq),
            in_specs=[pl.BlockSpec((Q_TILE, HEADS * LANES), rows),
                      kvspec(2), kvspec(3), kvspec(4), kvspec(5),
                      pl.BlockSpec((Q_TILE, KV_GROUPS * LANES), rows),
                      pl.BlockSpec((Q_TILE, LANES), rows),
                      pl.BlockSpec((Q_TILE, GQA * LANES), rows),
                      const(qaux.shape), const(kaux.shape), const(ex.shape)],
            out_specs=pl.BlockSpec((Q_TILE, GQA * LANES), rows),
            scratch_shapes=[pltpu.VMEM((GQA * Q_TILE, LANES), F32)] * 4
            + [pltpu.VMEM((GQA * Q_TILE, SEL_CHUNK), F32)] * 3),
        out_shape=jax.ShapeDtypeStruct((T, GQA * LANES), BF16),
        compiler_params=_cparams("parallel", "parallel"),
    )(lst, cnt, qw, kv, kv, kv, kv, sel, ng, acc, qaux, kaux, ex)


def _nsa_consts(S):
    h = jnp.arange(HEADS, dtype=F32).reshape(KV_GROUPS, GQA, 1, 1)
    slope = jnp.exp2(-8.0 * (h + 1.0) / HEADS)
    lane = jnp.arange(LANES)[None, None, None, :]
    qaux = jnp.where(lane == 0, slope * SEL_BLOCK, jnp.where(lane == 1, slope, 0.0))
    qaux = jnp.broadcast_to(qaux, (KV_GROUPS, GQA, Q_TILE, LANES)).reshape(KV_GROUPS, GQA * Q_TILE, LANES)

    def pos_cols(pos):
        l2 = jnp.arange(LANES)[None, :]
        return jnp.where(l2 == 0, pos[:, None] // SEL_BLOCK, jnp.where(l2 == 1, pos[:, None] % SEL_BLOCK, 0))

    kaux = pos_cols(jnp.arange(S))
    caux = pos_cols(jnp.arange(S // CMP_STRIDE) * CMP_STRIDE + (CMP_BLOCK - 1))
    ex = (jnp.arange(LANES)[:, None] == (jnp.arange(S)[None, :] // SEL_BLOCK))
    return qaux.astype(BF16), kaux.astype(BF16), caux.astype(BF16), ex.astype(BF16)


def _pool_kernel(u_ref, halo_ref, w_ref, sc_ref, o_ref, x_sc):
    ts = u_ref.shape[0]
    i = pl.program_id(1)
    x_sc[0:POOL_HALO] = jnp.where(i > 0, halo_ref[...], 0.0)
    x_sc[POOL_HALO:POOL_HALO + ts] = u_ref[...]
    t = i * ts + lax.broadcasted_iota(jnp.int32, (ts, LANES), 0)
    for g, w in enumerate(POOL_WINDOWS):
        cols = slice(g * LANES, (g + 1) * LANES)
        cur = x_sc[POOL_HALO:POOL_HALO + ts, cols]
        acc = cur
        for k in range(1, w):
            acc = acc + x_sc[POOL_HALO - k:POOL_HALO - k + ts, cols]
        cnt = jnp.minimum(t + 1, w).astype(F32)
        pooled = acc / cnt - cur
        o_ref[:, cols] = (_dot(pooled.astype(BF16), w_ref[g]) * sc_ref[:, cols]).astype(o_ref.dtype)


def _pool(u, pool_w, pool_scale, B, S, ts):
    nt = S // ts
    hb = ts // POOL_HALO
    C = u.shape[1]
    return pl.pallas_call(
        _pool_kernel,
        grid=(B, nt),
        in_specs=[pl.BlockSpec((ts, C), lambda b, i: (b * nt + i, 0)),
                  pl.BlockSpec((POOL_HALO, C), lambda b, i: (jnp.maximum((b * nt + i) * hb - 1, 0), 0)),
                  pl.BlockSpec(pool_w.shape, lambda b, i: (0, 0, 0)),
                  pl.BlockSpec((1, C), lambda b, i: (0, 0))],
        out_specs=pl.BlockSpec((ts, C), lambda b, i: (b * nt + i, 0)),
        out_shape=jax.ShapeDtypeStruct((B * S, C), BF16),
        scratch_shapes=[pltpu.VMEM((POOL_HALO + ts, C), F32)],
        compiler_params=_cparams("parallel", "parallel"),
    )(u, u, pool_w, pool_scale)


SSM_PASS = 8
SSM_BLOCKS = SSM_GROUPS * SSM_GROUP // LANES
SSM_BLK_W = 2 * SSM_NS // SSM_BLOCKS


def _ssm_kernel(u_ref, bt_ref, a_ref, c_ref, d_ref, wg_ref, bg_ref, o_ref, xh, hst):
    rows = u_ref.shape[0]

    @pl.when(pl.program_id(0) == 0)
    def _():
        hst[...] = jnp.zeros(hst.shape, F32)

    u = u_ref[...]
    ub = u.astype(BF16)
    for k in range(SSM_BLOCKS):
        xh[:, k * SSM_BLK_W:(k + 1) * SSM_BLK_W] = _dot(ub[:, k * LANES:(k + 1) * LANES], bt_ref[k])
    low = lax.broadcasted_iota(jnp.int32, (SUBLANES, LANES), 0) < 4
    nchunk = SSM_NS // LANES
    per_blk = nchunk // SSM_BLOCKS
    for p in range(nchunk // SSM_PASS):
        cols = [p * SSM_PASS + j for j in range(SSM_PASS)]
        start = [(c // per_blk) * SSM_BLK_W + (c % per_blk) * LANES for c in cols]
        re = [slice(s, s + LANES) for s in start]
        im = [slice(s + SSM_BLK_W // 2, s + SSM_BLK_W // 2 + LANES) for s in start]
        ar = [a_ref[:, s] for s in re]
        ai = [a_ref[:, s] for s in im]

        def tile(k, carry, re=re, im=im, ar=ar, ai=ai):
            cr, ci = carry
            r0 = pl.multiple_of(k * SUBLANES, SUBLANES)
            ncr, nci = [], []
            for j in range(SSM_PASS):
                vr = xh[pl.ds(r0, SUBLANES), re[j]]
                vi = xh[pl.ds(r0, SUBLANES), im[j]]
                t1r = ar[j] * cr[j] - ai[j] * ci[j] + vr
                t1i = ar[j] * ci[j] + ai[j] * cr[j] + vi
                pr = pltpu.roll(t1r, 4, 0)
                pi = pltpu.roll(t1i, 4, 0)
                t2r = ar[j] * pr - ai[j] * pi + vr
                t2i = ar[j] * pi + ai[j] * pr + vi
                xh[pl.ds(r0, SUBLANES), re[j]] = jnp.where(low, t1r, t2r)
                xh[pl.ds(r0, SUBLANES), im[j]] = jnp.where(low, t1i, t2i)
                ncr.append(pltpu.roll(t2r, 4, 0))
                nci.append(pltpu.roll(t2i, 4, 0))
            return tuple(ncr), tuple(nci)

        cr0 = tuple(hst[:, s] for s in re)
        ci0 = tuple(hst[:, s] for s in im)
        cr, ci = lax.fori_loop(0, rows // SUBLANES, tile, (cr0, ci0))
        for j in range(SSM_PASS):
            hst[:, re[j]] = cr[j]
            hst[:, im[j]] = ci[j]
    y = jnp.concatenate([_dot(xh[:, k * SSM_BLK_W:(k + 1) * SSM_BLK_W].astype(BF16), c_ref[k])
                         for k in range(SSM_BLOCKS)], axis=1) + d_ref[...] * u
    z = _gelu(y)
    o_ref[...] = (z * _sigmoid(_dot(z.astype(BF16), wg_ref[...]) + bg_ref[...])).astype(o_ref.dtype)


def _ssm(u_tm, bt, a_b, cmat, d_skip, w_glu, b_glu, rows):
    n, C = u_tm.shape
    const = lambda shp: pl.BlockSpec(shp, lambda i: tuple(0 for _ in shp))
    return pl.pallas_call(
        _ssm_kernel,
        grid=(n // rows,),
        in_specs=[pl.BlockSpec((rows, C), lambda i: (i, 0)),
                  const(bt.shape), const(a_b.shape), const(cmat.shape), const(d_skip.shape),
                  const(w_glu.shape), const(b_glu.shape)],
        out_specs=pl.BlockSpec((rows, C), lambda i: (i, 0)),
        out_shape=jax.ShapeDtypeStruct((n, C), BF16),
        scratch_shapes=[pltpu.VMEM((rows, 2 * SSM_NS), F32), pltpu.VMEM((SUBLANES, 2 * SSM_NS), F32)],
        compiler_params=_cparams("arbitrary"),
    )(u_tm, bt, a_b, cmat, d_skip, w_glu, b_glu)


def _merge_kernel(on_ref, op_ref, os_ref, bg_ref, x_ref, wn_ref, wp_ref, ws_ref, wo_ref, g_ref, b_ref, xo_ref, xb_ref):
    D = x_ref.shape[1]
    gates = _sigmoid(bg_ref[...])
    m = (gates[:, 0:D] * _dot(on_ref[...], wn_ref[...])
         + gates[:, D:2 * D] * _dot(op_ref[...], wp_ref[...])
         + gates[:, 2 * D:3 * D] * _dot(os_ref[...], ws_ref[...]))
    h = DN_ALPHA * x_ref[...] + _dot(m.astype(BF16), wo_ref[...])
    y = _layer_norm(h, g_ref[...], b_ref[...])
    xo_ref[...] = y
    xb_ref[...] = y.astype(BF16)


def _merge(o_nsa, o_pool, o_ssm_tm, bg, x, wn, wp, ws, wo, g, b, B, S, tm):
    nt = S // tm
    T, D = x.shape
    W = o_nsa.shape[1]
    rows = lambda bb, i: (bb * nt + i, 0)
    const = lambda shp: pl.BlockSpec(shp, lambda bb, i: tuple(0 for _ in shp))
    return pl.pallas_call(
        _merge_kernel,
        grid=(B, nt),
        in_specs=[pl.BlockSpec((tm, W), rows), pl.BlockSpec((tm, W), rows),
                  pl.BlockSpec((tm, W), lambda bb, i: (i, bb)),
                  pl.BlockSpec((tm, 3 * D), rows), pl.BlockSpec((tm, D), rows),
                  const(wn.shape), const(wp.shape), const(ws.shape), const(wo.shape),
                  const(g.shape), const(b.shape)],
        out_specs=[pl.BlockSpec((tm, D), rows), pl.BlockSpec((tm, D), rows)],
        out_shape=[jax.ShapeDtypeStruct((T, D), F32), jax.ShapeDtypeStruct((T, D), BF16)],
        compiler_params=_cparams("parallel", "parallel"),
    )(o_nsa, o_pool, o_ssm_tm, bg, x, wn, wp, ws, wo, g, b)


def _router_kernel(x_ref, wr_ref, br_ref, ri_ref, rw_ref, cnt_ref, carry):
    tm = x_ref.shape[0]

    @pl.when(pl.program_id(0) == 0)
    def _():
        carry[...] = jnp.zeros(carry.shape, F32)

    logits = jnp.dot(x_ref[...], wr_ref[...], preferred_element_type=F32,
                     precision=lax.Precision.HIGHEST) + br_ref[...]
    lane = lax.broadcasted_iota(jnp.int32, (tm, LANES), 1)
    big = jnp.int32(LANES)
    isg = lane < MOE_GROUPS
    lg = jnp.where(isg, logits, NEG_INF)
    eg = jnp.where(isg, jnp.exp(lg - jnp.max(lg, -1, keepdims=True)), 0.0)
    pgv = jnp.where(isg, eg / jnp.sum(eg, -1, keepdims=True), -1.0)
    p_g = jnp.max(pgv, -1, keepdims=True)
    g_sel = jnp.min(jnp.where(pgv == p_g, lane, big), -1, keepdims=True)
    ine = ((lane >= EXP_LANE0) & (lane < EXP_LANE0 + MOE_EXPERTS)
           & (jnp.right_shift(lane - EXP_LANE0, EPG_SHIFT) == g_sel))
    le = jnp.where(ine, logits, NEG_INF)
    ee = jnp.where(ine, jnp.exp(le - jnp.max(le, -1, keepdims=True)), 0.0)
    p = jnp.where(ine, ee / jnp.sum(ee, -1, keepdims=True), -1.0)
    p1 = jnp.max(p, -1, keepdims=True)
    i1 = jnp.min(jnp.where(p == p1, lane, big), -1, keepdims=True)
    prest = jnp.where(lane == i1, -1.0, p)
    p2 = jnp.max(prest, -1, keepdims=True)
    i2 = jnp.min(jnp.where(prest == p2, lane, big), -1, keepdims=True)
    den = p1 + p2
    w0 = p1 / den * p_g
    w1 = p2 / den * p_g
    e0 = i1 - EXP_LANE0
    e1 = i2 - EXP_LANE0
    is0 = lane == e0
    is1 = lane == e1
    onehot = jnp.where(is0 | is1, 1.0, 0.0)
    r = lax.broadcasted_iota(jnp.int32, (tm, tm), 0)
    c = lax.broadcasted_iota(jnp.int32, (tm, tm), 1)
    tri = jnp.where(r > c, 1.0, 0.0).astype(BF16)
    cum = _dot(tri, onehot.astype(BF16)) + carry[0:1, :]
    rank0 = jnp.sum(jnp.where(is0, cum, 0.0), -1, keepdims=True).astype(jnp.int32)
    rank1 = jnp.sum(jnp.where(is1, cum, 0.0), -1, keepdims=True).astype(jnp.int32)
    carry[...] = carry[...] + jnp.sum(onehot, 0, keepdims=True)
    cnt_ref[...] = carry[...]
    zi = jnp.zeros((tm, LANES), jnp.int32)
    ri_ref[...] = jnp.where(lane == 0, e0, jnp.where(lane == 1, e1, jnp.where(lane == 2, rank0, jnp.where(lane == 3, rank1, zi))))
    rw_ref[...] = jnp.where(lane == 0, w0, jnp.where(lane == 1, w1, 0.0))


def _router(x, wr, br, tm):
    T, D = x.shape
    return pl.pallas_call(
        _router_kernel,
        grid=(T // tm,),
        in_specs=[pl.BlockSpec((tm, D), lambda i: (i, 0)),
                  pl.BlockSpec(wr.shape, lambda i: (0, 0)), pl.BlockSpec(br.shape, lambda i: (0, 0))],
        out_specs=[pl.BlockSpec((tm, LANES), lambda i: (i, 0)), pl.BlockSpec((tm, LANES), lambda i: (i, 0)),
                   pl.BlockSpec((SUBLANES, LANES), lambda i: (0, 0))],
        out_shape=[jax.ShapeDtypeStruct((T, LANES), jnp.int32), jax.ShapeDtypeStruct((T, LANES), F32),
                   jax.ShapeDtypeStruct((SUBLANES, LANES), F32)],
        scratch_shapes=[pltpu.VMEM((SUBLANES, LANES), F32)],
        compiler_params=_cparams("arbitrary"),
    )(x, wr, br)


def _row_copy(src, s, dst, d, sem):
    return pltpu.make_async_copy(src.at[pl.ds(s, 1)], dst.at[pl.ds(d, 1)], sem)


def _dispatch_kernel(dest_ref, x_ref, xs_in, xs_hbm, sem, *, tm):
    del xs_in

    def issue(r, c):
        for k in range(2):
            _row_copy(x_ref, r, xs_hbm, dest_ref[0, 0, 2 * r + k], sem).start()
        return c

    lax.fori_loop(0, tm, issue, 0, unroll=DMA_UNROLL)
    pltpu.make_async_copy(xs_hbm.at[pl.ds(0, 2 * tm)], xs_hbm.at[pl.ds(0, 2 * tm)], sem).wait()


def _dispatch(dest3, x, xs0, tm):
    T = x.shape[0]
    return pl.pallas_call(
        functools.partial(_dispatch_kernel, tm=tm),
        grid=(T // tm,),
        in_specs=[pl.BlockSpec((1, 1, 2 * tm), lambda i: (i, 0, 0), memory_space=pltpu.SMEM),
                  pl.BlockSpec((tm, x.shape[1]), lambda i: (i, 0)), pl.BlockSpec(memory_space=pl.ANY)],
        out_specs=pl.BlockSpec(memory_space=pl.ANY),
        out_shape=jax.ShapeDtypeStruct(xs0.shape, xs0.dtype),
        scratch_shapes=[pltpu.SemaphoreType.DMA(())],
        input_output_aliases={2: 0},
        compiler_params=_cparams("arbitrary"),
    )(dest3, x, xs0)


def _expert_kernel(be_ref, nu_ref, xs_ref, wg_ref, wu_ref, wd_ref, ys_ref):
    i = pl.program_id(0)

    @pl.when(i < nu_ref[0])
    def _():
        x = xs_ref[...].astype(BF16)
        a = _dot(x, wg_ref[0].astype(BF16))
        h = a * _sigmoid(a) * _dot(x, wu_ref[0].astype(BF16))
        ys_ref[...] = _dot(h.astype(BF16), wd_ref[0].astype(BF16))

    @pl.when(i >= nu_ref[0])
    def _():
        ys_ref[...] = jnp.zeros(ys_ref.shape, F32)


def _experts(blk_e, nused, xs, wg, wu, wd):
    P, D = xs.shape
    F = wg.shape[2]
    nblk = P // MOE_ROWS
    return pl.pallas_call(
        _expert_kernel,
        grid_spec=pltpu.PrefetchScalarGridSpec(
            num_scalar_prefetch=2, grid=(nblk,),
            in_specs=[pl.BlockSpec((MOE_ROWS, D), lambda i, be, nu: (i, 0)),
                      pl.BlockSpec((1, D, F), lambda i, be, nu: (be[i], 0, 0)),
                      pl.BlockSpec((1, D, F), lambda i, be, nu: (be[i], 0, 0)),
                      pl.BlockSpec((1, F, D), lambda i, be, nu: (be[i], 0, 0))],
            out_specs=pl.BlockSpec((MOE_ROWS, D), lambda i, be, nu: (i, 0))),
        out_shape=jax.ShapeDtypeStruct((P, D), F32),
        compiler_params=_cparams("arbitrary"),
    )(blk_e, nused, xs, wg, wu, wd)


def _combine_kernel(dest_ref, ys_hbm, x_ref, rw_ref, g_ref, b_ref, xo_ref, xb_ref, buf, sem, *, tm):
    def issue(r, c):
        for k in range(2):
            _row_copy(ys_hbm, dest_ref[0, 0, 2 * r + k], buf.at[k], r, sem).start()
        return c

    lax.fori_loop(0, tm, issue, 0, unroll=DMA_UNROLL)
    for k in range(2):
        pltpu.make_async_copy(ys_hbm.at[pl.ds(0, tm)], buf.at[k], sem).wait()
    w = rw_ref[...]
    h = DN_ALPHA * x_ref[...] + w[:, 0:1] * buf[0] + w[:, 1:2] * buf[1]
    y = _layer_norm(h, g_ref[...], b_ref[...])
    xo_ref[...] = y
    xb_ref[...] = y.astype(BF16)


def _combine(dest3, ys, x, rw, g, b, tm):
    T, D = x.shape
    return pl.pallas_call(
        functools.partial(_combine_kernel, tm=tm),
        grid=(T // tm,),
        in_specs=[pl.BlockSpec((1, 1, 2 * tm), lambda i: (i, 0, 0), memory_space=pltpu.SMEM),
                  pl.BlockSpec(memory_space=pl.ANY),
                  pl.BlockSpec((tm, D), lambda i: (i, 0)), pl.BlockSpec((tm, LANES), lambda i: (i, 0)),
                  pl.BlockSpec(g.shape, lambda i: (0, 0)), pl.BlockSpec(b.shape, lambda i: (0, 0))],
        out_specs=[pl.BlockSpec((tm, D), lambda i: (i, 0)), pl.BlockSpec((tm, D), lambda i: (i, 0))],
        out_shape=[jax.ShapeDtypeStruct((T, D), F32), jax.ShapeDtypeStruct((T, D), BF16)],
        scratch_shapes=[pltpu.VMEM((2, tm, D), F32), pltpu.SemaphoreType.DMA(())],
        compiler_params=_cparams("arbitrary"),
    )(dest3, ys, x, rw, g, b)


def _prep_in_proj(w_in):
    L, D, _ = w_in.shape
    nw = HEADS * HEAD_DIM
    kvw = 6 * KV_GROUPS * HEAD_DIM
    o = 0
    wq = w_in[..., o:o + nw]; o += nw
    wkv = w_in[..., o:o + kvw]; o += kvw
    wng = w_in[..., o:o + 3 * HEADS]; o += 3 * HEADS
    wpool = w_in[..., o:o + D // 2]; o += D // 2
    wssm = w_in[..., o:o + D // 2]; o += D // 2
    wbg = w_in[..., o:]
    wq = wq.reshape(L, D, KV_GROUPS, GQA, HEAD_DIM) * (HEAD_DIM ** -0.5)
    z = jnp.zeros_like(wq[:, :, 0])
    wq_wide = jnp.stack([jnp.concatenate([wq[:, :, 0], z], -1), jnp.concatenate([z, wq[:, :, 1]], -1)], 2)
    wq_wide = wq_wide.reshape(L, D, HEADS * LANES)
    wng = jnp.pad(wng, ((0, 0), (0, 0), (0, LANES - 3 * HEADS)))
    w_a = jnp.concatenate([wq_wide, wkv], -1).astype(BF16)
    w_b = jnp.concatenate([wng, wpool, wbg], -1).astype(BF16)
    return w_a, w_b, wssm.astype(BF16)


def _prep_compress(pe_k, pe_v, w1k, w2k, w1v, w2v):
    L = pe_k.shape[0]
    half = CMP_BLOCK // 2
    cw = 2 * KV_GROUPS * HEAD_DIM
    eye_s = jnp.eye(2, dtype=F32)
    eye_g = jnp.eye(KV_GROUPS, dtype=F32)
    w1r = jnp.stack([w1k, w1v], 1).reshape(L, 2, 2, half, HEAD_DIM, CMP_HIDDEN)
    w1 = jnp.einsum('aspldh,st,gj->alsgdptjh', w1r, eye_s, eye_g)
    per = jnp.stack([pe_k, pe_v], 1).reshape(L, 2, 2, half, HEAD_DIM)
    pe = jnp.broadcast_to(per.transpose(0, 2, 3, 1, 4)[:, :, :, :, None, :], (L, 2, half, 2, KV_GROUPS, HEAD_DIM))
    pe = jnp.pad(pe.reshape(L, 2, half * cw), ((0, 0), (0, SUBLANES - 2), (0, 0)))
    w2 = jnp.einsum('ashd,st,gj->asghtjd', jnp.stack([w2k, w2v], 1), eye_s, eye_g)
    w1 = w1.reshape(L, half * cw, 2 * 2 * KV_GROUPS * CMP_HIDDEN)
    w2 = w2.reshape(L, 2 * KV_GROUPS * CMP_HIDDEN, 2 * KV_GROUPS * HEAD_DIM)
    return w1.astype(BF16), pe.astype(BF16), w2.astype(BF16)


def _overlap(nc):
    i0 = jnp.arange(nc)[:, None] * CMP_STRIDE
    j0 = jnp.arange(LANES)[None, :] * SEL_BLOCK
    return ((i0 < j0 + SEL_BLOCK) & (i0 + CMP_BLOCK > j0)).astype(BF16)


def _prep_ssm(lam_re, lam_im, log_dt, b_re, b_im, c_re, c_im):
    L = lam_re.shape[0]
    dt = jnp.exp(log_dt)[..., None]
    mag = jnp.exp(lam_re * dt)
    ar, ai = mag * jnp.cos(lam_im * dt), mag * jnp.sin(lam_im * dt)
    den = lam_re * lam_re + lam_im * lam_im
    cr = ((ar - 1.0) * lam_re + ai * lam_im) / den
    ci = (ai * lam_re - (ar - 1.0) * lam_im) / den
    bbr = cr[..., None] * b_re - ci[..., None] * b_im
    bbi = cr[..., None] * b_im + ci[..., None] * b_re
    nb = SSM_BLOCKS
    gb = SSM_GROUPS // nb
    half = SSM_BLK_W // 2
    eye = jnp.eye(gb, dtype=F32)
    bt = jnp.concatenate([jnp.einsum('lkgph,gj->lkghjp', m.reshape(L, nb, gb, SSM_STATE, SSM_GROUP), eye)
                          .reshape(L, nb, LANES, half) for m in (bbr, bbi)], -1)
    cm = jnp.concatenate([jnp.einsum('lkghp,gj->lkgpjh', m.reshape(L, nb, gb, SSM_GROUP, SSM_STATE), eye)
                          .reshape(L, nb, half, LANES) for m in (c_re, -c_im)], 2)
    a = jnp.concatenate([ar.reshape(L, nb, half), ai.reshape(L, nb, half)], -1).reshape(L, 1, 2 * SSM_NS)
    a = jnp.broadcast_to(a, (L, SUBLANES, 2 * SSM_NS))
    return bt.astype(BF16), a, cm.astype(BF16)


def _prep_router(w_grp, b_grp, w_exp, b_exp):
    L, D, _ = w_grp.shape
    wr = jnp.zeros((L, D, LANES), F32).at[:, :, 0:MOE_GROUPS].set(w_grp)
    wr = wr.at[:, :, EXP_LANE0:EXP_LANE0 + MOE_EXPERTS].set(w_exp)
    br = jnp.zeros((L, 1, LANES), F32).at[:, 0, 0:MOE_GROUPS].set(b_grp)
    br = br.at[:, 0, EXP_LANE0:EXP_LANE0 + MOE_EXPERTS].set(b_exp)
    return wr, br


def _layer(B, S, consts, carry, w):
    x, xb = carry
    T, D = x.shape
    nc = S // CMP_STRIDE
    qw, kv = _proj(xb, w["w_a"], ((HEADS * LANES, BF16), (6 * LANES, BF16)), 512)
    ng, u_pool, bg = _proj(xb, w["w_b"], ((LANES, F32), (D // 2, F32), (3 * D, F32)), 256)
    u_ssm = _proj_time_major(xb, w["w_c"], B, S, 512).reshape(S * B, D // 2)
    kr = kv[:, 0:2 * LANES].reshape(T // CMP_STRIDE, CMP_STRIDE * 2 * LANES)
    kvc = _compress(kr, w["cw1"], w["cpe"], w["cw2"], B, nc)
    qaux, kaux, caux, ex = consts
    acc, sel, un = _nsa1(qw, kvc, ng, w["ov"], qaux, caux, B, S)
    lst, cnt = _chunk_lists(un, B, S)
    o_nsa = _nsa2(lst, cnt, qw, kv, sel, ng, acc, qaux, kaux, ex, B, S)
    o_pool = _pool(u_pool, w["pool_w"], w["pool_scale"], B, S, 512)
    o_ssm = _ssm(u_ssm, w["bt"], w["a"], w["cm"], w["d"], w["w_glu"], w["b_glu"], 128 * B)
    o_ssm = o_ssm.reshape(S, B * (D // 2))
    x, xb = _merge(o_nsa, o_pool, o_ssm, bg, x, w["wn"], w["wp"], w["ws"], w["wo"], w["ln1_g"], w["ln1_b"], B, S, 256)
    tm = 256
    ri, rw, cnt = _router(x, w["wr"], w["br"], 512)
    counts = cnt[0, 0:MOE_EXPERTS].astype(jnp.int32)
    padded = (counts + MOE_ROWS - 1) // MOE_ROWS * MOE_ROWS
    pend = jnp.cumsum(padded)
    pstart = pend - padded
    nblk = (2 * T) // MOE_ROWS + MOE_EXPERTS
    eids = jnp.arange(MOE_EXPERTS, dtype=jnp.int32)
    dest = jnp.sum(jnp.where(ri[:, 0:2, None] == eids, pstart, 0), -1) + ri[:, 2:4]
    dest3 = dest.reshape(T // tm, 1, 2 * tm)
    blk_e = jnp.minimum(jnp.sum(pend[None, :] <= jnp.arange(nblk, dtype=jnp.int32)[:, None] * MOE_ROWS, -1),
                        MOE_EXPERTS - 1).astype(jnp.int32)
    nused = (pend[-1:] // MOE_ROWS).astype(jnp.int32)
    xs = _dispatch(dest3, x, jnp.zeros((nblk * MOE_ROWS, D), F32), tm)
    ys = _experts(blk_e, nused, xs, w["wg"], w["wu"], w["wd"])
    x, xb = _combine(dest3, ys, x, rw, w["ln2_g"], w["ln2_b"], tm)
    return (x, xb), None


def kernel(x, w_in, cmp_pe_k, cmp_pe_v, cmp_w1_k, cmp_w2_k, cmp_w1_v, cmp_w2_v, pool_w, pool_scale, ssm_lam_re, ssm_lam_im, ssm_log_dt, ssm_b_re, ssm_b_im, ssm_c_re, ssm_c_im, ssm_d, ssm_w_glu, ssm_b_glu, w_up_nsa, w_up_pool, w_up_ssm, w_out, ln1_g, ln1_b, router_w_grp, router_b_grp, router_w_exp, router_b_exp, moe_w_gate, moe_w_up, moe_w_down, ln2_g, ln2_b):
    B, S, D = x.shape
    L = w_in.shape[0]
    assert B == 4 and D == D_MODEL and S % SEL_CHUNK == 0 and S // SEL_BLOCK <= LANES
    assert Q_TILE % (2 * SEL_BLOCK) == 0 and SEL_CHUNK % Q_TILE == 0
    w_a, w_b, w_c = _prep_in_proj(w_in)
    cw1, cpe, cw2 = _prep_compress(cmp_pe_k, cmp_pe_v, cmp_w1_k, cmp_w2_k, cmp_w1_v, cmp_w2_v)
    bt, a, cm = _prep_ssm(ssm_lam_re, ssm_lam_im, ssm_log_dt, ssm_b_re, ssm_b_im, ssm_c_re, ssm_c_im)
    wr, br = _prep_router(router_w_grp, router_b_grp, router_w_exp, router_b_exp)
    wn = w_up_nsa.reshape(L, KV_GROUPS, GQA, HEAD_DIM, D).transpose(0, 2, 1, 3, 4).reshape(L, HEADS * HEAD_DIM, D)
    row = lambda v: v.reshape(L, 1, -1)
    ws = dict(
        w_a=w_a, w_b=w_b, w_c=w_c, cw1=cw1, cpe=cpe, cw2=cw2,
        ov=jnp.broadcast_to(_overlap(S // CMP_STRIDE), (L, S // CMP_STRIDE, LANES)),
        pool_w=pool_w.astype(BF16), pool_scale=row(pool_scale),
        bt=bt, a=a, cm=cm, d=row(ssm_d), w_glu=ssm_w_glu.astype(BF16), b_glu=row(ssm_b_glu),
        wn=wn.astype(BF16), wp=w_up_pool.astype(BF16), ws=w_up_ssm.astype(BF16), wo=w_out.astype(BF16),
        ln1_g=row(ln1_g), ln1_b=row(ln1_b), wr=wr, br=br,
        wg=moe_w_gate, wu=moe_w_up, wd=moe_w_down,
        ln2_g=row(ln2_g), ln2_b=row(ln2_b))
    xf = x.reshape(B * S, D)
    (xf, _), _ = lax.scan(functools.partial(_layer, B, S, _nsa_consts(S)), (xf, xf.astype(BF16)), ws)
    return xf.reshape(B, S, D)
```

```python
import functools
import math

import jax
import jax.numpy as jnp
from jax import lax
from jax.experimental import pallas as pl
from jax.experimental.pallas import tpu as pltpu

F32 = jnp.float32
BF16 = jnp.bfloat16

D_MODEL = 1024
DEPTH = 4
HEAD_DIM = 64
HEADS = 8
KV_GROUPS = 2
GQA = HEADS // KV_GROUPS
CMP_BLOCK = 32
CMP_STRIDE = 16
CMP_HIDDEN = 128
SEL_BLOCK = 64
SEL_TOPN = 16
WINDOW = 512
FORCE_SCORE = 1.0e4
NEG_INF = -1.0e30
POOL_WINDOWS = (2, 4, 8, 16)
POOL_HALO = 16
SSM_GROUP = 16
SSM_GROUPS = 32
SSM_STATE = 64
SSM_NS = SSM_GROUPS * SSM_STATE
MOE_GROUPS = 4
MOE_EPG = 8
MOE_EXPERTS = 32
MOE_FF = 512
DN_ALPHA = (2 * DEPTH) ** 0.25
LN_EPS = 1e-5

LANES = 128
SUBLANES = 8
VMEM_LIMIT = 48 * 1024 * 1024
Q_TILE = 128
Q2_TILE = 128
SEL_CHUNK = 512
MOE_ROWS = 256
EXP_LANE0 = 32
DMA_UNROLL = 4
SEL_SHIFT = SEL_BLOCK.bit_length() - 1
EPG_SHIFT = MOE_EPG.bit_length() - 1


def _cparams(*sem):
    return pltpu.CompilerParams(dimension_semantics=sem, vmem_limit_bytes=VMEM_LIMIT)


def _gelu(x):
    return 0.5 * x * (1.0 + jnp.tanh(math.sqrt(2.0 / math.pi) * (x + 0.044715 * x * x * x)))


def _sigmoid(x):
    return 1.0 / (1.0 + jnp.exp(-x))


def _dot(a, b):
    return jnp.dot(a, b, preferred_element_type=F32)


def _dot_nt(a, b):
    return lax.dot_general(a, b, (((1,), (1,)), ((), ())), preferred_element_type=F32)


def _layer_norm(h, g, b):
    mu = jnp.mean(h, -1, keepdims=True)
    c = h - mu
    var = jnp.mean(c * c, -1, keepdims=True)
    return c * lax.rsqrt(var + LN_EPS) * g + b


def _proj_kernel(x_ref, w_ref, *out_refs, offs):
    x = x_ref[...]
    for o_ref, (a, n) in zip(out_refs, offs):
        o_ref[...] = _dot(x, w_ref[:, a:a + n]).astype(o_ref.dtype)


def _proj(xb, w, outs, tm):
    T, K = xb.shape
    N = w.shape[1]
    offs, a = [], 0
    for n, _ in outs:
        offs.append((a, n))
        a += n
    assert a == N and T % tm == 0
    return pl.pallas_call(
        functools.partial(_proj_kernel, offs=tuple(offs)),
        grid=(T // tm,),
        in_specs=[pl.BlockSpec((tm, K), lambda i: (i, 0)), pl.BlockSpec((K, N), lambda i: (0, 0))],
        out_specs=[pl.BlockSpec((tm, n), lambda i: (i, 0)) for n, _ in outs],
        out_shape=[jax.ShapeDtypeStruct((T, n), dt) for n, dt in outs],
        compiler_params=_cparams("parallel"),
    )(xb, w)


def _proj_tm_kernel(x_ref, w_ref, o_ref):
    o_ref[...] = _dot(x_ref[...], w_ref[...])


def _proj_time_major(xb, w, B, S, tm):
    K = xb.shape[1]
    N = w.shape[1]
    nt = S // tm
    return pl.pallas_call(
        _proj_tm_kernel,
        grid=(B, nt),
        in_specs=[pl.BlockSpec((tm, K), lambda b, i: (b * nt + i, 0)), pl.BlockSpec((K, N), lambda b, i: (0, 0))],
        out_specs=pl.BlockSpec((tm, N), lambda b, i: (i, b)),
        out_shape=jax.ShapeDtypeStruct((S, B * N), F32),
        compiler_params=_cparams("parallel", "parallel"),
    )(xb, w)


def _compress_kernel(kr_ref, w1_ref, pe_ref, w2_ref, out_ref):
    nc = kr_ref.shape[0]
    hw = 4 * CMP_HIDDEN
    a = _dot(kr_ref[...], w1_ref[...])
    pb = _dot(pe_ref[...], w1_ref[...])
    second = pltpu.roll(a[:, hw:], nc - 1, 0)
    row = lax.broadcasted_iota(jnp.int32, (nc, hw), 0)
    second = jnp.where(row < nc - 1, second, 0.0)
    hid = _gelu(a[:, :hw] + second + pb[0:1, :hw] + pb[1:2, hw:])
    out_ref[0] = _dot(hid.astype(BF16), w2_ref[...]).astype(out_ref.dtype)


def _compress(kr, w1, pe_in, w2, B, nc):
    return pl.pallas_call(
        _compress_kernel,
        grid=(B,),
        in_specs=[pl.BlockSpec((nc, kr.shape[1]), lambda b: (b, 0)),
                  pl.BlockSpec(w1.shape, lambda b: (0, 0)),
                  pl.BlockSpec(pe_in.shape, lambda b: (0, 0)),
                  pl.BlockSpec(w2.shape, lambda b: (0, 0))],
        out_specs=pl.BlockSpec((1, nc, 2 * LANES), lambda b: (b, 0, 0)),
        out_shape=jax.ShapeDtypeStruct((B, nc, 2 * LANES), BF16),
        compiler_params=_cparams("parallel"),
    )(kr, w1, pe_in, w2)


def _slope(h):
    return 2.0 ** (-8.0 * (h + 1) / HEADS)


def _stack_heads(q, g):
    return jnp.concatenate([q[:, (g * GQA + r) * LANES:(g * GQA + r + 1) * LANES] for r in range(GQA)], axis=0)


def _split3_dot(a, b):
    hi = a.astype(BF16)
    r1 = a - hi.astype(F32)
    mid = r1.astype(BF16)
    lo = (r1 - mid.astype(F32)).astype(BF16)
    return _dot(hi, b) + _dot(mid, b) + _dot(lo, b)


def _nsa1_kernel(q_ref, kvc_ref, ng_ref, ov_ref, qaux_ref, caux_ref, acc_ref, sel_ref, un_ref, sc_ref):
    Q = q_ref.shape[0]
    nc = kvc_ref.shape[1]
    t0 = pl.program_id(1) * Q
    q = q_ref[...]
    kc = jnp.concatenate([kvc_ref[0, :, 0:LANES], caux_ref[...]], axis=1)
    vc = kvc_ref[0, :, LANES:2 * LANES]
    gate = _sigmoid(ng_ref[...])
    qrow = lax.broadcasted_iota(jnp.int32, (Q, nc), 0)
    ccol = lax.broadcasted_iota(jnp.int32, (Q, nc), 1)
    bias = jnp.where((t0 + qrow) >= (ccol * CMP_STRIDE + (CMP_BLOCK - 1)), 0.0, NEG_INF)
    bias4 = jnp.concatenate([bias] * GQA, axis=0)
    lane = lax.broadcasted_iota(jnp.int32, (Q, LANES), 1)
    tb = jnp.right_shift(t0 + lax.broadcasted_iota(jnp.int32, (Q, LANES), 0), SEL_SHIFT)
    jrow = lax.broadcasted_iota(jnp.int32, (LANES, KV_GROUPS * Q), 0).astype(F32)
    outs = []
    for g in range(KV_GROUPS):
        qs = jnp.concatenate([_stack_heads(q, g), qaux_ref[g]], axis=1)
        s = _dot_nt(qs, kc) + bias4
        m = jnp.max(s, -1, keepdims=True)
        e = jnp.exp(s - m)
        inv = jnp.where(m > 0.5 * NEG_INF, 1.0 / jnp.maximum(jnp.sum(e, -1, keepdims=True), 1e-30), 0.0)
        p = e * inv
        oc = _dot(p.astype(BF16), vc)
        psum = p[0:Q]
        for r in range(1, GQA):
            psum = psum + p[r * Q:(r + 1) * Q]
        outs.append([gate[:, 3 * (g * GQA + r):3 * (g * GQA + r) + 1] * oc[r * Q:(r + 1) * Q] for r in range(GQA)])
        imp = _split3_dot(psum, ov_ref[...])
        forced = (lane == 0) | (lane == tb) | (lane == tb - 1)
        score = jnp.where(forced, FORCE_SCORE, jnp.where(lane <= tb, imp, -1.0))
        sc_ref[:, g * Q:(g + 1) * Q] = score.T
    work = sc_ref[...]
    taken = jnp.zeros(work.shape, F32)
    for _ in range(SEL_TOPN):
        best = jnp.max(work, 0, keepdims=True)
        first = jnp.min(jnp.where(work == best, jrow, float(LANES)), 0, keepdims=True)
        hit = jrow == first
        taken = jnp.where(hit, 1.0, taken)
        work = jnp.where(hit, -2.0, work)
    for g in range(KV_GROUPS):
        sel = taken[:, g * Q:(g + 1) * Q].T
        sel = jnp.where(lane <= tb, sel, 0.0)
        sel_ref[:, g * LANES:(g + 1) * LANES] = sel.astype(sel_ref.dtype)
        un_ref[:, g * LANES:(g + 1) * LANES] = jnp.broadcast_to(jnp.max(sel, 0, keepdims=True), (SUBLANES, LANES))
    for r in range(GQA):
        acc_ref[:, r * LANES:(r + 1) * LANES] = jnp.where(lane < HEAD_DIM, outs[0][r], outs[1][r])


def _nsa1(qw, kvc, ng, ov, qaux, caux, B, S):
    nq = S // Q_TILE
    nc = kvc.shape[1]
    T = B * S
    rows = lambda b, i: (b * nq + i, 0)
    return pl.pallas_call(
        _nsa1_kernel,
        grid=(B, nq),
        in_specs=[pl.BlockSpec((Q_TILE, HEADS * LANES), rows),
                  pl.BlockSpec((1, nc, 2 * LANES), lambda b, i: (b, 0, 0)),
                  pl.BlockSpec((Q_TILE, LANES), rows),
                  pl.BlockSpec((nc, LANES), lambda b, i: (0, 0)),
                  pl.BlockSpec(qaux.shape, lambda b, i: (0, 0, 0)),
                  pl.BlockSpec((nc, LANES), lambda b, i: (0, 0))],
        out_specs=[pl.BlockSpec((Q_TILE, GQA * LANES), rows),
                   pl.BlockSpec((Q_TILE, KV_GROUPS * LANES), rows),
                   pl.BlockSpec((SUBLANES, KV_GROUPS * LANES), rows)],
        out_shape=[jax.ShapeDtypeStruct((T, GQA * LANES), F32),
                   jax.ShapeDtypeStruct((T, KV_GROUPS * LANES), BF16),
                   jax.ShapeDtypeStruct((B * nq * SUBLANES, KV_GROUPS * LANES), F32)],
        scratch_shapes=[pltpu.VMEM((LANES, KV_GROUPS * Q_TILE), F32)],
        compiler_params=_cparams("parallel", "parallel"),
    )(qw, kvc, ng, ov, qaux, caux)


def _nsa2_kernel(lst_ref, cnt_ref, q_ref, ks_ref, vs_ref, kw_ref, vw_ref, sel_ref, ng_ref, acc_ref,
                 qaux_ref, kaux_ref, ex_ref, o_ref, m_sc, l_sc, a_sc, ow_sc, s0_sc, s1_sc, sd_sc):
    Q = q_ref.shape[0]
    C = SEL_CHUNK
    WK = WINDOW + Q
    maxch = ex_ref.shape[1] // C
    tile = pl.program_id(0) * pl.num_programs(1) + pl.program_id(1)
    t0 = pl.program_id(1) * Q
    q = q_ref[...]
    gate = _sigmoid(ng_ref[...])
    lane = lax.broadcasted_iota(jnp.int32, (Q, LANES), 1)
    cdiag = t0 // C
    start = pl.multiple_of(jnp.maximum(t0 - WINDOW, 0), Q)
    wrow = lax.broadcasted_iota(jnp.int32, (Q, WK), 0)
    wcol = lax.broadcasted_iota(jnp.int32, (Q, WK), 1)
    wdist = (t0 + wrow) - (start + wcol)
    wbias = jnp.where((wdist >= 0) & (wdist < WINDOW), 0.0, NEG_INF)
    wbias4 = jnp.concatenate([wbias] * GQA, axis=0)
    qrow = lax.broadcasted_iota(jnp.int32, (Q, C), 0)
    kcol = lax.broadcasted_iota(jnp.int32, (Q, C), 1)
    outs = []
    for g in range(KV_GROUPS):
        qs = jnp.concatenate([_stack_heads(q, g), qaux_ref[g]], axis=1)
        selg = sel_ref[:, g * LANES:(g + 1) * LANES]
        m_sc[...] = jnp.full(m_sc.shape, NEG_INF, F32)
        l_sc[...] = jnp.zeros(l_sc.shape, F32)
        a_sc[...] = jnp.zeros(a_sc.shape, F32)

        def scores(c, s_out, qs=qs, selg=selg):
            k0 = pl.multiple_of(c * C, C)
            kk = jnp.concatenate([ks_ref[pl.ds(k0, C), :], kaux_ref[pl.ds(k0, C), :]], axis=1)
            valid = (_dot(selg, ex_ref[:, pl.ds(k0, C)]) > 0.5) & ((t0 + qrow) >= (k0 + kcol))
            bias = jnp.where(valid, 0.0, NEG_INF)
            s_out[...] = _dot_nt(qs, kk) + jnp.concatenate([bias] * GQA, axis=0)

        def accumulate(c, s_in):
            k0 = pl.multiple_of(c * C, C)
            s = s_in[...]
            m_old = m_sc[...]
            m_new = jnp.maximum(m_old, jnp.max(s, -1, keepdims=True))
            alpha = jnp.exp(m_old - m_new)
            e = jnp.exp(s - m_new[:, 0:1])
            l_sc[...] = alpha * l_sc[...] + jnp.sum(e, -1, keepdims=True)
            a_sc[...] = alpha * a_sc[...] + _dot(e.astype(BF16), vs_ref[pl.ds(k0, C), :])
            m_sc[...] = m_new

        base = (tile * KV_GROUPS + g) * maxch
        scores(cdiag, sd_sc)
        scores(lst_ref[base], s0_sc)
        kw = jnp.concatenate([kw_ref[pl.ds(start, WK), :], kaux_ref[pl.ds(start, WK), :]], axis=1)
        sw = _dot_nt(qs, kw) + wbias4
        e = jnp.exp(sw - jnp.max(sw, -1, keepdims=True))
        ow_sc[...] = _dot(e.astype(BF16), vw_ref[pl.ds(start, WK), :]) / jnp.maximum(jnp.sum(e, -1, keepdims=True), 1e-30)

        def step(j, carry, scores=scores, accumulate=accumulate, base=base):
            cur = lst_ref[base + j]
            nxt = lst_ref[base + j + 1]

            @pl.when(j % 2 == 0)
            def _():
                scores(nxt, s1_sc)
                accumulate(cur, s0_sc)

            @pl.when(j % 2 == 1)
            def _():
                scores(nxt, s0_sc)
                accumulate(cur, s1_sc)

            return carry

        lax.fori_loop(0, cnt_ref[tile * KV_GROUPS + g], step, 0)
        accumulate(cdiag, sd_sc)
        o_w = ow_sc[...]
        o_s = a_sc[...] / jnp.maximum(l_sc[...], 1e-30)
        og = []
        for r in range(GQA):
            h = g * GQA + r
            rows = slice(r * Q, (r + 1) * Q)
            og.append(gate[:, 3 * h + 1:3 * h + 2] * o_s[rows] + gate[:, 3 * h + 2:3 * h + 3] * o_w[rows])
        outs.append(og)
    for r in range(GQA):
        cols = slice(r * LANES, (r + 1) * LANES)
        o_ref[:, cols] = (acc_ref[:, cols] + jnp.where(lane < HEAD_DIM, outs[0][r], outs[1][r])).astype(o_ref.dtype)


def _chunk_lists(un, B, S):
    nq = S // Q2_TILE
    maxch = S // SEL_CHUNK
    per = SEL_CHUNK // SEL_BLOCK
    un = un.reshape(B * nq, Q2_TILE // Q_TILE, SUBLANES, KV_GROUPS, LANES)[:, :, 0, :, :maxch * per].max(1)
    flags = un.reshape(B * nq, KV_GROUPS, maxch, per).max(-1) > 0.5
    cdiag = (jnp.arange(B * nq, dtype=jnp.int32) % nq) * Q2_TILE // SEL_CHUNK
    flags = flags & (jnp.arange(maxch, dtype=jnp.int32)[None, None, :] < cdiag[:, None, None])
    pos = jnp.cumsum(flags.astype(jnp.int32), -1)
    hit = flags[..., None, :] & (pos[..., None, :] == jnp.arange(1, maxch + 1, dtype=jnp.int32)[:, None])
    lst = jnp.sum(jnp.where(hit, jnp.arange(maxch, dtype=jnp.int32), 0), -1)
    return lst.astype(jnp.int32).reshape(-1), pos[..., -1].reshape(-1)


def _nsa2(lst, cnt, qw, kv, sel, ng, acc, qaux, kaux, ex, B, S):
    Q = Q2_TILE
    nq = S // Q
    T = B * S
    rows = lambda b, i, *_: (b * nq + i, 0)
    kvspec = lambda col: pl.BlockSpec((S, LANES), lambda b, i, *_, col=col: (b, col))
    const = lambda shp: pl.BlockSpec(shp, lambda b, i, *_: tuple(0 for _ in shp))
    return pl.pallas_call(
        _nsa2_kernel,
        grid_spec=pltpu.PrefetchScalarGridSpec(
            num_scalar_prefetch=2, grid=(B, nq),
            in_specs=[pl.BlockSpec((Q, HEADS * LANES), rows),
                      kvspec(2), kvspec(3), kvspec(4), kvspec(5),
                      pl.BlockSpec((Q, KV_GROUPS * LANES), rows),
                      pl.BlockSpec((Q, LANES), rows),
                      pl.BlockSpec((Q, GQA * LANES), rows),
                      const(qaux.shape), const(kaux.shape), const(ex.shape)],
            out_specs=pl.BlockSpec((Q, GQA * LANES), rows),
            scratch_shapes=[pltpu.VMEM((GQA * Q, LANES), F32)] * 4
            + [pltpu.VMEM((GQA * Q, SEL_CHUNK), F32)] * 3),
        out_shape=jax.ShapeDtypeStruct((T, GQA * LANES), BF16),
        compiler_params=_cparams("parallel", "parallel"),
    )(lst, cnt, qw, kv, kv, kv, kv, sel, ng, acc, qaux, kaux, ex)


def _nsa_consts(S):
    h = jnp.arange(HEADS, dtype=F32).reshape(KV_GROUPS, GQA, 1, 1)
    slope = jnp.exp2(-8.0 * (h + 1.0) / HEADS)
    lane = jnp.arange(LANES)[None, None, None, :]
    qaux = jnp.where(lane == 0, slope * SEL_BLOCK, jnp.where(lane == 1, slope, 0.0))
    qaux1, qaux2 = (jnp.broadcast_to(qaux, (KV_GROUPS, GQA, n, LANES)).reshape(KV_GROUPS, GQA * n, LANES).astype(BF16)
                    for n in (Q_TILE, Q2_TILE))

    def pos_cols(pos):
        l2 = jnp.arange(LANES)[None, :]
        return jnp.where(l2 == 0, pos[:, None] // SEL_BLOCK, jnp.where(l2 == 1, pos[:, None] % SEL_BLOCK, 0))

    kaux = pos_cols(jnp.arange(S))
    caux = pos_cols(jnp.arange(S // CMP_STRIDE) * CMP_STRIDE + (CMP_BLOCK - 1))
    ex = (jnp.arange(LANES)[:, None] == (jnp.arange(S)[None, :] // SEL_BLOCK))
    return qaux1, qaux2, kaux.astype(BF16), caux.astype(BF16), ex.astype(BF16)


def _pool_kernel(u_ref, halo_ref, w_ref, sc_ref, o_ref, x_sc):
    ts = u_ref.shape[0]
    i = pl.program_id(1)
    x_sc[0:POOL_HALO] = jnp.where(i > 0, halo_ref[...], 0.0)
    x_sc[POOL_HALO:POOL_HALO + ts] = u_ref[...]
    t = i * ts + lax.broadcasted_iota(jnp.int32, (ts, LANES), 0)
    for g, w in enumerate(POOL_WINDOWS):
        cols = slice(g * LANES, (g + 1) * LANES)
        cur = x_sc[POOL_HALO:POOL_HALO + ts, cols]
        acc = cur
        for k in range(1, w):
            acc = acc + x_sc[POOL_HALO - k:POOL_HALO - k + ts, cols]
        cnt = jnp.minimum(t + 1, w).astype(F32)
        pooled = acc / cnt - cur
        o_ref[:, cols] = (_dot(pooled.astype(BF16), w_ref[g]) * sc_ref[:, cols]).astype(o_ref.dtype)


def _pool(u, pool_w, pool_scale, B, S, ts):
    nt = S // ts
    hb = ts // POOL_HALO
    C = u.shape[1]
    return pl.pallas_call(
        _pool_kernel,
        grid=(B, nt),
        in_specs=[pl.BlockSpec((ts, C), lambda b, i: (b * nt + i, 0)),
                  pl.BlockSpec((POOL_HALO, C), lambda b, i: (jnp.maximum((b * nt + i) * hb - 1, 0), 0)),
                  pl.BlockSpec(pool_w.shape, lambda b, i: (0, 0, 0)),
                  pl.BlockSpec((1, C), lambda b, i: (0, 0))],
        out_specs=pl.BlockSpec((ts, C), lambda b, i: (b * nt + i, 0)),
        out_shape=jax.ShapeDtypeStruct((B * S, C), BF16),
        scratch_shapes=[pltpu.VMEM((POOL_HALO + ts, C), F32)],
        compiler_params=_cparams("parallel", "parallel"),
    )(u, u, pool_w, pool_scale)


SSM_PASS = 8
SSM_BLOCKS = SSM_GROUPS * SSM_GROUP // LANES
SSM_BLK_W = 2 * SSM_NS // SSM_BLOCKS


def _ssm_kernel(u_ref, bt_ref, a_ref, c_ref, d_ref, wg_ref, bg_ref, o_ref, xh, hst):
    rows = u_ref.shape[0]

    @pl.when(pl.program_id(0) == 0)
    def _():
        hst[...] = jnp.zeros(hst.shape, F32)

    u = u_ref[...]
    ub = u.astype(BF16)
    for k in range(SSM_BLOCKS):
        xh[:, k * SSM_BLK_W:(k + 1) * SSM_BLK_W] = _dot(ub[:, k * LANES:(k + 1) * LANES], bt_ref[k])
    low = lax.broadcasted_iota(jnp.int32, (SUBLANES, LANES), 0) < 4
    nchunk = SSM_NS // LANES
    per_blk = nchunk // SSM_BLOCKS
    for p in range(nchunk // SSM_PASS):
        cols = [p * SSM_PASS + j for j in range(SSM_PASS)]
        start = [(c // per_blk) * SSM_BLK_W + (c % per_blk) * LANES for c in cols]
        re = [slice(s, s + LANES) for s in start]
        im = [slice(s + SSM_BLK_W // 2, s + SSM_BLK_W // 2 + LANES) for s in start]
        ar = [a_ref[:, s] for s in re]
        ai = [a_ref[:, s] for s in im]

        def tile(k, carry, re=re, im=im, ar=ar, ai=ai):
            cr, ci = carry
            r0 = pl.multiple_of(k * SUBLANES, SUBLANES)
            ncr, nci = [], []
            for j in range(SSM_PASS):
                vr = xh[pl.ds(r0, SUBLANES), re[j]]
                vi = xh[pl.ds(r0, SUBLANES), im[j]]
                t1r = ar[j] * cr[j] - ai[j] * ci[j] + vr
                t1i = ar[j] * ci[j] + ai[j] * cr[j] + vi
                pr = pltpu.roll(t1r, 4, 0)
                pi = pltpu.roll(t1i, 4, 0)
                t2r = ar[j] * pr - ai[j] * pi + vr
                t2i = ar[j] * pi + ai[j] * pr + vi
                xh[pl.ds(r0, SUBLANES), re[j]] = jnp.where(low, t1r, t2r)
                xh[pl.ds(r0, SUBLANES), im[j]] = jnp.where(low, t1i, t2i)
                ncr.append(pltpu.roll(t2r, 4, 0))
                nci.append(pltpu.roll(t2i, 4, 0))
            return tuple(ncr), tuple(nci)

        cr0 = tuple(hst[:, s] for s in re)
        ci0 = tuple(hst[:, s] for s in im)
        cr, ci = lax.fori_loop(0, rows // SUBLANES, tile, (cr0, ci0))
        for j in range(SSM_PASS):
            hst[:, re[j]] = cr[j]
            hst[:, im[j]] = ci[j]
    y = jnp.concatenate([_dot(xh[:, k * SSM_BLK_W:(k + 1) * SSM_BLK_W].astype(BF16), c_ref[k])
                         for k in range(SSM_BLOCKS)], axis=1) + d_ref[...] * u
    z = _gelu(y)
    o_ref[...] = (z * _sigmoid(_dot(z.astype(BF16), wg_ref[...]) + bg_ref[...])).astype(o_ref.dtype)


def _ssm(u_tm, bt, a_b, cmat, d_skip, w_glu, b_glu, rows):
    n, C = u_tm.shape
    const = lambda shp: pl.BlockSpec(shp, lambda i: tuple(0 for _ in shp))
    return pl.pallas_call(
        _ssm_kernel,
        grid=(n // rows,),
        in_specs=[pl.BlockSpec((rows, C), lambda i: (i, 0)),
                  const(bt.shape), const(a_b.shape), const(cmat.shape), const(d_skip.shape),
                  const(w_glu.shape), const(b_glu.shape)],
        out_specs=pl.BlockSpec((rows, C), lambda i: (i, 0)),
        out_shape=jax.ShapeDtypeStruct((n, C), BF16),
        scratch_shapes=[pltpu.VMEM((rows, 2 * SSM_NS), F32), pltpu.VMEM((SUBLANES, 2 * SSM_NS), F32)],
        compiler_params=_cparams("arbitrary"),
    )(u_tm, bt, a_b, cmat, d_skip, w_glu, b_glu)


def _merge_kernel(on_ref, op_ref, os_ref, xin_ref, x_ref, wbg_ref, wn_ref, wp_ref, ws_ref, wo_ref, g_ref, b_ref,
                  xo_ref, xb_ref):
    D = x_ref.shape[1]
    xin = xin_ref[...]
    m = None
    for k, (o_ref, w_ref) in enumerate(((on_ref, wn_ref), (op_ref, wp_ref), (os_ref, ws_ref))):
        gate = _sigmoid(_dot(xin, wbg_ref[:, k * D:(k + 1) * D]))
        term = gate * _dot(o_ref[...], w_ref[...])
        m = term if m is None else m + term
    h = DN_ALPHA * x_ref[...] + _dot(m.astype(BF16), wo_ref[...])
    y = _layer_norm(h, g_ref[...], b_ref[...])
    xo_ref[...] = y
    xb_ref[...] = y.astype(BF16)


def _merge(o_nsa, o_pool, o_ssm_tm, xb, x, wbg, wn, wp, ws, wo, g, b, B, S, tm):
    nt = S // tm
    T, D = x.shape
    W = o_nsa.shape[1]
    rows = lambda bb, i: (bb * nt + i, 0)
    const = lambda shp: pl.BlockSpec(shp, lambda bb, i: tuple(0 for _ in shp))
    return pl.pallas_call(
        _merge_kernel,
        grid=(B, nt),
        in_specs=[pl.BlockSpec((tm, W), rows), pl.BlockSpec((tm, W), rows),
                  pl.BlockSpec((tm, W), lambda bb, i: (i, bb)),
                  pl.BlockSpec((tm, D), rows), pl.BlockSpec((tm, D), rows),
                  const(wbg.shape), const(wn.shape), const(wp.shape), const(ws.shape), const(wo.shape),
                  const(g.shape), const(b.shape)],
        out_specs=[pl.BlockSpec((tm, D), rows), pl.BlockSpec((tm, D), rows)],
        out_shape=[jax.ShapeDtypeStruct((T, D), F32), jax.ShapeDtypeStruct((T, D), BF16)],
        compiler_params=_cparams("parallel", "parallel"),
    )(o_nsa, o_pool, o_ssm_tm, xb, x, wbg, wn, wp, ws, wo, g, b)


def _router_kernel(x_ref, wr_ref, br_ref, ri_ref, rw_ref, cnt_ref, carry):
    tm = x_ref.shape[0]

    @pl.when(pl.program_id(0) == 0)
    def _():
        carry[...] = jnp.zeros(carry.shape, F32)

    x = x_ref[...]
    xh = x.astype(BF16)
    xl = (x - xh.astype(F32)).astype(BF16)
    both = _dot(xh, wr_ref[...])
    logits = both[:, 0:LANES] + both[:, LANES:2 * LANES] + _dot(xl, wr_ref[:, 0:LANES]) + br_ref[...]
    lane = lax.broadcasted_iota(jnp.int32, (tm, LANES), 1)
    big = jnp.int32(LANES)
    isg = lane < MOE_GROUPS
    lg = jnp.where(isg, logits, NEG_INF)
    eg = jnp.where(isg, jnp.exp(lg - jnp.max(lg, -1, keepdims=True)), 0.0)
    pgv = jnp.where(isg, eg / jnp.sum(eg, -1, keepdims=True), -1.0)
    p_g = jnp.max(pgv, -1, keepdims=True)
    g_sel = jnp.min(jnp.where(pgv == p_g, lane, big), -1, keepdims=True)
    ine = ((lane >= EXP_LANE0) & (lane < EXP_LANE0 + MOE_EXPERTS)
           & (jnp.right_shift(lane - EXP_LANE0, EPG_SHIFT) == g_sel))
    le = jnp.where(ine, logits, NEG_INF)
    ee = jnp.where(ine, jnp.exp(le - jnp.max(le, -1, keepdims=True)), 0.0)
    p = jnp.where(ine, ee / jnp.sum(ee, -1, keepdims=True), -1.0)
    p1 = jnp.max(p, -1, keepdims=True)
    i1 = jnp.min(jnp.where(p == p1, lane, big), -1, keepdims=True)
    prest = jnp.where(lane == i1, -1.0, p)
    p2 = jnp.max(prest, -1, keepdims=True)
    i2 = jnp.min(jnp.where(prest == p2, lane, big), -1, keepdims=True)
    den = p1 + p2
    w0 = p1 / den * p_g
    w1 = p2 / den * p_g
    e0 = i1 - EXP_LANE0
    e1 = i2 - EXP_LANE0
    is0 = lane == e0
    is1 = lane == e1
    onehot = jnp.where(is0 | is1, 1.0, 0.0)
    r = lax.broadcasted_iota(jnp.int32, (tm, tm), 0)
    c = lax.broadcasted_iota(jnp.int32, (tm, tm), 1)
    tri = jnp.where(r > c, 1.0, 0.0).astype(BF16)
    cum = _dot(tri, onehot.astype(BF16)) + carry[0:1, :]
    rank0 = jnp.sum(jnp.where(is0, cum, 0.0), -1, keepdims=True).astype(jnp.int32)
    rank1 = jnp.sum(jnp.where(is1, cum, 0.0), -1, keepdims=True).astype(jnp.int32)
    carry[...] = carry[...] + jnp.sum(onehot, 0, keepdims=True)
    cnt_ref[...] = carry[...]
    zi = jnp.zeros((tm, LANES), jnp.int32)
    ri_ref[...] = jnp.where(lane == 0, e0, jnp.where(lane == 1, e1, jnp.where(lane == 2, rank0, jnp.where(lane == 3, rank1, zi))))
    rw_ref[...] = jnp.where(lane == 0, w0, jnp.where(lane == 1, w1, 0.0))


def _router(x, wr, br, tm):
    T, D = x.shape
    return pl.pallas_call(
        _router_kernel,
        grid=(T // tm,),
        in_specs=[pl.BlockSpec((tm, D), lambda i: (i, 0)),
                  pl.BlockSpec(wr.shape, lambda i: (0, 0)), pl.BlockSpec(br.shape, lambda i: (0, 0))],
        out_specs=[pl.BlockSpec((tm, LANES), lambda i: (i, 0)), pl.BlockSpec((tm, LANES), lambda i: (i, 0)),
                   pl.BlockSpec((SUBLANES, LANES), lambda i: (0, 0))],
        out_shape=[jax.ShapeDtypeStruct((T, LANES), jnp.int32), jax.ShapeDtypeStruct((T, LANES), F32),
                   jax.ShapeDtypeStruct((SUBLANES, LANES), F32)],
        scratch_shapes=[pltpu.VMEM((SUBLANES, LANES), F32)],
        compiler_params=_cparams("arbitrary"),
    )(x, wr, br)


def _row_copy(src, s, dst, d, sem):
    return pltpu.make_async_copy(src.at[pl.ds(s, 1)], dst.at[pl.ds(d, 1)], sem)


def _dispatch_kernel(dest_ref, x_ref, xs_in, xs_hbm, sem, *, tm):
    del xs_in

    def issue(r, c):
        for k in range(2):
            _row_copy(x_ref, r, xs_hbm, dest_ref[0, 0, 2 * r + k], sem).start(priority=k)
        return c

    lax.fori_loop(0, tm, issue, 0, unroll=DMA_UNROLL)
    pltpu.make_async_copy(xs_hbm.at[pl.ds(0, 2 * tm)], xs_hbm.at[pl.ds(0, 2 * tm)], sem).wait()


def _dispatch(dest3, x, xs0, tm):
    T = x.shape[0]
    return pl.pallas_call(
        functools.partial(_dispatch_kernel, tm=tm),
        grid=(T // tm,),
        in_specs=[pl.BlockSpec((1, 1, 2 * tm), lambda i: (i, 0, 0), memory_space=pltpu.SMEM),
                  pl.BlockSpec((tm, x.shape[1]), lambda i: (i, 0)), pl.BlockSpec(memory_space=pl.ANY)],
        out_specs=pl.BlockSpec(memory_space=pl.ANY),
        out_shape=jax.ShapeDtypeStruct(xs0.shape, xs0.dtype),
        scratch_shapes=[pltpu.SemaphoreType.DMA(())],
        input_output_aliases={2: 0},
        compiler_params=_cparams("arbitrary"),
    )(dest3, x, xs0)


def _expert_kernel(be_ref, nu_ref, xs_ref, wg_ref, wu_ref, wd_ref, ys_ref):
    i = pl.program_id(0)

    @pl.when(i < nu_ref[0])
    def _():
        x = xs_ref[...].astype(BF16)
        a = _dot(x, wg_ref[0].astype(BF16))
        h = a * _sigmoid(a) * _dot(x, wu_ref[0].astype(BF16))
        ys_ref[...] = _dot(h.astype(BF16), wd_ref[0].astype(BF16))

    @pl.when(i >= nu_ref[0])
    def _():
        ys_ref[...] = jnp.zeros(ys_ref.shape, F32)


def _experts(blk_e, nused, xs, wg, wu, wd):
    P, D = xs.shape
    F = wg.shape[2]
    nblk = P // MOE_ROWS
    return pl.pallas_call(
        _expert_kernel,
        grid_spec=pltpu.PrefetchScalarGridSpec(
            num_scalar_prefetch=2, grid=(nblk,),
            in_specs=[pl.BlockSpec((MOE_ROWS, D), lambda i, be, nu: (i, 0)),
                      pl.BlockSpec((1, D, F), lambda i, be, nu: (be[i], 0, 0)),
                      pl.BlockSpec((1, D, F), lambda i, be, nu: (be[i], 0, 0)),
                      pl.BlockSpec((1, F, D), lambda i, be, nu: (be[i], 0, 0))],
            out_specs=pl.BlockSpec((MOE_ROWS, D), lambda i, be, nu: (i, 0))),
        out_shape=jax.ShapeDtypeStruct((P, D), F32),
        compiler_params=_cparams("arbitrary"),
    )(blk_e, nused, xs, wg, wu, wd)


def _combine_kernel(dest_ref, ys_hbm, x_ref, rw_ref, g_ref, b_ref, xo_ref, xb_ref, buf, sem, *, tm):
    def issue(r, c):
        for k in range(2):
            _row_copy(ys_hbm, dest_ref[0, 0, 2 * r + k], buf.at[k], r, sem).start(priority=k)
        return c

    lax.fori_loop(0, tm, issue, 0, unroll=DMA_UNROLL)
    for k in range(2):
        pltpu.make_async_copy(ys_hbm.at[pl.ds(0, tm)], buf.at[k], sem).wait()
    w = rw_ref[...]
    h = DN_ALPHA * x_ref[...] + w[:, 0:1] * buf[0] + w[:, 1:2] * buf[1]
    y = _layer_norm(h, g_ref[...], b_ref[...])
    xo_ref[...] = y
    xb_ref[...] = y.astype(BF16)


def _combine(dest3, ys, x, rw, g, b, tm):
    T, D = x.shape
    return pl.pallas_call(
        functools.partial(_combine_kernel, tm=tm),
        grid=(T // tm,),
        in_specs=[pl.BlockSpec((1, 1, 2 * tm), lambda i: (i, 0, 0), memory_space=pltpu.SMEM),
                  pl.BlockSpec(memory_space=pl.ANY),
                  pl.BlockSpec((tm, D), lambda i: (i, 0)), pl.BlockSpec((tm, LANES), lambda i: (i, 0)),
                  pl.BlockSpec(g.shape, lambda i: (0, 0)), pl.BlockSpec(b.shape, lambda i: (0, 0))],
        out_specs=[pl.BlockSpec((tm, D), lambda i: (i, 0)), pl.BlockSpec((tm, D), lambda i: (i, 0))],
        out_shape=[jax.ShapeDtypeStruct((T, D), F32), jax.ShapeDtypeStruct((T, D), BF16)],
        scratch_shapes=[pltpu.VMEM((2, tm, D), F32), pltpu.SemaphoreType.DMA(())],
        compiler_params=_cparams("arbitrary"),
    )(dest3, ys, x, rw, g, b)


def _prep_in_proj(w_in):
    L, D, _ = w_in.shape
    nw = HEADS * HEAD_DIM
    kvw = 6 * KV_GROUPS * HEAD_DIM
    o = 0
    wq = w_in[..., o:o + nw]; o += nw
    wkv = w_in[..., o:o + kvw]; o += kvw
    wng = w_in[..., o:o + 3 * HEADS]; o += 3 * HEADS
    wpool = w_in[..., o:o + D // 2]; o += D // 2
    wssm = w_in[..., o:o + D // 2]; o += D // 2
    wbg = w_in[..., o:]
    wq = wq.reshape(L, D, KV_GROUPS, GQA, HEAD_DIM) * (HEAD_DIM ** -0.5)
    z = jnp.zeros_like(wq[:, :, 0])
    wq_wide = jnp.stack([jnp.concatenate([wq[:, :, 0], z], -1), jnp.concatenate([z, wq[:, :, 1]], -1)], 2)
    wq_wide = wq_wide.reshape(L, D, HEADS * LANES)
    wng = jnp.pad(wng, ((0, 0), (0, 0), (0, LANES - 3 * HEADS)))
    w_a = jnp.concatenate([wq_wide, wkv, wng, wpool], -1).astype(BF16)
    return w_a, wbg.astype(BF16), wssm.astype(BF16)


def _prep_compress(pe_k, pe_v, w1k, w2k, w1v, w2v):
    L = pe_k.shape[0]
    half = CMP_BLOCK // 2
    cw = 2 * KV_GROUPS * HEAD_DIM
    eye_s = jnp.eye(2, dtype=F32)
    eye_g = jnp.eye(KV_GROUPS, dtype=F32)
    w1r = jnp.stack([w1k, w1v], 1).reshape(L, 2, 2, half, HEAD_DIM, CMP_HIDDEN)
    w1 = jnp.einsum('aspldh,st,gj->alsgdptjh', w1r, eye_s, eye_g)
    per = jnp.stack([pe_k, pe_v], 1).reshape(L, 2, 2, half, HEAD_DIM)
    pe = jnp.broadcast_to(per.transpose(0, 2, 3, 1, 4)[:, :, :, :, None, :], (L, 2, half, 2, KV_GROUPS, HEAD_DIM))
    pe = jnp.pad(pe.reshape(L, 2, half * cw), ((0, 0), (0, SUBLANES - 2), (0, 0)))
    w2 = jnp.einsum('ashd,st,gj->asghtjd', jnp.stack([w2k, w2v], 1), eye_s, eye_g)
    w1 = w1.reshape(L, half * cw, 2 * 2 * KV_GROUPS * CMP_HIDDEN)
    w2 = w2.reshape(L, 2 * KV_GROUPS * CMP_HIDDEN, 2 * KV_GROUPS * HEAD_DIM)
    return w1.astype(BF16), pe.astype(BF16), w2.astype(BF16)


def _overlap(nc):
    i0 = jnp.arange(nc)[:, None] * CMP_STRIDE
    j0 = jnp.arange(LANES)[None, :] * SEL_BLOCK
    return ((i0 < j0 + SEL_BLOCK) & (i0 + CMP_BLOCK > j0)).astype(BF16)


def _prep_ssm(lam_re, lam_im, log_dt, b_re, b_im, c_re, c_im):
    L = lam_re.shape[0]
    dt = jnp.exp(log_dt)[..., None]
    mag = jnp.exp(lam_re * dt)
    ar, ai = mag * jnp.cos(lam_im * dt), mag * jnp.sin(lam_im * dt)
    den = lam_re * lam_re + lam_im * lam_im
    cr = ((ar - 1.0) * lam_re + ai * lam_im) / den
    ci = (ai * lam_re - (ar - 1.0) * lam_im) / den
    bbr = cr[..., None] * b_re - ci[..., None] * b_im
    bbi = cr[..., None] * b_im + ci[..., None] * b_re
    nb = SSM_BLOCKS
    gb = SSM_GROUPS // nb
    half = SSM_BLK_W // 2
    eye = jnp.eye(gb, dtype=F32)
    bt = jnp.concatenate([jnp.einsum('lkgph,gj->lkghjp', m.reshape(L, nb, gb, SSM_STATE, SSM_GROUP), eye)
                          .reshape(L, nb, LANES, half) for m in (bbr, bbi)], -1)
    cm = jnp.concatenate([jnp.einsum('lkghp,gj->lkgpjh', m.reshape(L, nb, gb, SSM_GROUP, SSM_STATE), eye)
                          .reshape(L, nb, half, LANES) for m in (c_re, -c_im)], 2)
    a = jnp.concatenate([ar.reshape(L, nb, half), ai.reshape(L, nb, half)], -1).reshape(L, 1, 2 * SSM_NS)
    a = jnp.broadcast_to(a, (L, SUBLANES, 2 * SSM_NS))
    return bt.astype(BF16), a, cm.astype(BF16)


def _prep_router(w_grp, b_grp, w_exp, b_exp):
    L, D, _ = w_grp.shape
    wr = jnp.zeros((L, D, LANES), F32).at[:, :, 0:MOE_GROUPS].set(w_grp)
    wr = wr.at[:, :, EXP_LANE0:EXP_LANE0 + MOE_EXPERTS].set(w_exp)
    br = jnp.zeros((L, 1, LANES), F32).at[:, 0, 0:MOE_GROUPS].set(b_grp)
    br = br.at[:, 0, EXP_LANE0:EXP_LANE0 + MOE_EXPERTS].set(b_exp)
    hi = wr.astype(BF16)
    lo = (wr - hi.astype(F32)).astype(BF16)
    return jnp.concatenate([hi, lo], -1), br


def _layer(B, S, consts, carry, w):
    x, xb = carry
    T, D = x.shape
    nc = S // CMP_STRIDE
    xb_in = xb
    qw, kv, ng, u_pool = _proj(xb, w["w_a"], ((HEADS * LANES, BF16), (6 * LANES, BF16), (LANES, F32), (D // 2, F32)), 512)
    u_ssm = _proj_time_major(xb, w["w_c"], B, S, 512).reshape(S * B, D // 2)
    kr = kv[:, 0:2 * LANES].reshape(T // CMP_STRIDE, CMP_STRIDE * 2 * LANES)
    kvc = _compress(kr, w["cw1"], w["cpe"], w["cw2"], B, nc)
    qaux1, qaux2, kaux, caux, ex = consts
    acc, sel, un = _nsa1(qw, kvc, ng, w["ov"], qaux1, caux, B, S)
    lst, cnt = _chunk_lists(un, B, S)
    o_nsa = _nsa2(lst, cnt, qw, kv, sel, ng, acc, qaux2, kaux, ex, B, S)
    o_pool = _pool(u_pool, w["pool_w"], w["pool_scale"], B, S, 512)
    o_ssm = _ssm(u_ssm, w["bt"], w["a"], w["cm"], w["d"], w["w_glu"], w["b_glu"], 128 * B)
    o_ssm = o_ssm.reshape(S, B * (D // 2))
    x, xb = _merge(o_nsa, o_pool, o_ssm, xb_in, x, w["w_bg"], w["wn"], w["wp"], w["ws"], w["wo"],
                   w["ln1_g"], w["ln1_b"], B, S, 256)
    tm = 256
    ri, rw, cnt = _router(x, w["wr"], w["br"], 512)
    counts = cnt[0, 0:MOE_EXPERTS].astype(jnp.int32)
    padded = (counts + MOE_ROWS - 1) // MOE_ROWS * MOE_ROWS
    pend = jnp.cumsum(padded)
    pstart = pend - padded
    nblk = (2 * T) // MOE_ROWS + MOE_EXPERTS
    eids = jnp.arange(MOE_EXPERTS, dtype=jnp.int32)
    dest = jnp.sum(jnp.where(ri[:, 0:2, None] == eids, pstart, 0), -1) + ri[:, 2:4]
    dest3 = dest.reshape(T // tm, 1, 2 * tm)
    blk_e = jnp.minimum(jnp.sum(pend[None, :] <= jnp.arange(nblk, dtype=jnp.int32)[:, None] * MOE_ROWS, -1),
                        MOE_EXPERTS - 1).astype(jnp.int32)
    nused = (pend[-1:] // MOE_ROWS).astype(jnp.int32)
    xs = _dispatch(dest3, x, jnp.zeros((nblk * MOE_ROWS, D), F32), tm)
    ys = _experts(blk_e, nused, xs, w["wg"], w["wu"], w["wd"])
    x, xb = _combine(dest3, ys, x, rw, w["ln2_g"], w["ln2_b"], tm)
    return (x, xb), None


def kernel(x, w_in, cmp_pe_k, cmp_pe_v, cmp_w1_k, cmp_w2_k, cmp_w1_v, cmp_w2_v, pool_w, pool_scale, ssm_lam_re, ssm_lam_im, ssm_log_dt, ssm_b_re, ssm_b_im, ssm_c_re, ssm_c_im, ssm_d, ssm_w_glu, ssm_b_glu, w_up_nsa, w_up_pool, w_up_ssm, w_out, ln1_g, ln1_b, router_w_grp, router_b_grp, router_w_exp, router_b_exp, moe_w_gate, moe_w_up, moe_w_down, ln2_g, ln2_b):
    B, S, D = x.shape
    L = w_in.shape[0]
    assert B == 4 and D == D_MODEL and S % SEL_CHUNK == 0 and S // SEL_BLOCK <= LANES
    assert Q_TILE == LANES and Q2_TILE % Q_TILE == 0 and SEL_CHUNK % Q2_TILE == 0 and WINDOW % Q2_TILE == 0
    w_a, w_bg, w_c = _prep_in_proj(w_in)
    cw1, cpe, cw2 = _prep_compress(cmp_pe_k, cmp_pe_v, cmp_w1_k, cmp_w2_k, cmp_w1_v, cmp_w2_v)
    bt, a, cm = _prep_ssm(ssm_lam_re, ssm_lam_im, ssm_log_dt, ssm_b_re, ssm_b_im, ssm_c_re, ssm_c_im)
    wr, br = _prep_router(router_w_grp, router_b_grp, router_w_exp, router_b_exp)
    wn = w_up_nsa.reshape(L, KV_GROUPS, GQA, HEAD_DIM, D).transpose(0, 2, 1, 3, 4).reshape(L, HEADS * HEAD_DIM, D)
    row = lambda v: v.reshape(L, 1, -1)
    ws = dict(
        w_a=w_a, w_bg=w_bg, w_c=w_c, cw1=cw1, cpe=cpe, cw2=cw2,
        ov=jnp.broadcast_to(_overlap(S // CMP_STRIDE), (L, S // CMP_STRIDE, LANES)),
        pool_w=pool_w.astype(BF16), pool_scale=row(pool_scale),
        bt=bt, a=a, cm=cm, d=row(ssm_d), w_glu=ssm_w_glu.astype(BF16), b_glu=row(ssm_b_glu),
        wn=wn.astype(BF16), wp=w_up_pool.astype(BF16), ws=w_up_ssm.astype(BF16), wo=w_out.astype(BF16),
        ln1_g=row(ln1_g), ln1_b=row(ln1_b), wr=wr, br=br,
        wg=moe_w_gate, wu=moe_w_up, wd=moe_w_down,
        ln2_g=row(ln2_g), ln2_b=row(ln2_b))
    xf = x.reshape(B * S, D)
    (xf, _), _ = lax.scan(functools.partial(_layer, B, S, _nsa_consts(S)), (xf, xf.astype(BF16)), ws)
    return xf.reshape(B, S, D)
```

```python
import functools
import math

import jax
import jax.numpy as jnp
from jax import lax
from jax.experimental import pallas as pl
from jax.experimental.pallas import tpu as pltpu

F32 = jnp.float32
BF16 = jnp.bfloat16

D_MODEL = 1024
DEPTH = 4
HEAD_DIM = 64
HEADS = 8
KV_GROUPS = 2
GQA = HEADS // KV_GROUPS
CMP_BLOCK = 32
CMP_STRIDE = 16
CMP_HIDDEN = 128
SEL_BLOCK = 64
SEL_TOPN = 16
WINDOW = 512
FORCE_SCORE = 1.0e4
NEG_INF = -1.0e30
POOL_WINDOWS = (2, 4, 8, 16)
POOL_HALO = 16
SSM_GROUP = 16
SSM_GROUPS = 32
SSM_STATE = 64
SSM_NS = SSM_GROUPS * SSM_STATE
MOE_GROUPS = 4
MOE_EPG = 8
MOE_EXPERTS = 32
MOE_FF = 512
DN_ALPHA = (2 * DEPTH) ** 0.25
LN_EPS = 1e-5

LANES = 128
SUBLANES = 8
VMEM_LIMIT = 48 * 1024 * 1024
Q_TILE = 128
Q2_TILE = 128
SEL_CHUNK = 512
MOE_ROWS = 256
EXP_LANE0 = 32
DMA_UNROLL = 4
SEL_SHIFT = SEL_BLOCK.bit_length() - 1
EPG_SHIFT = MOE_EPG.bit_length() - 1


def _cparams(*sem):
    return pltpu.CompilerParams(dimension_semantics=sem, vmem_limit_bytes=VMEM_LIMIT)


def _gelu(x):
    return 0.5 * x * (1.0 + jnp.tanh(math.sqrt(2.0 / math.pi) * (x + 0.044715 * x * x * x)))


def _sigmoid(x):
    return 1.0 / (1.0 + jnp.exp(-x))


def _dot(a, b):
    return jnp.dot(a, b, preferred_element_type=F32)


def _dot_nt(a, b):
    return lax.dot_general(a, b, (((1,), (1,)), ((), ())), preferred_element_type=F32)


def _layer_norm(h, g, b):
    mu = jnp.mean(h, -1, keepdims=True)
    c = h - mu
    var = jnp.mean(c * c, -1, keepdims=True)
    return c * lax.rsqrt(var + LN_EPS) * g + b


def _proj_kernel(x_ref, w_ref, *out_refs, offs):
    x = x_ref[...]
    for o_ref, (a, n) in zip(out_refs, offs):
        o_ref[...] = _dot(x, w_ref[:, a:a + n]).astype(o_ref.dtype)


def _proj(xb, w, outs, tm):
    T, K = xb.shape
    N = w.shape[1]
    offs, a = [], 0
    for n, _ in outs:
        offs.append((a, n))
        a += n
    assert a == N and T % tm == 0
    return pl.pallas_call(
        functools.partial(_proj_kernel, offs=tuple(offs)),
        grid=(T // tm,),
        in_specs=[pl.BlockSpec((tm, K), lambda i: (i, 0)), pl.BlockSpec((K, N), lambda i: (0, 0))],
        out_specs=[pl.BlockSpec((tm, n), lambda i: (i, 0)) for n, _ in outs],
        out_shape=[jax.ShapeDtypeStruct((T, n), dt) for n, dt in outs],
        compiler_params=_cparams("parallel"),
    )(xb, w)


def _proj_tm_kernel(x_ref, w_ref, o_ref):
    o_ref[...] = _dot(x_ref[...], w_ref[...])


def _proj_time_major(xb, w, B, S, tm):
    K = xb.shape[1]
    N = w.shape[1]
    nt = S // tm
    return pl.pallas_call(
        _proj_tm_kernel,
        grid=(B, nt),
        in_specs=[pl.BlockSpec((tm, K), lambda b, i: (b * nt + i, 0)), pl.BlockSpec((K, N), lambda b, i: (0, 0))],
        out_specs=pl.BlockSpec((tm, N), lambda b, i: (i, b)),
        out_shape=jax.ShapeDtypeStruct((S, B * N), F32),
        compiler_params=_cparams("parallel", "parallel"),
    )(xb, w)


def _compress_kernel(kr_ref, w1_ref, pe_ref, w2_ref, out_ref):
    nc = kr_ref.shape[0]
    hw = 4 * CMP_HIDDEN
    a = _dot(kr_ref[...], w1_ref[...])
    pb = _dot(pe_ref[...], w1_ref[...])
    second = pltpu.roll(a[:, hw:], nc - 1, 0)
    row = lax.broadcasted_iota(jnp.int32, (nc, hw), 0)
    second = jnp.where(row < nc - 1, second, 0.0)
    hid = _gelu(a[:, :hw] + second + pb[0:1, :hw] + pb[1:2, hw:])
    out_ref[0] = _dot(hid.astype(BF16), w2_ref[...]).astype(out_ref.dtype)


def _compress(kr, w1, pe_in, w2, B, nc):
    return pl.pallas_call(
        _compress_kernel,
        grid=(B,),
        in_specs=[pl.BlockSpec((nc, kr.shape[1]), lambda b: (b, 0)),
                  pl.BlockSpec(w1.shape, lambda b: (0, 0)),
                  pl.BlockSpec(pe_in.shape, lambda b: (0, 0)),
                  pl.BlockSpec(w2.shape, lambda b: (0, 0))],
        out_specs=pl.BlockSpec((1, nc, 2 * LANES), lambda b: (b, 0, 0)),
        out_shape=jax.ShapeDtypeStruct((B, nc, 2 * LANES), BF16),
        compiler_params=_cparams("parallel"),
    )(kr, w1, pe_in, w2)


def _slope(h):
    return 2.0 ** (-8.0 * (h + 1) / HEADS)


def _stack_heads(q, g):
    return jnp.concatenate([q[:, (g * GQA + r) * LANES:(g * GQA + r + 1) * LANES] for r in range(GQA)], axis=0)


def _split3_dot(a, b):
    hi = a.astype(BF16)
    r1 = a - hi.astype(F32)
    mid = r1.astype(BF16)
    lo = (r1 - mid.astype(F32)).astype(BF16)
    return _dot(hi, b) + _dot(mid, b) + _dot(lo, b)


def _nsa1_kernel(q_ref, kvc_ref, ng_ref, ov_ref, qaux_ref, caux_ref, acc_ref, sel_ref, un_ref, sc_ref, oc_sc):
    Q = q_ref.shape[0]
    nc = kvc_ref.shape[1]
    qb = pl.program_id(1)
    t0 = qb * Q
    lane = lax.broadcasted_iota(jnp.int32, (Q, LANES), 1)
    tb = jnp.right_shift(t0 + lax.broadcasted_iota(jnp.int32, (Q, LANES), 0), SEL_SHIFT)
    jrow = lax.broadcasted_iota(jnp.int32, (LANES, KV_GROUPS * Q), 0).astype(F32)

    def attend(n):
        q = q_ref[...]
        kc = jnp.concatenate([kvc_ref[0, 0:n, 0:LANES], caux_ref[0:n, :]], axis=1)
        vc = kvc_ref[0, 0:n, LANES:2 * LANES]
        qrow = lax.broadcasted_iota(jnp.int32, (Q, n), 0)
        ccol = lax.broadcasted_iota(jnp.int32, (Q, n), 1)
        bias = jnp.where((t0 + qrow) >= (ccol * CMP_STRIDE + (CMP_BLOCK - 1)), 0.0, NEG_INF)
        bias4 = jnp.concatenate([bias] * GQA, axis=0)
        for g in range(KV_GROUPS):
            qs = jnp.concatenate([_stack_heads(q, g), qaux_ref[g]], axis=1)
            s = _dot_nt(qs, kc) + bias4
            m = jnp.max(s, -1, keepdims=True)
            e = jnp.exp(s - m)
            inv = jnp.where(m > 0.5 * NEG_INF, 1.0 / jnp.maximum(jnp.sum(e, -1, keepdims=True), 1e-30), 0.0)
            p = e * inv
            oc_sc[g] = _dot(p.astype(BF16), vc)
            psum = p[0:Q]
            for r in range(1, GQA):
                psum = psum + p[r * Q:(r + 1) * Q]
            imp = _split3_dot(psum, ov_ref[0:n, :])
            forced = (lane == 0) | (lane == tb) | (lane == tb - 1)
            score = jnp.where(forced, FORCE_SCORE, jnp.where(lane <= tb, imp, -1.0))
            sc_ref[:, g * Q:(g + 1) * Q] = score.T

    step = min(LANES, nc)
    need = (t0 + Q - CMP_BLOCK) // CMP_STRIDE + 1
    variant = (need + step - 1) // step
    for v in range(1, nc // step + 1):
        pl.when(variant == v)(functools.partial(attend, v * step))
    gate = _sigmoid(ng_ref[...])
    outs = [[gate[:, 3 * (g * GQA + r):3 * (g * GQA + r) + 1] * oc_sc[g, r * Q:(r + 1) * Q, :] for r in range(GQA)]
            for g in range(KV_GROUPS)]
    work = sc_ref[...]
    taken = jnp.zeros(work.shape, F32)
    for _ in range(SEL_TOPN):
        best = jnp.max(work, 0, keepdims=True)
        first = jnp.min(jnp.where(work == best, jrow, float(LANES)), 0, keepdims=True)
        hit = jrow == first
        taken = jnp.where(hit, 1.0, taken)
        work = jnp.where(hit, -2.0, work)
    for g in range(KV_GROUPS):
        sel = taken[:, g * Q:(g + 1) * Q].T
        sel = jnp.where(lane <= tb, sel, 0.0)
        sel_ref[:, g * LANES:(g + 1) * LANES] = sel.astype(sel_ref.dtype)
        un_ref[:, g * LANES:(g + 1) * LANES] = jnp.broadcast_to(jnp.max(sel, 0, keepdims=True), (SUBLANES, LANES))
    for r in range(GQA):
        acc_ref[:, r * LANES:(r + 1) * LANES] = jnp.where(lane < HEAD_DIM, outs[0][r], outs[1][r])


def _nsa1(qw, kvc, ng, ov, qaux, caux, B, S):
    nq = S // Q_TILE
    nc = kvc.shape[1]
    T = B * S
    rows = lambda b, i: (b * nq + i, 0)
    return pl.pallas_call(
        _nsa1_kernel,
        grid=(B, nq),
        in_specs=[pl.BlockSpec((Q_TILE, HEADS * LANES), rows),
                  pl.BlockSpec((1, nc, 2 * LANES), lambda b, i: (b, 0, 0)),
                  pl.BlockSpec((Q_TILE, LANES), rows),
                  pl.BlockSpec((nc, LANES), lambda b, i: (0, 0)),
                  pl.BlockSpec(qaux.shape, lambda b, i: (0, 0, 0)),
                  pl.BlockSpec((nc, LANES), lambda b, i: (0, 0))],
        out_specs=[pl.BlockSpec((Q_TILE, GQA * LANES), rows),
                   pl.BlockSpec((Q_TILE, KV_GROUPS * LANES), rows),
                   pl.BlockSpec((SUBLANES, KV_GROUPS * LANES), rows)],
        out_shape=[jax.ShapeDtypeStruct((T, GQA * LANES), F32),
                   jax.ShapeDtypeStruct((T, KV_GROUPS * LANES), BF16),
                   jax.ShapeDtypeStruct((B * nq * SUBLANES, KV_GROUPS * LANES), F32)],
        scratch_shapes=[pltpu.VMEM((LANES, KV_GROUPS * Q_TILE), F32),
                        pltpu.VMEM((KV_GROUPS, GQA * Q_TILE, LANES), F32)],
        compiler_params=_cparams("parallel", "parallel"),
    )(qw, kvc, ng, ov, qaux, caux)


def _nsa2_kernel(lst_ref, cnt_ref, q_ref, ks_ref, vs_ref, kw_ref, vw_ref, sel_ref, ng_ref, acc_ref,
                 qaux_ref, kaux_ref, ex_ref, o_ref, m_sc, l_sc, a_sc, ow_sc, s0_sc, s1_sc, sd_sc):
    Q = q_ref.shape[0]
    C = SEL_CHUNK
    WK = WINDOW + Q
    maxch = ex_ref.shape[1] // C
    tile = pl.program_id(0) * pl.num_programs(1) + pl.program_id(1)
    t0 = pl.program_id(1) * Q
    q = q_ref[...]
    gate = _sigmoid(ng_ref[...])
    lane = lax.broadcasted_iota(jnp.int32, (Q, LANES), 1)
    cdiag = t0 // C
    start = pl.multiple_of(jnp.maximum(t0 - WINDOW, 0), Q)
    wrow = lax.broadcasted_iota(jnp.int32, (Q, WK), 0)
    wcol = lax.broadcasted_iota(jnp.int32, (Q, WK), 1)
    wdist = (t0 + wrow) - (start + wcol)
    wbias = jnp.where((wdist >= 0) & (wdist < WINDOW), 0.0, NEG_INF)
    wbias4 = jnp.concatenate([wbias] * GQA, axis=0)
    qrow = lax.broadcasted_iota(jnp.int32, (Q, C), 0)
    kcol = lax.broadcasted_iota(jnp.int32, (Q, C), 1)
    outs = []
    for g in range(KV_GROUPS):
        qs = jnp.concatenate([_stack_heads(q, g), qaux_ref[g]], axis=1)
        selg = sel_ref[:, g * LANES:(g + 1) * LANES]
        m_sc[...] = jnp.full(m_sc.shape, NEG_INF, F32)
        l_sc[...] = jnp.zeros(l_sc.shape, F32)
        a_sc[...] = jnp.zeros(a_sc.shape, F32)

        def scores(c, s_out, qs=qs, selg=selg):
            k0 = pl.multiple_of(c * C, C)
            kk = jnp.concatenate([ks_ref[pl.ds(k0, C), :], kaux_ref[pl.ds(k0, C), :]], axis=1)
            valid = (_dot(selg, ex_ref[:, pl.ds(k0, C)]) > 0.5) & ((t0 + qrow) >= (k0 + kcol))
            bias = jnp.where(valid, 0.0, NEG_INF)
            s_out[...] = _dot_nt(qs, kk) + jnp.concatenate([bias] * GQA, axis=0)

        def accumulate(c, s_in):
            k0 = pl.multiple_of(c * C, C)
            s = s_in[...]
            m_old = m_sc[...]
            m_new = jnp.maximum(m_old, jnp.max(s, -1, keepdims=True))
            alpha = jnp.exp(m_old - m_new)
            e = jnp.exp(s - m_new[:, 0:1])
            l_sc[...] = alpha * l_sc[...] + jnp.sum(e, -1, keepdims=True)
            a_sc[...] = alpha * a_sc[...] + _dot(e.astype(BF16), vs_ref[pl.ds(k0, C), :])
            m_sc[...] = m_new

        base = (tile * KV_GROUPS + g) * maxch
        scores(cdiag, sd_sc)
        scores(lst_ref[base], s0_sc)
        kw = jnp.concatenate([kw_ref[pl.ds(start, WK), :], kaux_ref[pl.ds(start, WK), :]], axis=1)
        sw = _dot_nt(qs, kw) + wbias4
        e = jnp.exp(sw - jnp.max(sw, -1, keepdims=True))
        ow_sc[...] = _dot(e.astype(BF16), vw_ref[pl.ds(start, WK), :]) / jnp.maximum(jnp.sum(e, -1, keepdims=True), 1e-30)

        def step(j, carry, scores=scores, accumulate=accumulate, base=base):
            cur = lst_ref[base + j]
            nxt = lst_ref[base + j + 1]

            @pl.when(j % 2 == 0)
            def _():
                scores(nxt, s1_sc)
                accumulate(cur, s0_sc)

            @pl.when(j % 2 == 1)
            def _():
                scores(nxt, s0_sc)
                accumulate(cur, s1_sc)

            return carry

        lax.fori_loop(0, cnt_ref[tile * KV_GROUPS + g], step, 0)
        accumulate(cdiag, sd_sc)
        o_w = ow_sc[...]
        o_s = a_sc[...] / jnp.maximum(l_sc[...], 1e-30)
        og = []
        for r in range(GQA):
            h = g * GQA + r
            rows = slice(r * Q, (r + 1) * Q)
            og.append(gate[:, 3 * h + 1:3 * h + 2] * o_s[rows] + gate[:, 3 * h + 2:3 * h + 3] * o_w[rows])
        outs.append(og)
    for r in range(GQA):
        cols = slice(r * LANES, (r + 1) * LANES)
        o_ref[:, cols] = (acc_ref[:, cols] + jnp.where(lane < HEAD_DIM, outs[0][r], outs[1][r])).astype(o_ref.dtype)


def _chunk_lists(un, B, S):
    nq = S // Q2_TILE
    maxch = S // SEL_CHUNK
    per = SEL_CHUNK // SEL_BLOCK
    un = un.reshape(B * nq, Q2_TILE // Q_TILE, SUBLANES, KV_GROUPS, LANES)[:, :, 0, :, :maxch * per].max(1)
    flags = un.reshape(B * nq, KV_GROUPS, maxch, per).max(-1) > 0.5
    cdiag = (jnp.arange(B * nq, dtype=jnp.int32) % nq) * Q2_TILE // SEL_CHUNK
    flags = flags & (jnp.arange(maxch, dtype=jnp.int32)[None, None, :] < cdiag[:, None, None])
    pos = jnp.cumsum(flags.astype(jnp.int32), -1)
    hit = flags[..., None, :] & (pos[..., None, :] == jnp.arange(1, maxch + 1, dtype=jnp.int32)[:, None])
    lst = jnp.sum(jnp.where(hit, jnp.arange(maxch, dtype=jnp.int32), 0), -1)
    return lst.astype(jnp.int32).reshape(-1), pos[..., -1].reshape(-1)


def _nsa2(lst, cnt, qw, kv, sel, ng, acc, qaux, kaux, ex, B, S):
    Q = Q2_TILE
    nq = S // Q
    T = B * S
    rows = lambda b, i, *_: (b * nq + i, 0)
    kvspec = lambda col: pl.BlockSpec((S, LANES), lambda b, i, *_, col=col: (b, col))
    const = lambda shp: pl.BlockSpec(shp, lambda b, i, *_: tuple(0 for _ in shp))
    return pl.pallas_call(
        _nsa2_kernel,
        grid_spec=pltpu.PrefetchScalarGridSpec(
            num_scalar_prefetch=2, grid=(B, nq),
            in_specs=[pl.BlockSpec((Q, HEADS * LANES), rows),
                      kvspec(2), kvspec(3), kvspec(4), kvspec(5),
                      pl.BlockSpec((Q, KV_GROUPS * LANES), rows),
                      pl.BlockSpec((Q, LANES), rows),
                      pl.BlockSpec((Q, GQA * LANES), rows),
                      const(qaux.shape), const(kaux.shape), const(ex.shape)],
            out_specs=pl.BlockSpec((Q, GQA * LANES), rows),
            scratch_shapes=[pltpu.VMEM((GQA * Q, LANES), F32)] * 4
            + [pltpu.VMEM((GQA * Q, SEL_CHUNK), F32)] * 3),
        out_shape=jax.ShapeDtypeStruct((T, GQA * LANES), BF16),
        compiler_params=_cparams("parallel", "parallel"),
    )(lst, cnt, qw, kv, kv, kv, kv, sel, ng, acc, qaux, kaux, ex)


def _nsa_consts(S):
    h = jnp.arange(HEADS, dtype=F32).reshape(KV_GROUPS, GQA, 1, 1)
    slope = jnp.exp2(-8.0 * (h + 1.0) / HEADS)
    lane = jnp.arange(LANES)[None, None, None, :]
    qaux = jnp.where(lane == 0, slope * SEL_BLOCK, jnp.where(lane == 1, slope, 0.0))
    qaux1, qaux2 = (jnp.broadcast_to(qaux, (KV_GROUPS, GQA, n, LANES)).reshape(KV_GROUPS, GQA * n, LANES).astype(BF16)
                    for n in (Q_TILE, Q2_TILE))

    def pos_cols(pos):
        l2 = jnp.arange(LANES)[None, :]
        return jnp.where(l2 == 0, pos[:, None] // SEL_BLOCK, jnp.where(l2 == 1, pos[:, None] % SEL_BLOCK, 0))

    kaux = pos_cols(jnp.arange(S))
    caux = pos_cols(jnp.arange(S // CMP_STRIDE) * CMP_STRIDE + (CMP_BLOCK - 1))
    ex = (jnp.arange(LANES)[:, None] == (jnp.arange(S)[None, :] // SEL_BLOCK))
    return qaux1, qaux2, kaux.astype(BF16), caux.astype(BF16), ex.astype(BF16)


def _pool_kernel(u_ref, halo_ref, w_ref, sc_ref, o_ref, x_sc):
    ts = u_ref.shape[0]
    i = pl.program_id(1)
    x_sc[0:POOL_HALO] = jnp.where(i > 0, halo_ref[...], 0.0)
    x_sc[POOL_HALO:POOL_HALO + ts] = u_ref[...]
    t = i * ts + lax.broadcasted_iota(jnp.int32, (ts, LANES), 0)
    for g, w in enumerate(POOL_WINDOWS):
        cols = slice(g * LANES, (g + 1) * LANES)
        cur = x_sc[POOL_HALO:POOL_HALO + ts, cols]
        acc = cur
        for k in range(1, w):
            acc = acc + x_sc[POOL_HALO - k:POOL_HALO - k + ts, cols]
        cnt = jnp.minimum(t + 1, w).astype(F32)
        pooled = acc / cnt - cur
        o_ref[:, cols] = (_dot(pooled.astype(BF16), w_ref[g]) * sc_ref[:, cols]).astype(o_ref.dtype)


def _pool(u, pool_w, pool_scale, B, S, ts):
    nt = S // ts
    hb = ts // POOL_HALO
    C = u.shape[1]
    return pl.pallas_call(
        _pool_kernel,
        grid=(B, nt),
        in_specs=[pl.BlockSpec((ts, C), lambda b, i: (b * nt + i, 0)),
                  pl.BlockSpec((POOL_HALO, C), lambda b, i: (jnp.maximum((b * nt + i) * hb - 1, 0), 0)),
                  pl.BlockSpec(pool_w.shape, lambda b, i: (0, 0, 0)),
                  pl.BlockSpec((1, C), lambda b, i: (0, 0))],
        out_specs=pl.BlockSpec((ts, C), lambda b, i: (b * nt + i, 0)),
        out_shape=jax.ShapeDtypeStruct((B * S, C), BF16),
        scratch_shapes=[pltpu.VMEM((POOL_HALO + ts, C), F32)],
        compiler_params=_cparams("parallel", "parallel"),
    )(u, u, pool_w, pool_scale)


SSM_PASS = 8
SSM_BLOCKS = SSM_GROUPS * SSM_GROUP // LANES
SSM_BLK_W = 2 * SSM_NS // SSM_BLOCKS


def _ssm_kernel(u_ref, bt_ref, a_ref, c_ref, d_ref, wg_ref, bg_ref, o_ref, xh, hst):
    rows = u_ref.shape[0]

    @pl.when(pl.program_id(0) == 0)
    def _():
        hst[...] = jnp.zeros(hst.shape, F32)

    u = u_ref[...]
    ub = u.astype(BF16)
    for k in range(SSM_BLOCKS):
        xh[:, k * SSM_BLK_W:(k + 1) * SSM_BLK_W] = _dot(ub[:, k * LANES:(k + 1) * LANES], bt_ref[k])
    low = lax.broadcasted_iota(jnp.int32, (SUBLANES, LANES), 0) < 4
    nchunk = SSM_NS // LANES
    per_blk = nchunk // SSM_BLOCKS
    for p in range(nchunk // SSM_PASS):
        cols = [p * SSM_PASS + j for j in range(SSM_PASS)]
        start = [(c // per_blk) * SSM_BLK_W + (c % per_blk) * LANES for c in cols]
        re = [slice(s, s + LANES) for s in start]
        im = [slice(s + SSM_BLK_W // 2, s + SSM_BLK_W // 2 + LANES) for s in start]
        ar = [a_ref[:, s] for s in re]
        ai = [a_ref[:, s] for s in im]

        def tile(k, carry, re=re, im=im, ar=ar, ai=ai):
            cr, ci = carry
            r0 = pl.multiple_of(k * SUBLANES, SUBLANES)
            ncr, nci = [], []
            for j in range(SSM_PASS):
                vr = xh[pl.ds(r0, SUBLANES), re[j]]
                vi = xh[pl.ds(r0, SUBLANES), im[j]]
                t1r = ar[j] * cr[j] - ai[j] * ci[j] + vr
                t1i = ar[j] * ci[j] + ai[j] * cr[j] + vi
                pr = pltpu.roll(t1r, 4, 0)
                pi = pltpu.roll(t1i, 4, 0)
                t2r = ar[j] * pr - ai[j] * pi + vr
                t2i = ar[j] * pi + ai[j] * pr + vi
                xh[pl.ds(r0, SUBLANES), re[j]] = jnp.where(low, t1r, t2r)
                xh[pl.ds(r0, SUBLANES), im[j]] = jnp.where(low, t1i, t2i)
                ncr.append(pltpu.roll(t2r, 4, 0))
                nci.append(pltpu.roll(t2i, 4, 0))
            return tuple(ncr), tuple(nci)

        cr0 = tuple(hst[:, s] for s in re)
        ci0 = tuple(hst[:, s] for s in im)
        cr, ci = lax.fori_loop(0, rows // SUBLANES, tile, (cr0, ci0))
        for j in range(SSM_PASS):
            hst[:, re[j]] = cr[j]
            hst[:, im[j]] = ci[j]
    y = jnp.concatenate([_dot(xh[:, k * SSM_BLK_W:(k + 1) * SSM_BLK_W].astype(BF16), c_ref[k])
                         for k in range(SSM_BLOCKS)], axis=1) + d_ref[...] * u
    z = _gelu(y)
    o_ref[...] = (z * _sigmoid(_dot(z.astype(BF16), wg_ref[...]) + bg_ref[...])).astype(o_ref.dtype)


def _ssm(u_tm, bt, a_b, cmat, d_skip, w_glu, b_glu, rows):
    n, C = u_tm.shape
    const = lambda shp: pl.BlockSpec(shp, lambda i: tuple(0 for _ in shp))
    return pl.pallas_call(
        _ssm_kernel,
        grid=(n // rows,),
        in_specs=[pl.BlockSpec((rows, C), lambda i: (i, 0)),
                  const(bt.shape), const(a_b.shape), const(cmat.shape), const(d_skip.shape),
                  const(w_glu.shape), const(b_glu.shape)],
        out_specs=pl.BlockSpec((rows, C), lambda i: (i, 0)),
        out_shape=jax.ShapeDtypeStruct((n, C), BF16),
        scratch_shapes=[pltpu.VMEM((rows, 2 * SSM_NS), F32), pltpu.VMEM((SUBLANES, 2 * SSM_NS), F32)],
        compiler_params=_cparams("arbitrary"),
    )(u_tm, bt, a_b, cmat, d_skip, w_glu, b_glu)


def _merge_kernel(on_ref, op_ref, os_ref, xin_ref, x_ref, wbg_ref, wn_ref, wp_ref, ws_ref, wo_ref, g_ref, b_ref,
                  xo_ref, xb_ref):
    D = x_ref.shape[1]
    xin = xin_ref[...]
    m = None
    for k, (o_ref, w_ref) in enumerate(((on_ref, wn_ref), (op_ref, wp_ref), (os_ref, ws_ref))):
        gate = _sigmoid(_dot(xin, wbg_ref[:, k * D:(k + 1) * D]))
        term = gate * _dot(o_ref[...], w_ref[...])
        m = term if m is None else m + term
    h = DN_ALPHA * x_ref[...] + _dot(m.astype(BF16), wo_ref[...])
    y = _layer_norm(h, g_ref[...], b_ref[...])
    xo_ref[...] = y
    xb_ref[...] = y.astype(BF16)


def _merge(o_nsa, o_pool, o_ssm_tm, xb, x, wbg, wn, wp, ws, wo, g, b, B, S, tm):
    nt = S // tm
    T, D = x.shape
    W = o_nsa.shape[1]
    rows = lambda bb, i: (bb * nt + i, 0)
    const = lambda shp: pl.BlockSpec(shp, lambda bb, i: tuple(0 for _ in shp))
    return pl.pallas_call(
        _merge_kernel,
        grid=(B, nt),
        in_specs=[pl.BlockSpec((tm, W), rows), pl.BlockSpec((tm, W), rows),
                  pl.BlockSpec((tm, W), lambda bb, i: (i, bb)),
                  pl.BlockSpec((tm, D), rows), pl.BlockSpec((tm, D), rows),
                  const(wbg.shape), const(wn.shape), const(wp.shape), const(ws.shape), const(wo.shape),
                  const(g.shape), const(b.shape)],
        out_specs=[pl.BlockSpec((tm, D), rows), pl.BlockSpec((tm, D), rows)],
        out_shape=[jax.ShapeDtypeStruct((T, D), F32), jax.ShapeDtypeStruct((T, D), BF16)],
        compiler_params=_cparams("parallel", "parallel"),
    )(o_nsa, o_pool, o_ssm_tm, xb, x, wbg, wn, wp, ws, wo, g, b)


def _router_kernel(x_ref, wr_ref, br_ref, ri_ref, rw_ref, cnt_ref, carry):
    tm = x_ref.shape[0]

    @pl.when(pl.program_id(0) == 0)
    def _():
        carry[...] = jnp.zeros(carry.shape, F32)

    x = x_ref[...]
    xh = x.astype(BF16)
    xl = (x - xh.astype(F32)).astype(BF16)
    both = _dot(xh, wr_ref[...])
    logits = both[:, 0:LANES] + both[:, LANES:2 * LANES] + _dot(xl, wr_ref[:, 0:LANES]) + br_ref[...]
    lane = lax.broadcasted_iota(jnp.int32, (tm, LANES), 1)
    big = jnp.int32(LANES)
    isg = lane < MOE_GROUPS
    lg = jnp.where(isg, logits, NEG_INF)
    eg = jnp.where(isg, jnp.exp(lg - jnp.max(lg, -1, keepdims=True)), 0.0)
    pgv = jnp.where(isg, eg / jnp.sum(eg, -1, keepdims=True), -1.0)
    p_g = jnp.max(pgv, -1, keepdims=True)
    g_sel = jnp.min(jnp.where(pgv == p_g, lane, big), -1, keepdims=True)
    ine = ((lane >= EXP_LANE0) & (lane < EXP_LANE0 + MOE_EXPERTS)
           & (jnp.right_shift(lane - EXP_LANE0, EPG_SHIFT) == g_sel))
    le = jnp.where(ine, logits, NEG_INF)
    ee = jnp.where(ine, jnp.exp(le - jnp.max(le, -1, keepdims=True)), 0.0)
    p = jnp.where(ine, ee / jnp.sum(ee, -1, keepdims=True), -1.0)
    p1 = jnp.max(p, -1, keepdims=True)
    i1 = jnp.min(jnp.where(p == p1, lane, big), -1, keepdims=True)
    prest = jnp.where(lane == i1, -1.0, p)
    p2 = jnp.max(prest, -1, keepdims=True)
    i2 = jnp.min(jnp.where(prest == p2, lane, big), -1, keepdims=True)
    den = p1 + p2
    w0 = p1 / den * p_g
    w1 = p2 / den * p_g
    e0 = i1 - EXP_LANE0
    e1 = i2 - EXP_LANE0
    is0 = lane == e0
    is1 = lane == e1
    onehot = jnp.where(is0 | is1, 1.0, 0.0)
    r = lax.broadcasted_iota(jnp.int32, (tm, tm), 0)
    c = lax.broadcasted_iota(jnp.int32, (tm, tm), 1)
    tri = jnp.where(r > c, 1.0, 0.0).astype(BF16)
    cum = _dot(tri, onehot.astype(BF16)) + carry[0:1, :]
    rank0 = jnp.sum(jnp.where(is0, cum, 0.0), -1, keepdims=True).astype(jnp.int32)
    rank1 = jnp.sum(jnp.where(is1, cum, 0.0), -1, keepdims=True).astype(jnp.int32)
    carry[...] = carry[...] + jnp.sum(onehot, 0, keepdims=True)
    cnt_ref[...] = carry[...]
    zi = jnp.zeros((tm, LANES), jnp.int32)
    ri_ref[...] = jnp.where(lane == 0, e0, jnp.where(lane == 1, e1, jnp.where(lane == 2, rank0, jnp.where(lane == 3, rank1, zi))))
    rw_ref[...] = jnp.where(lane == 0, w0, jnp.where(lane == 1, w1, 0.0))


def _router(x, wr, br, tm):
    T, D = x.shape
    return pl.pallas_call(
        _router_kernel,
        grid=(T // tm,),
        in_specs=[pl.BlockSpec((tm, D), lambda i: (i, 0)),
                  pl.BlockSpec(wr.shape, lambda i: (0, 0)), pl.BlockSpec(br.shape, lambda i: (0, 0))],
        out_specs=[pl.BlockSpec((tm, LANES), lambda i: (i, 0)), pl.BlockSpec((tm, LANES), lambda i: (i, 0)),
                   pl.BlockSpec((SUBLANES, LANES), lambda i: (0, 0))],
        out_shape=[jax.ShapeDtypeStruct((T, LANES), jnp.int32), jax.ShapeDtypeStruct((T, LANES), F32),
                   jax.ShapeDtypeStruct((SUBLANES, LANES), F32)],
        scratch_shapes=[pltpu.VMEM((SUBLANES, LANES), F32)],
        compiler_params=_cparams("arbitrary"),
    )(x, wr, br)


def _row_copy(src, s, dst, d, sem):
    return pltpu.make_async_copy(src.at[pl.ds(s, 1)], dst.at[pl.ds(d, 1)], sem)


def _dispatch_kernel(dest_ref, x_ref, xs_in, xs_hbm, sem, *, tm):
    del xs_in

    def issue(r, c):
        for k in range(2):
            _row_copy(x_ref, r, xs_hbm, dest_ref[0, 0, 2 * r + k], sem).start(priority=k)
        return c

    lax.fori_loop(0, tm, issue, 0, unroll=DMA_UNROLL)
    pltpu.make_async_copy(xs_hbm.at[pl.ds(0, 2 * tm)], xs_hbm.at[pl.ds(0, 2 * tm)], sem).wait()


def _dispatch(dest3, x, xs0, tm):
    T = x.shape[0]
    return pl.pallas_call(
        functools.partial(_dispatch_kernel, tm=tm),
        grid=(T // tm,),
        in_specs=[pl.BlockSpec((1, 1, 2 * tm), lambda i: (i, 0, 0), memory_space=pltpu.SMEM),
                  pl.BlockSpec((tm, x.shape[1]), lambda i: (i, 0)), pl.BlockSpec(memory_space=pl.ANY)],
        out_specs=pl.BlockSpec(memory_space=pl.ANY),
        out_shape=jax.ShapeDtypeStruct(xs0.shape, xs0.dtype),
        scratch_shapes=[pltpu.SemaphoreType.DMA(())],
        input_output_aliases={2: 0},
        compiler_params=_cparams("arbitrary"),
    )(dest3, x, xs0)


def _expert_kernel(be_ref, nu_ref, xs_ref, wg_ref, wu_ref, wd_ref, ys_ref):
    i = pl.program_id(0)

    @pl.when(i < nu_ref[0])
    def _():
        x = xs_ref[...].astype(BF16)
        a = _dot(x, wg_ref[0].astype(BF16))
        h = a * _sigmoid(a) * _dot(x, wu_ref[0].astype(BF16))
        ys_ref[...] = _dot(h.astype(BF16), wd_ref[0].astype(BF16))

    @pl.when(i >= nu_ref[0])
    def _():
        ys_ref[...] = jnp.zeros(ys_ref.shape, F32)


def _experts(blk_e, nused, xs, wg, wu, wd):
    P, D = xs.shape
    F = wg.shape[2]
    nblk = P // MOE_ROWS
    return pl.pallas_call(
        _expert_kernel,
        grid_spec=pltpu.PrefetchScalarGridSpec(
            num_scalar_prefetch=2, grid=(nblk,),
            in_specs=[pl.BlockSpec((MOE_ROWS, D), lambda i, be, nu: (i, 0)),
                      pl.BlockSpec((1, D, F), lambda i, be, nu: (be[i], 0, 0)),
                      pl.BlockSpec((1, D, F), lambda i, be, nu: (be[i], 0, 0)),
                      pl.BlockSpec((1, F, D), lambda i, be, nu: (be[i], 0, 0))],
            out_specs=pl.BlockSpec((MOE_ROWS, D), lambda i, be, nu: (i, 0))),
        out_shape=jax.ShapeDtypeStruct((P, D), F32),
        compiler_params=_cparams("arbitrary"),
    )(blk_e, nused, xs, wg, wu, wd)


def _combine_kernel(dest_ref, ys_hbm, x_ref, rw_ref, g_ref, b_ref, xo_ref, xb_ref, buf, sem, *, tm):
    def issue(r, c):
        for k in range(2):
            _row_copy(ys_hbm, dest_ref[0, 0, 2 * r + k], buf.at[k], r, sem).start(priority=k)
        return c

    lax.fori_loop(0, tm, issue, 0, unroll=DMA_UNROLL)
    for k in range(2):
        pltpu.make_async_copy(ys_hbm.at[pl.ds(0, tm)], buf.at[k], sem).wait()
    w = rw_ref[...]
    h = DN_ALPHA * x_ref[...] + w[:, 0:1] * buf[0] + w[:, 1:2] * buf[1]
    y = _layer_norm(h, g_ref[...], b_ref[...])
    xo_ref[...] = y
    xb_ref[...] = y.astype(BF16)


def _combine(dest3, ys, x, rw, g, b, tm):
    T, D = x.shape
    return pl.pallas_call(
        functools.partial(_combine_kernel, tm=tm),
        grid=(T // tm,),
        in_specs=[pl.BlockSpec((1, 1, 2 * tm), lambda i: (i, 0, 0), memory_space=pltpu.SMEM),
                  pl.BlockSpec(memory_space=pl.ANY),
                  pl.BlockSpec((tm, D), lambda i: (i, 0)), pl.BlockSpec((tm, LANES), lambda i: (i, 0)),
                  pl.BlockSpec(g.shape, lambda i: (0, 0)), pl.BlockSpec(b.shape, lambda i: (0, 0))],
        out_specs=[pl.BlockSpec((tm, D), lambda i: (i, 0)), pl.BlockSpec((tm, D), lambda i: (i, 0))],
        out_shape=[jax.ShapeDtypeStruct((T, D), F32), jax.ShapeDtypeStruct((T, D), BF16)],
        scratch_shapes=[pltpu.VMEM((2, tm, D), F32), pltpu.SemaphoreType.DMA(())],
        compiler_params=_cparams("arbitrary"),
    )(dest3, ys, x, rw, g, b)


def _prep_in_proj(w_in):
    L, D, _ = w_in.shape
    nw = HEADS * HEAD_DIM
    kvw = 6 * KV_GROUPS * HEAD_DIM
    o = 0
    wq = w_in[..., o:o + nw]; o += nw
    wkv = w_in[..., o:o + kvw]; o += kvw
    wng = w_in[..., o:o + 3 * HEADS]; o += 3 * HEADS
    wpool = w_in[..., o:o + D // 2]; o += D // 2
    wssm = w_in[..., o:o + D // 2]; o += D // 2
    wbg = w_in[..., o:]
    wq = wq.reshape(L, D, KV_GROUPS, GQA, HEAD_DIM) * (HEAD_DIM ** -0.5)
    z = jnp.zeros_like(wq[:, :, 0])
    wq_wide = jnp.stack([jnp.concatenate([wq[:, :, 0], z], -1), jnp.concatenate([z, wq[:, :, 1]], -1)], 2)
    wq_wide = wq_wide.reshape(L, D, HEADS * LANES)
    wng = jnp.pad(wng, ((0, 0), (0, 0), (0, LANES - 3 * HEADS)))
    w_a = jnp.concatenate([wq_wide, wkv, wng, wpool], -1).astype(BF16)
    return w_a, wbg.astype(BF16), wssm.astype(BF16)


def _prep_compress(pe_k, pe_v, w1k, w2k, w1v, w2v):
    L = pe_k.shape[0]
    half = CMP_BLOCK // 2
    cw = 2 * KV_GROUPS * HEAD_DIM
    eye_s = jnp.eye(2, dtype=F32)
    eye_g = jnp.eye(KV_GROUPS, dtype=F32)
    w1r = jnp.stack([w1k, w1v], 1).reshape(L, 2, 2, half, HEAD_DIM, CMP_HIDDEN)
    w1 = jnp.einsum('aspldh,st,gj->alsgdptjh', w1r, eye_s, eye_g)
    per = jnp.stack([pe_k, pe_v], 1).reshape(L, 2, 2, half, HEAD_DIM)
    pe = jnp.broadcast_to(per.transpose(0, 2, 3, 1, 4)[:, :, :, :, None, :], (L, 2, half, 2, KV_GROUPS, HEAD_DIM))
    pe = jnp.pad(pe.reshape(L, 2, half * cw), ((0, 0), (0, SUBLANES - 2), (0, 0)))
    w2 = jnp.einsum('ashd,st,gj->asghtjd', jnp.stack([w2k, w2v], 1), eye_s, eye_g)
    w1 = w1.reshape(L, half * cw, 2 * 2 * KV_GROUPS * CMP_HIDDEN)
    w2 = w2.reshape(L, 2 * KV_GROUPS * CMP_HIDDEN, 2 * KV_GROUPS * HEAD_DIM)
    return w1.astype(BF16), pe.astype(BF16), w2.astype(BF16)


def _overlap(nc):
    i0 = jnp.arange(nc)[:, None] * CMP_STRIDE
    j0 = jnp.arange(LANES)[None, :] * SEL_BLOCK
    return ((i0 < j0 + SEL_BLOCK) & (i0 + CMP_BLOCK > j0)).astype(BF16)


def _prep_ssm(lam_re, lam_im, log_dt, b_re, b_im, c_re, c_im):
    L = lam_re.shape[0]
    dt = jnp.exp(log_dt)[..., None]
    mag = jnp.exp(lam_re * dt)
    ar, ai = mag * jnp.cos(lam_im * dt), mag * jnp.sin(lam_im * dt)
    den = lam_re * lam_re + lam_im * lam_im
    cr = ((ar - 1.0) * lam_re + ai * lam_im) / den
    ci = (ai * lam_re - (ar - 1.0) * lam_im) / den
    bbr = cr[..., None] * b_re - ci[..., None] * b_im
    bbi = cr[..., None] * b_im + ci[..., None] * b_re
    nb = SSM_BLOCKS
    gb = SSM_GROUPS // nb
    half = SSM_BLK_W // 2
    eye = jnp.eye(gb, dtype=F32)
    bt = jnp.concatenate([jnp.einsum('lkgph,gj->lkghjp', m.reshape(L, nb, gb, SSM_STATE, SSM_GROUP), eye)
                          .reshape(L, nb, LANES, half) for m in (bbr, bbi)], -1)
    cm = jnp.concatenate([jnp.einsum('lkghp,gj->lkgpjh', m.reshape(L, nb, gb, SSM_GROUP, SSM_STATE), eye)
                          .reshape(L, nb, half, LANES) for m in (c_re, -c_im)], 2)
    a = jnp.concatenate([ar.reshape(L, nb, half), ai.reshape(L, nb, half)], -1).reshape(L, 1, 2 * SSM_NS)
    a = jnp.broadcast_to(a, (L, SUBLANES, 2 * SSM_NS))
    return bt.astype(BF16), a, cm.astype(BF16)


def _prep_router(w_grp, b_grp, w_exp, b_exp):
    L, D, _ = w_grp.shape
    wr = jnp.zeros((L, D, LANES), F32).at[:, :, 0:MOE_GROUPS].set(w_grp)
    wr = wr.at[:, :, EXP_LANE0:EXP_LANE0 + MOE_EXPERTS].set(w_exp)
    br = jnp.zeros((L, 1, LANES), F32).at[:, 0, 0:MOE_GROUPS].set(b_grp)
    br = br.at[:, 0, EXP_LANE0:EXP_LANE0 + MOE_EXPERTS].set(b_exp)
    hi = wr.astype(BF16)
    lo = (wr - hi.astype(F32)).astype(BF16)
    return jnp.concatenate([hi, lo], -1), br


def _layer(B, S, consts, expert_w, carry, w):
    x, xb = carry
    T, D = x.shape
    nc = S // CMP_STRIDE
    xb_in = xb
    qw, kv, ng, u_pool = _proj(xb, w["w_a"], ((HEADS * LANES, BF16), (6 * LANES, BF16), (LANES, F32), (D // 2, F32)), 512)
    u_ssm = _proj_time_major(xb, w["w_c"], B, S, 512).reshape(S * B, D // 2)
    kr = kv[:, 0:2 * LANES].reshape(T // CMP_STRIDE, CMP_STRIDE * 2 * LANES)
    kvc = _compress(kr, w["cw1"], w["cpe"], w["cw2"], B, nc)
    qaux1, qaux2, kaux, caux, ex = consts
    acc, sel, un = _nsa1(qw, kvc, ng, w["ov"], qaux1, caux, B, S)
    lst, cnt = _chunk_lists(un, B, S)
    o_nsa = _nsa2(lst, cnt, qw, kv, sel, ng, acc, qaux2, kaux, ex, B, S)
    o_pool = _pool(u_pool, w["pool_w"], w["pool_scale"], B, S, 512)
    o_ssm = _ssm(u_ssm, w["bt"], w["a"], w["cm"], w["d"], w["w_glu"], w["b_glu"], 128 * B)
    o_ssm = o_ssm.reshape(S, B * (D // 2))
    x, xb = _merge(o_nsa, o_pool, o_ssm, xb_in, x, w["w_bg"], w["wn"], w["wp"], w["ws"], w["wo"],
                   w["ln1_g"], w["ln1_b"], B, S, 256)
    tm = 256
    ri, rw, cnt = _router(x, w["wr"], w["br"], 512)
    counts = cnt[0, 0:MOE_EXPERTS].astype(jnp.int32)
    padded = (counts + MOE_ROWS - 1) // MOE_ROWS * MOE_ROWS
    pend = jnp.cumsum(padded)
    pstart = pend - padded
    nblk = (2 * T) // MOE_ROWS + MOE_EXPERTS
    eids = jnp.arange(MOE_EXPERTS, dtype=jnp.int32)
    dest = jnp.sum(jnp.where(ri[:, 0:2, None] == eids, pstart, 0), -1) + ri[:, 2:4]
    dest3 = dest.reshape(T // tm, 1, 2 * tm)
    blk_e = jnp.minimum(jnp.sum(pend[None, :] <= jnp.arange(nblk, dtype=jnp.int32)[:, None] * MOE_ROWS, -1),
                        MOE_EXPERTS - 1).astype(jnp.int32)
    nused = (pend[-1:] // MOE_ROWS).astype(jnp.int32)
    xs = _dispatch(dest3, x, jnp.zeros((nblk * MOE_ROWS, D), F32), tm)
    ys = _experts(blk_e + w["layer"] * MOE_EXPERTS, nused, xs, *expert_w)
    x, xb = _combine(dest3, ys, x, rw, w["ln2_g"], w["ln2_b"], tm)
    return (x, xb), None


def kernel(x, w_in, cmp_pe_k, cmp_pe_v, cmp_w1_k, cmp_w2_k, cmp_w1_v, cmp_w2_v, pool_w, pool_scale, ssm_lam_re, ssm_lam_im, ssm_log_dt, ssm_b_re, ssm_b_im, ssm_c_re, ssm_c_im, ssm_d, ssm_w_glu, ssm_b_glu, w_up_nsa, w_up_pool, w_up_ssm, w_out, ln1_g, ln1_b, router_w_grp, router_b_grp, router_w_exp, router_b_exp, moe_w_gate, moe_w_up, moe_w_down, ln2_g, ln2_b):
    B, S, D = x.shape
    L = w_in.shape[0]
    assert B == 4 and D == D_MODEL and S % SEL_CHUNK == 0 and S // SEL_BLOCK <= LANES
    assert Q_TILE == LANES and Q2_TILE % Q_TILE == 0 and SEL_CHUNK % Q2_TILE == 0 and WINDOW % Q2_TILE == 0
    w_a, w_bg, w_c = _prep_in_proj(w_in)
    cw1, cpe, cw2 = _prep_compress(cmp_pe_k, cmp_pe_v, cmp_w1_k, cmp_w2_k, cmp_w1_v, cmp_w2_v)
    bt, a, cm = _prep_ssm(ssm_lam_re, ssm_lam_im, ssm_log_dt, ssm_b_re, ssm_b_im, ssm_c_re, ssm_c_im)
    wr, br = _prep_router(router_w_grp, router_b_grp, router_w_exp, router_b_exp)
    wn = w_up_nsa.reshape(L, KV_GROUPS, GQA, HEAD_DIM, D).transpose(0, 2, 1, 3, 4).reshape(L, HEADS * HEAD_DIM, D)
    row = lambda v: v.reshape(L, 1, -1)
    ws = dict(
        w_a=w_a, w_bg=w_bg, w_c=w_c, cw1=cw1, cpe=cpe, cw2=cw2,
        ov=jnp.broadcast_to(_overlap(S // CMP_STRIDE), (L, S // CMP_STRIDE, LANES)),
        pool_w=pool_w.astype(BF16), pool_scale=row(pool_scale),
        bt=bt, a=a, cm=cm, d=row(ssm_d), w_glu=ssm_w_glu.astype(BF16), b_glu=row(ssm_b_glu),
        wn=wn.astype(BF16), wp=w_up_pool.astype(BF16), ws=w_up_ssm.astype(BF16), wo=w_out.astype(BF16),
        ln1_g=row(ln1_g), ln1_b=row(ln1_b), wr=wr, br=br,
        layer=jnp.arange(L, dtype=jnp.int32),
        ln2_g=row(ln2_g), ln2_b=row(ln2_b))
    xf = x.reshape(B * S, D)
    expert_w = tuple(m.reshape((L * MOE_EXPERTS,) + m.shape[2:]) for m in (moe_w_gate, moe_w_up, moe_w_down))
    (xf, _), _ = lax.scan(functools.partial(_layer, B, S, _nsa_consts(S), expert_w), (xf, xf.astype(BF16)), ws)
    return xf.reshape(B, S, D)
```

```python
import functools
import math

import jax
import jax.numpy as jnp
from jax import lax
from jax.experimental import pallas as pl
from jax.experimental.pallas import tpu as pltpu

F32 = jnp.float32
BF16 = jnp.bfloat16

D_MODEL = 1024
DEPTH = 4
HEAD_DIM = 64
HEADS = 8
KV_GROUPS = 2
GQA = HEADS // KV_GROUPS
CMP_BLOCK = 32
CMP_STRIDE = 16
CMP_HIDDEN = 128
SEL_BLOCK = 64
SEL_TOPN = 16
WINDOW = 512
FORCE_SCORE = 1.0e4
NEG_INF = -1.0e30
POOL_WINDOWS = (2, 4, 8, 16)
POOL_HALO = 16
SSM_GROUP = 16
SSM_GROUPS = 32
SSM_STATE = 64
SSM_NS = SSM_GROUPS * SSM_STATE
MOE_GROUPS = 4
MOE_EPG = 8
MOE_EXPERTS = 32
MOE_FF = 512
DN_ALPHA = (2 * DEPTH) ** 0.25
LN_EPS = 1e-5

LANES = 128
SUBLANES = 8
VMEM_LIMIT = 48 * 1024 * 1024
Q_TILE = 128
Q2_TILE = 128
SEL_CHUNK = 512
MOE_ROWS = 512
EXP_LANE0 = 32
DMA_UNROLL = 4
SEL_SHIFT = SEL_BLOCK.bit_length() - 1
EPG_SHIFT = MOE_EPG.bit_length() - 1


def _cparams(*sem):
    return pltpu.CompilerParams(dimension_semantics=sem, vmem_limit_bytes=VMEM_LIMIT)


def _gelu(x):
    return 0.5 * x * (1.0 + jnp.tanh(math.sqrt(2.0 / math.pi) * (x + 0.044715 * x * x * x)))


def _sigmoid(x):
    return 1.0 / (1.0 + jnp.exp(-x))


def _dot(a, b):
    return jnp.dot(a, b, preferred_element_type=F32)


def _dot_nt(a, b):
    return lax.dot_general(a, b, (((1,), (1,)), ((), ())), preferred_element_type=F32)


def _layer_norm(h, g, b):
    mu = jnp.mean(h, -1, keepdims=True)
    c = h - mu
    var = jnp.mean(c * c, -1, keepdims=True)
    return c * lax.rsqrt(var + LN_EPS) * g + b


def _proj_kernel(x_ref, w_ref, *out_refs, offs):
    x = x_ref[...]
    for o_ref, (a, n) in zip(out_refs, offs):
        o_ref[...] = _dot(x, w_ref[:, a:a + n]).astype(o_ref.dtype)


def _proj(xb, w, outs, tm):
    T, K = xb.shape
    N = w.shape[1]
    offs, a = [], 0
    for n, _ in outs:
        offs.append((a, n))
        a += n
    assert a == N and T % tm == 0
    return pl.pallas_call(
        functools.partial(_proj_kernel, offs=tuple(offs)),
        grid=(T // tm,),
        in_specs=[pl.BlockSpec((tm, K), lambda i: (i, 0)), pl.BlockSpec((K, N), lambda i: (0, 0))],
        out_specs=[pl.BlockSpec((tm, n), lambda i: (i, 0)) for n, _ in outs],
        out_shape=[jax.ShapeDtypeStruct((T, n), dt) for n, dt in outs],
        compiler_params=_cparams("parallel"),
    )(xb, w)


def _proj_tm_kernel(x_ref, w_ref, o_ref):
    o_ref[...] = _dot(x_ref[...], w_ref[...])


def _proj_time_major(xb, w, B, S, tm):
    K = xb.shape[1]
    N = w.shape[1]
    nt = S // tm
    return pl.pallas_call(
        _proj_tm_kernel,
        grid=(B, nt),
        in_specs=[pl.BlockSpec((tm, K), lambda b, i: (b * nt + i, 0)), pl.BlockSpec((K, N), lambda b, i: (0, 0))],
        out_specs=pl.BlockSpec((tm, N), lambda b, i: (i, b)),
        out_shape=jax.ShapeDtypeStruct((S, B * N), F32),
        compiler_params=_cparams("parallel", "parallel"),
    )(xb, w)


def _compress_kernel(kr_ref, w1_ref, pe_ref, w2_ref, out_ref):
    nc = kr_ref.shape[0]
    hw = 4 * CMP_HIDDEN
    a = _dot(kr_ref[...], w1_ref[...])
    pb = _dot(pe_ref[...], w1_ref[...])
    second = pltpu.roll(a[:, hw:], nc - 1, 0)
    row = lax.broadcasted_iota(jnp.int32, (nc, hw), 0)
    second = jnp.where(row < nc - 1, second, 0.0)
    hid = _gelu(a[:, :hw] + second + pb[0:1, :hw] + pb[1:2, hw:])
    out_ref[0] = _dot(hid.astype(BF16), w2_ref[...]).astype(out_ref.dtype)


def _compress(kr, w1, pe_in, w2, B, nc):
    return pl.pallas_call(
        _compress_kernel,
        grid=(B,),
        in_specs=[pl.BlockSpec((nc, kr.shape[1]), lambda b: (b, 0)),
                  pl.BlockSpec(w1.shape, lambda b: (0, 0)),
                  pl.BlockSpec(pe_in.shape, lambda b: (0, 0)),
                  pl.BlockSpec(w2.shape, lambda b: (0, 0))],
        out_specs=pl.BlockSpec((1, nc, 2 * LANES), lambda b: (b, 0, 0)),
        out_shape=jax.ShapeDtypeStruct((B, nc, 2 * LANES), BF16),
        compiler_params=_cparams("parallel"),
    )(kr, w1, pe_in, w2)


def _slope(h):
    return 2.0 ** (-8.0 * (h + 1) / HEADS)


def _stack_heads(q, g):
    return jnp.concatenate([q[:, (g * GQA + r) * LANES:(g * GQA + r + 1) * LANES] for r in range(GQA)], axis=0)


def _split3_dot(a, b):
    hi = a.astype(BF16)
    r1 = a - hi.astype(F32)
    mid = r1.astype(BF16)
    lo = (r1 - mid.astype(F32)).astype(BF16)
    return _dot(hi, b) + _dot(mid, b) + _dot(lo, b)


def _nsa1_kernel(q_ref, kvc_ref, ng_ref, ov_ref, qaux_ref, caux_ref, acc_ref, sel_ref, un_ref, sc_ref, oc_sc):
    Q = q_ref.shape[0]
    nc = kvc_ref.shape[1]
    qb = pl.program_id(1)
    t0 = qb * Q
    lane = lax.broadcasted_iota(jnp.int32, (Q, LANES), 1)
    tb = jnp.right_shift(t0 + lax.broadcasted_iota(jnp.int32, (Q, LANES), 0), SEL_SHIFT)
    jrow = lax.broadcasted_iota(jnp.int32, (LANES, KV_GROUPS * Q), 0).astype(F32)

    def attend(n):
        q = q_ref[...]
        kc = jnp.concatenate([kvc_ref[0, 0:n, 0:LANES], caux_ref[0:n, :]], axis=1)
        vc = kvc_ref[0, 0:n, LANES:2 * LANES]
        qrow = lax.broadcasted_iota(jnp.int32, (Q, n), 0)
        ccol = lax.broadcasted_iota(jnp.int32, (Q, n), 1)
        bias = jnp.where((t0 + qrow) >= (ccol * CMP_STRIDE + (CMP_BLOCK - 1)), 0.0, NEG_INF)
        bias4 = jnp.concatenate([bias] * GQA, axis=0)
        for g in range(KV_GROUPS):
            qs = jnp.concatenate([_stack_heads(q, g), qaux_ref[g]], axis=1)
            s = _dot_nt(qs, kc) + bias4
            m = jnp.max(s, -1, keepdims=True)
            e = jnp.exp(s - m)
            inv = jnp.where(m > 0.5 * NEG_INF, 1.0 / jnp.maximum(jnp.sum(e, -1, keepdims=True), 1e-30), 0.0)
            p = e * inv
            oc_sc[g] = _dot(p.astype(BF16), vc)
            psum = p[0:Q]
            for r in range(1, GQA):
                psum = psum + p[r * Q:(r + 1) * Q]
            imp = _split3_dot(psum, ov_ref[0:n, :])
            forced = (lane == 0) | (lane == tb) | (lane == tb - 1)
            score = jnp.where(forced, FORCE_SCORE, jnp.where(lane <= tb, imp, -1.0))
            sc_ref[:, g * Q:(g + 1) * Q] = score.T

    step = min(LANES, nc)
    need = (t0 + Q - CMP_BLOCK) // CMP_STRIDE + 1
    variant = (need + step - 1) // step
    for v in range(1, nc // step + 1):
        pl.when(variant == v)(functools.partial(attend, v * step))
    gate = _sigmoid(ng_ref[...])
    outs = [[gate[:, 3 * (g * GQA + r):3 * (g * GQA + r) + 1] * oc_sc[g, r * Q:(r + 1) * Q, :] for r in range(GQA)]
            for g in range(KV_GROUPS)]
    work = sc_ref[...]
    taken = jnp.zeros(work.shape, F32)
    for _ in range(SEL_TOPN):
        best = jnp.max(work, 0, keepdims=True)
        first = jnp.min(jnp.where(work == best, jrow, float(LANES)), 0, keepdims=True)
        hit = jrow == first
        taken = jnp.where(hit, 1.0, taken)
        work = jnp.where(hit, -2.0, work)
    for g in range(KV_GROUPS):
        sel = taken[:, g * Q:(g + 1) * Q].T
        sel = jnp.where(lane <= tb, sel, 0.0)
        sel_ref[:, g * LANES:(g + 1) * LANES] = sel.astype(sel_ref.dtype)
        un_ref[:, g * LANES:(g + 1) * LANES] = jnp.broadcast_to(jnp.max(sel, 0, keepdims=True), (SUBLANES, LANES))
    for r in range(GQA):
        acc_ref[:, r * LANES:(r + 1) * LANES] = jnp.where(lane < HEAD_DIM, outs[0][r], outs[1][r])


def _nsa1(qw, kvc, ng, ov, qaux, caux, B, S):
    nq = S // Q_TILE
    nc = kvc.shape[1]
    T = B * S
    rows = lambda b, i: (b * nq + i, 0)
    return pl.pallas_call(
        _nsa1_kernel,
        grid=(B, nq),
        in_specs=[pl.BlockSpec((Q_TILE, HEADS * LANES), rows),
                  pl.BlockSpec((1, nc, 2 * LANES), lambda b, i: (b, 0, 0)),
                  pl.BlockSpec((Q_TILE, LANES), rows),
                  pl.BlockSpec((nc, LANES), lambda b, i: (0, 0)),
                  pl.BlockSpec(qaux.shape, lambda b, i: (0, 0, 0)),
                  pl.BlockSpec((nc, LANES), lambda b, i: (0, 0))],
        out_specs=[pl.BlockSpec((Q_TILE, GQA * LANES), rows),
                   pl.BlockSpec((Q_TILE, KV_GROUPS * LANES), rows),
                   pl.BlockSpec((SUBLANES, KV_GROUPS * LANES), rows)],
        out_shape=[jax.ShapeDtypeStruct((T, GQA * LANES), F32),
                   jax.ShapeDtypeStruct((T, KV_GROUPS * LANES), BF16),
                   jax.ShapeDtypeStruct((B * nq * SUBLANES, KV_GROUPS * LANES), F32)],
        scratch_shapes=[pltpu.VMEM((LANES, KV_GROUPS * Q_TILE), F32),
                        pltpu.VMEM((KV_GROUPS, GQA * Q_TILE, LANES), F32)],
        compiler_params=_cparams("parallel", "parallel"),
    )(qw, kvc, ng, ov, qaux, caux)


def _nsa2_kernel(lst_ref, cnt_ref, q_ref, ks_ref, vs_ref, kw_ref, vw_ref, sel_ref, ng_ref, acc_ref,
                 qaux_ref, kaux_ref, ex_ref, o_ref, m_sc, l_sc, a_sc, ow_sc, s0_sc, s1_sc, sd_sc):
    Q = q_ref.shape[0]
    C = SEL_CHUNK
    WK = WINDOW + Q
    maxch = ex_ref.shape[1] // C
    tile = pl.program_id(0) * pl.num_programs(1) + pl.program_id(1)
    t0 = pl.program_id(1) * Q
    q = q_ref[...]
    gate = _sigmoid(ng_ref[...])
    lane = lax.broadcasted_iota(jnp.int32, (Q, LANES), 1)
    cdiag = t0 // C
    start = pl.multiple_of(jnp.maximum(t0 - WINDOW, 0), Q)
    wrow = lax.broadcasted_iota(jnp.int32, (Q, WK), 0)
    wcol = lax.broadcasted_iota(jnp.int32, (Q, WK), 1)
    wdist = (t0 + wrow) - (start + wcol)
    wbias = jnp.where((wdist >= 0) & (wdist < WINDOW), 0.0, NEG_INF)
    wbias4 = jnp.concatenate([wbias] * GQA, axis=0)
    qrow = lax.broadcasted_iota(jnp.int32, (Q, C), 0)
    kcol = lax.broadcasted_iota(jnp.int32, (Q, C), 1)
    outs = []
    for g in range(KV_GROUPS):
        qs = jnp.concatenate([_stack_heads(q, g), qaux_ref[g]], axis=1)
        selg = sel_ref[:, g * LANES:(g + 1) * LANES]
        m_sc[...] = jnp.full(m_sc.shape, NEG_INF, F32)
        l_sc[...] = jnp.zeros(l_sc.shape, F32)
        a_sc[...] = jnp.zeros(a_sc.shape, F32)

        def scores(c, s_out, qs=qs, selg=selg):
            k0 = pl.multiple_of(c * C, C)
            kk = jnp.concatenate([ks_ref[pl.ds(k0, C), :], kaux_ref[pl.ds(k0, C), :]], axis=1)
            valid = (_dot(selg, ex_ref[:, pl.ds(k0, C)]) > 0.5) & ((t0 + qrow) >= (k0 + kcol))
            bias = jnp.where(valid, 0.0, NEG_INF)
            s_out[...] = _dot_nt(qs, kk) + jnp.concatenate([bias] * GQA, axis=0)

        def accumulate(c, s_in):
            k0 = pl.multiple_of(c * C, C)
            s = s_in[...]
            m_old = m_sc[...]
            m_new = jnp.maximum(m_old, jnp.max(s, -1, keepdims=True))
            alpha = jnp.exp(m_old - m_new)
            e = jnp.exp(s - m_new[:, 0:1])
            l_sc[...] = alpha * l_sc[...] + jnp.sum(e, -1, keepdims=True)
            a_sc[...] = alpha * a_sc[...] + _dot(e.astype(BF16), vs_ref[pl.ds(k0, C), :])
            m_sc[...] = m_new

        base = (tile * KV_GROUPS + g) * maxch
        scores(cdiag, sd_sc)
        scores(lst_ref[base], s0_sc)
        kw = jnp.concatenate([kw_ref[pl.ds(start, WK), :], kaux_ref[pl.ds(start, WK), :]], axis=1)
        sw = _dot_nt(qs, kw) + wbias4
        e = jnp.exp(sw - jnp.max(sw, -1, keepdims=True))
        ow_sc[...] = _dot(e.astype(BF16), vw_ref[pl.ds(start, WK), :]) / jnp.maximum(jnp.sum(e, -1, keepdims=True), 1e-30)

        def step(j, carry, scores=scores, accumulate=accumulate, base=base):
            cur = lst_ref[base + j]
            nxt = lst_ref[base + j + 1]

            @pl.when(j % 2 == 0)
            def _():
                scores(nxt, s1_sc)
                accumulate(cur, s0_sc)

            @pl.when(j % 2 == 1)
            def _():
                scores(nxt, s0_sc)
                accumulate(cur, s1_sc)

            return carry

        lax.fori_loop(0, cnt_ref[tile * KV_GROUPS + g], step, 0)
        accumulate(cdiag, sd_sc)
        o_w = ow_sc[...]
        o_s = a_sc[...] / jnp.maximum(l_sc[...], 1e-30)
        og = []
        for r in range(GQA):
            h = g * GQA + r
            rows = slice(r * Q, (r + 1) * Q)
            og.append(gate[:, 3 * h + 1:3 * h + 2] * o_s[rows] + gate[:, 3 * h + 2:3 * h + 3] * o_w[rows])
        outs.append(og)
    for r in range(GQA):
        cols = slice(r * LANES, (r + 1) * LANES)
        o_ref[:, cols] = (acc_ref[:, cols] + jnp.where(lane < HEAD_DIM, outs[0][r], outs[1][r])).astype(o_ref.dtype)


def _chunk_lists(un, B, S):
    nq = S // Q2_TILE
    maxch = S // SEL_CHUNK
    per = SEL_CHUNK // SEL_BLOCK
    un = un.reshape(B * nq, Q2_TILE // Q_TILE, SUBLANES, KV_GROUPS, LANES)[:, :, 0, :, :maxch * per].max(1)
    flags = un.reshape(B * nq, KV_GROUPS, maxch, per).max(-1) > 0.5
    cdiag = (jnp.arange(B * nq, dtype=jnp.int32) % nq) * Q2_TILE // SEL_CHUNK
    flags = flags & (jnp.arange(maxch, dtype=jnp.int32)[None, None, :] < cdiag[:, None, None])
    pos = jnp.cumsum(flags.astype(jnp.int32), -1)
    hit = flags[..., None, :] & (pos[..., None, :] == jnp.arange(1, maxch + 1, dtype=jnp.int32)[:, None])
    lst = jnp.sum(jnp.where(hit, jnp.arange(maxch, dtype=jnp.int32), 0), -1)
    return lst.astype(jnp.int32).reshape(-1), pos[..., -1].reshape(-1)


def _nsa2(lst, cnt, qw, kv, sel, ng, acc, qaux, kaux, ex, B, S):
    Q = Q2_TILE
    nq = S // Q
    T = B * S
    rows = lambda b, i, *_: (b * nq + i, 0)
    kvspec = lambda col: pl.BlockSpec((S, LANES), lambda b, i, *_, col=col: (b, col))
    const = lambda shp: pl.BlockSpec(shp, lambda b, i, *_: tuple(0 for _ in shp))
    return pl.pallas_call(
        _nsa2_kernel,
        grid_spec=pltpu.PrefetchScalarGridSpec(
            num_scalar_prefetch=2, grid=(B, nq),
            in_specs=[pl.BlockSpec((Q, HEADS * LANES), rows),
                      kvspec(2), kvspec(3), kvspec(4), kvspec(5),
                      pl.BlockSpec((Q, KV_GROUPS * LANES), rows),
                      pl.BlockSpec((Q, LANES), rows),
                      pl.BlockSpec((Q, GQA * LANES), rows),
                      const(qaux.shape), const(kaux.shape), const(ex.shape)],
            out_specs=pl.BlockSpec((Q, GQA * LANES), rows),
            scratch_shapes=[pltpu.VMEM((GQA * Q, LANES), F32)] * 4
            + [pltpu.VMEM((GQA * Q, SEL_CHUNK), F32)] * 3),
        out_shape=jax.ShapeDtypeStruct((T, GQA * LANES), BF16),
        compiler_params=_cparams("parallel", "parallel"),
    )(lst, cnt, qw, kv, kv, kv, kv, sel, ng, acc, qaux, kaux, ex)


def _nsa_consts(S):
    h = jnp.arange(HEADS, dtype=F32).reshape(KV_GROUPS, GQA, 1, 1)
    slope = jnp.exp2(-8.0 * (h + 1.0) / HEADS)
    lane = jnp.arange(LANES)[None, None, None, :]
    qaux = jnp.where(lane == 0, slope * SEL_BLOCK, jnp.where(lane == 1, slope, 0.0))
    qaux1, qaux2 = (jnp.broadcast_to(qaux, (KV_GROUPS, GQA, n, LANES)).reshape(KV_GROUPS, GQA * n, LANES).astype(BF16)
                    for n in (Q_TILE, Q2_TILE))

    def pos_cols(pos):
        l2 = jnp.arange(LANES)[None, :]
        return jnp.where(l2 == 0, pos[:, None] // SEL_BLOCK, jnp.where(l2 == 1, pos[:, None] % SEL_BLOCK, 0))

    kaux = pos_cols(jnp.arange(S))
    caux = pos_cols(jnp.arange(S // CMP_STRIDE) * CMP_STRIDE + (CMP_BLOCK - 1))
    ex = (jnp.arange(LANES)[:, None] == (jnp.arange(S)[None, :] // SEL_BLOCK))
    return qaux1, qaux2, kaux.astype(BF16), caux.astype(BF16), ex.astype(BF16)


def _pool_kernel(u_ref, halo_ref, w_ref, sc_ref, o_ref, x_sc):
    ts = u_ref.shape[0]
    i = pl.program_id(1)
    x_sc[0:POOL_HALO] = jnp.where(i > 0, halo_ref[...], 0.0)
    x_sc[POOL_HALO:POOL_HALO + ts] = u_ref[...]
    t = i * ts + lax.broadcasted_iota(jnp.int32, (ts, LANES), 0)
    for g, w in enumerate(POOL_WINDOWS):
        cols = slice(g * LANES, (g + 1) * LANES)
        cur = x_sc[POOL_HALO:POOL_HALO + ts, cols]
        acc = cur
        for k in range(1, w):
            acc = acc + x_sc[POOL_HALO - k:POOL_HALO - k + ts, cols]
        cnt = jnp.minimum(t + 1, w).astype(F32)
        pooled = acc / cnt - cur
        o_ref[:, cols] = (_dot(pooled.astype(BF16), w_ref[g]) * sc_ref[:, cols]).astype(o_ref.dtype)


def _pool(u, pool_w, pool_scale, B, S, ts):
    nt = S // ts
    hb = ts // POOL_HALO
    C = u.shape[1]
    return pl.pallas_call(
        _pool_kernel,
        grid=(B, nt),
        in_specs=[pl.BlockSpec((ts, C), lambda b, i: (b * nt + i, 0)),
                  pl.BlockSpec((POOL_HALO, C), lambda b, i: (jnp.maximum((b * nt + i) * hb - 1, 0), 0)),
                  pl.BlockSpec(pool_w.shape, lambda b, i: (0, 0, 0)),
                  pl.BlockSpec((1, C), lambda b, i: (0, 0))],
        out_specs=pl.BlockSpec((ts, C), lambda b, i: (b * nt + i, 0)),
        out_shape=jax.ShapeDtypeStruct((B * S, C), BF16),
        scratch_shapes=[pltpu.VMEM((POOL_HALO + ts, C), F32)],
        compiler_params=_cparams("parallel", "parallel"),
    )(u, u, pool_w, pool_scale)


SSM_BATCH = 4
SSM_PASS = 8
SSM_BLOCKS = SSM_GROUPS * SSM_GROUP // LANES
SSM_BLK_W = 2 * SSM_NS // SSM_BLOCKS


def _ssm_kernel(u_ref, bt_ref, a_ref, c_ref, d_ref, wg_ref, bg_ref, o_ref, xh, hst, io_sc):
    steps = u_ref.shape[0]
    C = u_ref.shape[1] // SSM_BATCH
    rows = steps * SSM_BATCH

    @pl.when(pl.program_id(0) == 0)
    def _():
        hst[...] = jnp.zeros(hst.shape, F32)

    nslab = C // LANES
    for b in range(SSM_BATCH):
        for j in range(nslab):
            io_sc[j, pl.ds(b, steps, stride=SSM_BATCH), :] = u_ref[:, b * C + j * LANES:b * C + (j + 1) * LANES]
    u = jnp.concatenate([io_sc[j] for j in range(nslab)], axis=1)
    ub = u.astype(BF16)
    for k in range(SSM_BLOCKS):
        xh[:, k * SSM_BLK_W:(k + 1) * SSM_BLK_W] = _dot(ub[:, k * LANES:(k + 1) * LANES], bt_ref[k])
    low = lax.broadcasted_iota(jnp.int32, (SUBLANES, LANES), 0) < 4
    nchunk = SSM_NS // LANES
    per_blk = nchunk // SSM_BLOCKS
    for p in range(nchunk // SSM_PASS):
        cols = [p * SSM_PASS + j for j in range(SSM_PASS)]
        start = [(c // per_blk) * SSM_BLK_W + (c % per_blk) * LANES for c in cols]
        re = [slice(s, s + LANES) for s in start]
        im = [slice(s + SSM_BLK_W // 2, s + SSM_BLK_W // 2 + LANES) for s in start]
        ar = [a_ref[:, s] for s in re]
        ai = [a_ref[:, s] for s in im]

        def tile(k, carry, re=re, im=im, ar=ar, ai=ai):
            cr, ci = carry
            r0 = pl.multiple_of(k * SUBLANES, SUBLANES)
            ncr, nci = [], []
            for j in range(SSM_PASS):
                vr = xh[pl.ds(r0, SUBLANES), re[j]]
                vi = xh[pl.ds(r0, SUBLANES), im[j]]
                t1r = ar[j] * cr[j] - ai[j] * ci[j] + vr
                t1i = ar[j] * ci[j] + ai[j] * cr[j] + vi
                pr = pltpu.roll(t1r, 4, 0)
                pi = pltpu.roll(t1i, 4, 0)
                t2r = ar[j] * pr - ai[j] * pi + vr
                t2i = ar[j] * pi + ai[j] * pr + vi
                xh[pl.ds(r0, SUBLANES), re[j]] = jnp.where(low, t1r, t2r)
                xh[pl.ds(r0, SUBLANES), im[j]] = jnp.where(low, t1i, t2i)
                ncr.append(pltpu.roll(t2r, 4, 0))
                nci.append(pltpu.roll(t2i, 4, 0))
            return tuple(ncr), tuple(nci)

        cr0 = tuple(hst[:, s] for s in re)
        ci0 = tuple(hst[:, s] for s in im)
        cr, ci = lax.fori_loop(0, rows // SUBLANES, tile, (cr0, ci0))
        for j in range(SSM_PASS):
            hst[:, re[j]] = cr[j]
            hst[:, im[j]] = ci[j]
    y = jnp.concatenate([_dot(xh[:, k * SSM_BLK_W:(k + 1) * SSM_BLK_W].astype(BF16), c_ref[k])
                         for k in range(SSM_BLOCKS)], axis=1) + d_ref[...] * u
    z = _gelu(y)
    out = z * _sigmoid(_dot(z.astype(BF16), wg_ref[...]) + bg_ref[...])
    for j in range(nslab):
        io_sc[j] = out[:, j * LANES:(j + 1) * LANES]
    for b in range(SSM_BATCH):
        for j in range(nslab):
            o_ref[:, b * C + j * LANES:b * C + (j + 1) * LANES] = (
                io_sc[j, pl.ds(b, steps, stride=SSM_BATCH), :].astype(o_ref.dtype))


def _ssm(u, bt, a_b, cmat, d_skip, w_glu, b_glu, steps):
    S, W = u.shape
    rows = steps * SSM_BATCH
    const = lambda shp: pl.BlockSpec(shp, lambda i: tuple(0 for _ in shp))
    return pl.pallas_call(
        _ssm_kernel,
        grid=(S // steps,),
        in_specs=[pl.BlockSpec((steps, W), lambda i: (i, 0)),
                  const(bt.shape), const(a_b.shape), const(cmat.shape), const(d_skip.shape),
                  const(w_glu.shape), const(b_glu.shape)],
        out_specs=pl.BlockSpec((steps, W), lambda i: (i, 0)),
        out_shape=jax.ShapeDtypeStruct((S, W), BF16),
        scratch_shapes=[pltpu.VMEM((rows, 2 * SSM_NS), F32), pltpu.VMEM((SUBLANES, 2 * SSM_NS), F32),
                        pltpu.VMEM((W // SSM_BATCH // LANES, rows, LANES), F32)],
        compiler_params=_cparams("arbitrary"),
    )(u, bt, a_b, cmat, d_skip, w_glu, b_glu)


def _merge_kernel(on_ref, op_ref, os_ref, xin_ref, x_ref, wbg_ref, wn_ref, wp_ref, ws_ref, wo_ref, g_ref, b_ref,
                  xo_ref, xb_ref):
    D = x_ref.shape[1]
    xin = xin_ref[...]
    m = None
    for k, (o_ref, w_ref) in enumerate(((on_ref, wn_ref), (op_ref, wp_ref), (os_ref, ws_ref))):
        gate = _sigmoid(_dot(xin, wbg_ref[:, k * D:(k + 1) * D]))
        term = gate * _dot(o_ref[...], w_ref[...])
        m = term if m is None else m + term
    h = DN_ALPHA * x_ref[...] + _dot(m.astype(BF16), wo_ref[...])
    y = _layer_norm(h, g_ref[...], b_ref[...])
    xo_ref[...] = y
    xb_ref[...] = y.astype(BF16)


def _merge(o_nsa, o_pool, o_ssm_tm, xb, x, wbg, wn, wp, ws, wo, g, b, B, S, tm):
    nt = S // tm
    T, D = x.shape
    W = o_nsa.shape[1]
    rows = lambda bb, i: (bb * nt + i, 0)
    const = lambda shp: pl.BlockSpec(shp, lambda bb, i: tuple(0 for _ in shp))
    return pl.pallas_call(
        _merge_kernel,
        grid=(B, nt),
        in_specs=[pl.BlockSpec((tm, W), rows), pl.BlockSpec((tm, W), rows),
                  pl.BlockSpec((tm, W), lambda bb, i: (i, bb)),
                  pl.BlockSpec((tm, D), rows), pl.BlockSpec((tm, D), rows),
                  const(wbg.shape), const(wn.shape), const(wp.shape), const(ws.shape), const(wo.shape),
                  const(g.shape), const(b.shape)],
        out_specs=[pl.BlockSpec((tm, D), rows), pl.BlockSpec((tm, D), rows)],
        out_shape=[jax.ShapeDtypeStruct((T, D), F32), jax.ShapeDtypeStruct((T, D), BF16)],
        compiler_params=_cparams("parallel", "parallel"),
    )(o_nsa, o_pool, o_ssm_tm, xb, x, wbg, wn, wp, ws, wo, g, b)


def _router_kernel(x_ref, wr_ref, br_ref, ri_ref, rw_ref, cnt_ref, carry):
    tm = x_ref.shape[0]

    @pl.when(pl.program_id(0) == 0)
    def _():
        carry[...] = jnp.zeros(carry.shape, F32)

    x = x_ref[...]
    xh = x.astype(BF16)
    xl = (x - xh.astype(F32)).astype(BF16)
    both = _dot(xh, wr_ref[...])
    logits = both[:, 0:LANES] + both[:, LANES:2 * LANES] + _dot(xl, wr_ref[:, 0:LANES]) + br_ref[...]
    lane = lax.broadcasted_iota(jnp.int32, (tm, LANES), 1)
    big = jnp.int32(LANES)
    isg = lane < MOE_GROUPS
    lg = jnp.where(isg, logits, NEG_INF)
    eg = jnp.where(isg, jnp.exp(lg - jnp.max(lg, -1, keepdims=True)), 0.0)
    pgv = jnp.where(isg, eg / jnp.sum(eg, -1, keepdims=True), -1.0)
    p_g = jnp.max(pgv, -1, keepdims=True)
    g_sel = jnp.min(jnp.where(pgv == p_g, lane, big), -1, keepdims=True)
    ine = ((lane >= EXP_LANE0) & (lane < EXP_LANE0 + MOE_EXPERTS)
           & (jnp.right_shift(lane - EXP_LANE0, EPG_SHIFT) == g_sel))
    le = jnp.where(ine, logits, NEG_INF)
    ee = jnp.where(ine, jnp.exp(le - jnp.max(le, -1, keepdims=True)), 0.0)
    p = jnp.where(ine, ee / jnp.sum(ee, -1, keepdims=True), -1.0)
    p1 = jnp.max(p, -1, keepdims=True)
    i1 = jnp.min(jnp.where(p == p1, lane, big), -1, keepdims=True)
    prest = jnp.where(lane == i1, -1.0, p)
    p2 = jnp.max(prest, -1, keepdims=True)
    i2 = jnp.min(jnp.where(prest == p2, lane, big), -1, keepdims=True)
    den = p1 + p2
    w0 = p1 / den * p_g
    w1 = p2 / den * p_g
    e0 = i1 - EXP_LANE0
    e1 = i2 - EXP_LANE0
    is0 = lane == e0
    is1 = lane == e1
    onehot = jnp.where(is0 | is1, 1.0, 0.0)
    r = lax.broadcasted_iota(jnp.int32, (tm, tm), 0)
    c = lax.broadcasted_iota(jnp.int32, (tm, tm), 1)
    tri = jnp.where(r > c, 1.0, 0.0).astype(BF16)
    cum = _dot(tri, onehot.astype(BF16)) + carry[0:1, :]
    rank0 = jnp.sum(jnp.where(is0, cum, 0.0), -1, keepdims=True).astype(jnp.int32)
    rank1 = jnp.sum(jnp.where(is1, cum, 0.0), -1, keepdims=True).astype(jnp.int32)
    carry[...] = carry[...] + jnp.sum(onehot, 0, keepdims=True)
    cnt_ref[...] = carry[...]
    zi = jnp.zeros((tm, LANES), jnp.int32)
    ri_ref[...] = jnp.where(lane == 0, e0, jnp.where(lane == 1, e1, jnp.where(lane == 2, rank0, jnp.where(lane == 3, rank1, zi))))
    rw_ref[...] = jnp.where(lane == 0, w0, jnp.where(lane == 1, w1, 0.0))


def _router(x, wr, br, tm):
    T, D = x.shape
    return pl.pallas_call(
        _router_kernel,
        grid=(T // tm,),
        in_specs=[pl.BlockSpec((tm, D), lambda i: (i, 0)),
                  pl.BlockSpec(wr.shape, lambda i: (0, 0)), pl.BlockSpec(br.shape, lambda i: (0, 0))],
        out_specs=[pl.BlockSpec((tm, LANES), lambda i: (i, 0)), pl.BlockSpec((tm, LANES), lambda i: (i, 0)),
                   pl.BlockSpec((SUBLANES, LANES), lambda i: (0, 0))],
        out_shape=[jax.ShapeDtypeStruct((T, LANES), jnp.int32), jax.ShapeDtypeStruct((T, LANES), F32),
                   jax.ShapeDtypeStruct((SUBLANES, LANES), F32)],
        scratch_shapes=[pltpu.VMEM((SUBLANES, LANES), F32)],
        compiler_params=_cparams("arbitrary"),
    )(x, wr, br)


def _row_copy(src, s, dst, d, sem):
    return pltpu.make_async_copy(src.at[pl.ds(s, 1)], dst.at[pl.ds(d, 1)], sem)


def _dispatch_kernel(dest_ref, x_ref, xs_in, xs_hbm, sem, *, tm):
    del xs_in

    def issue(r, c):
        for k in range(2):
            _row_copy(x_ref, r, xs_hbm, dest_ref[0, 0, 2 * r + k], sem).start(priority=k)
        return c

    lax.fori_loop(0, tm, issue, 0, unroll=DMA_UNROLL)
    pltpu.make_async_copy(xs_hbm.at[pl.ds(0, 2 * tm)], xs_hbm.at[pl.ds(0, 2 * tm)], sem).wait()


def _dispatch(dest3, x, xs0, tm):
    T = x.shape[0]
    return pl.pallas_call(
        functools.partial(_dispatch_kernel, tm=tm),
        grid=(T // tm,),
        in_specs=[pl.BlockSpec((1, 1, 2 * tm), lambda i: (i, 0, 0), memory_space=pltpu.SMEM),
                  pl.BlockSpec((tm, x.shape[1]), lambda i: (i, 0)), pl.BlockSpec(memory_space=pl.ANY)],
        out_specs=pl.BlockSpec(memory_space=pl.ANY),
        out_shape=jax.ShapeDtypeStruct(xs0.shape, xs0.dtype),
        scratch_shapes=[pltpu.SemaphoreType.DMA(())],
        input_output_aliases={2: 0},
        compiler_params=_cparams("arbitrary"),
    )(dest3, x, xs0)


def _expert_kernel(be_ref, nu_ref, xs_ref, wg_ref, wu_ref, wd_ref, ys_ref):
    i = pl.program_id(0)

    @pl.when(i < nu_ref[0])
    def _():
        x = xs_ref[...].astype(BF16)
        a = _dot(x, wg_ref[0].astype(BF16))
        h = a * _sigmoid(a) * _dot(x, wu_ref[0].astype(BF16))
        ys_ref[...] = _dot(h.astype(BF16), wd_ref[0].astype(BF16))

    @pl.when(i >= nu_ref[0])
    def _():
        ys_ref[...] = jnp.zeros(ys_ref.shape, F32)


def _experts(blk_e, nused, xs, wg, wu, wd):
    P, D = xs.shape
    F = wg.shape[2]
    nblk = P // MOE_ROWS
    return pl.pallas_call(
        _expert_kernel,
        grid_spec=pltpu.PrefetchScalarGridSpec(
            num_scalar_prefetch=2, grid=(nblk,),
            in_specs=[pl.BlockSpec((MOE_ROWS, D), lambda i, be, nu: (i, 0)),
                      pl.BlockSpec((1, D, F), lambda i, be, nu: (be[i], 0, 0)),
                      pl.BlockSpec((1, D, F), lambda i, be, nu: (be[i], 0, 0)),
                      pl.BlockSpec((1, F, D), lambda i, be, nu: (be[i], 0, 0))],
            out_specs=pl.BlockSpec((MOE_ROWS, D), lambda i, be, nu: (i, 0))),
        out_shape=jax.ShapeDtypeStruct((P, D), F32),
        compiler_params=_cparams("arbitrary"),
    )(blk_e, nused, xs, wg, wu, wd)


def _combine_kernel(dest_ref, ys_hbm, x_ref, rw_ref, g_ref, b_ref, xo_ref, xb_ref, buf, sem, *, tm):
    def issue(r, c):
        for k in range(2):
            _row_copy(ys_hbm, dest_ref[0, 0, 2 * r + k], buf.at[k], r, sem).start(priority=k)
        return c

    lax.fori_loop(0, tm, issue, 0, unroll=DMA_UNROLL)
    for k in range(2):
        pltpu.make_async_copy(ys_hbm.at[pl.ds(0, tm)], buf.at[k], sem).wait()
    w = rw_ref[...]
    h = DN_ALPHA * x_ref[...] + w[:, 0:1] * buf[0] + w[:, 1:2] * buf[1]
    y = _layer_norm(h, g_ref[...], b_ref[...])
    xo_ref[...] = y
    xb_ref[...] = y.astype(BF16)


def _combine(dest3, ys, x, rw, g, b, tm):
    T, D = x.shape
    return pl.pallas_call(
        functools.partial(_combine_kernel, tm=tm),
        grid=(T // tm,),
        in_specs=[pl.BlockSpec((1, 1, 2 * tm), lambda i: (i, 0, 0), memory_space=pltpu.SMEM),
                  pl.BlockSpec(memory_space=pl.ANY),
                  pl.BlockSpec((tm, D), lambda i: (i, 0)), pl.BlockSpec((tm, LANES), lambda i: (i, 0)),
                  pl.BlockSpec(g.shape, lambda i: (0, 0)), pl.BlockSpec(b.shape, lambda i: (0, 0))],
        out_specs=[pl.BlockSpec((tm, D), lambda i: (i, 0)), pl.BlockSpec((tm, D), lambda i: (i, 0))],
        out_shape=[jax.ShapeDtypeStruct((T, D), F32), jax.ShapeDtypeStruct((T, D), BF16)],
        scratch_shapes=[pltpu.VMEM((2, tm, D), F32), pltpu.SemaphoreType.DMA(())],
        compiler_params=_cparams("arbitrary"),
    )(dest3, ys, x, rw, g, b)


def _prep_in_proj(w_in):
    L, D, _ = w_in.shape
    nw = HEADS * HEAD_DIM
    kvw = 6 * KV_GROUPS * HEAD_DIM
    o = 0
    wq = w_in[..., o:o + nw]; o += nw
    wkv = w_in[..., o:o + kvw]; o += kvw
    wng = w_in[..., o:o + 3 * HEADS]; o += 3 * HEADS
    wpool = w_in[..., o:o + D // 2]; o += D // 2
    wssm = w_in[..., o:o + D // 2]; o += D // 2
    wbg = w_in[..., o:]
    wq = wq.reshape(L, D, KV_GROUPS, GQA, HEAD_DIM) * (HEAD_DIM ** -0.5)
    z = jnp.zeros_like(wq[:, :, 0])
    wq_wide = jnp.stack([jnp.concatenate([wq[:, :, 0], z], -1), jnp.concatenate([z, wq[:, :, 1]], -1)], 2)
    wq_wide = wq_wide.reshape(L, D, HEADS * LANES)
    wng = jnp.pad(wng, ((0, 0), (0, 0), (0, LANES - 3 * HEADS)))
    w_a = jnp.concatenate([wq_wide, wkv, wng, wpool], -1).astype(BF16)
    return w_a, wbg.astype(BF16), wssm.astype(BF16)


def _prep_compress(pe_k, pe_v, w1k, w2k, w1v, w2v):
    L = pe_k.shape[0]
    half = CMP_BLOCK // 2
    cw = 2 * KV_GROUPS * HEAD_DIM
    eye_s = jnp.eye(2, dtype=F32)
    eye_g = jnp.eye(KV_GROUPS, dtype=F32)
    w1r = jnp.stack([w1k, w1v], 1).reshape(L, 2, 2, half, HEAD_DIM, CMP_HIDDEN)
    w1 = jnp.einsum('aspldh,st,gj->alsgdptjh', w1r, eye_s, eye_g)
    per = jnp.stack([pe_k, pe_v], 1).reshape(L, 2, 2, half, HEAD_DIM)
    pe = jnp.broadcast_to(per.transpose(0, 2, 3, 1, 4)[:, :, :, :, None, :], (L, 2, half, 2, KV_GROUPS, HEAD_DIM))
    pe = jnp.pad(pe.reshape(L, 2, half * cw), ((0, 0), (0, SUBLANES - 2), (0, 0)))
    w2 = jnp.einsum('ashd,st,gj->asghtjd', jnp.stack([w2k, w2v], 1), eye_s, eye_g)
    w1 = w1.reshape(L, half * cw, 2 * 2 * KV_GROUPS * CMP_HIDDEN)
    w2 = w2.reshape(L, 2 * KV_GROUPS * CMP_HIDDEN, 2 * KV_GROUPS * HEAD_DIM)
    return w1.astype(BF16), pe.astype(BF16), w2.astype(BF16)


def _overlap(nc):
    i0 = jnp.arange(nc)[:, None] * CMP_STRIDE
    j0 = jnp.arange(LANES)[None, :] * SEL_BLOCK
    return ((i0 < j0 + SEL_BLOCK) & (i0 + CMP_BLOCK > j0)).astype(BF16)


def _prep_ssm(lam_re, lam_im, log_dt, b_re, b_im, c_re, c_im):
    L = lam_re.shape[0]
    dt = jnp.exp(log_dt)[..., None]
    mag = jnp.exp(lam_re * dt)
    ar, ai = mag * jnp.cos(lam_im * dt), mag * jnp.sin(lam_im * dt)
    den = lam_re * lam_re + lam_im * lam_im
    cr = ((ar - 1.0) * lam_re + ai * lam_im) / den
    ci = (ai * lam_re - (ar - 1.0) * lam_im) / den
    bbr = cr[..., None] * b_re - ci[..., None] * b_im
    bbi = cr[..., None] * b_im + ci[..., None] * b_re
    nb = SSM_BLOCKS
    gb = SSM_GROUPS // nb
    half = SSM_BLK_W // 2
    eye = jnp.eye(gb, dtype=F32)
    bt = jnp.concatenate([jnp.einsum('lkgph,gj->lkghjp', m.reshape(L, nb, gb, SSM_STATE, SSM_GROUP), eye)
                          .reshape(L, nb, LANES, half) for m in (bbr, bbi)], -1)
    cm = jnp.concatenate([jnp.einsum('lkghp,gj->lkgpjh', m.reshape(L, nb, gb, SSM_GROUP, SSM_STATE), eye)
                          .reshape(L, nb, half, LANES) for m in (c_re, -c_im)], 2)
    a = jnp.concatenate([ar.reshape(L, nb, half), ai.reshape(L, nb, half)], -1).reshape(L, 1, 2 * SSM_NS)
    a = jnp.broadcast_to(a, (L, SUBLANES, 2 * SSM_NS))
    return bt.astype(BF16), a, cm.astype(BF16)


def _prep_router(w_grp, b_grp, w_exp, b_exp):
    L, D, _ = w_grp.shape
    wr = jnp.zeros((L, D, LANES), F32).at[:, :, 0:MOE_GROUPS].set(w_grp)
    wr = wr.at[:, :, EXP_LANE0:EXP_LANE0 + MOE_EXPERTS].set(w_exp)
    br = jnp.zeros((L, 1, LANES), F32).at[:, 0, 0:MOE_GROUPS].set(b_grp)
    br = br.at[:, 0, EXP_LANE0:EXP_LANE0 + MOE_EXPERTS].set(b_exp)
    hi = wr.astype(BF16)
    lo = (wr - hi.astype(F32)).astype(BF16)
    return jnp.concatenate([hi, lo], -1), br


def _layer(B, S, consts, expert_w, carry, w):
    x, xb, xs = carry
    T, D = x.shape
    nc = S // CMP_STRIDE
    xb_in = xb
    qw, kv, ng, u_pool = _proj(xb, w["w_a"], ((HEADS * LANES, BF16), (6 * LANES, BF16), (LANES, F32), (D // 2, F32)), 512)
    u_ssm = _proj_time_major(xb, w["w_c"], B, S, 512)
    kr = kv[:, 0:2 * LANES].reshape(T // CMP_STRIDE, CMP_STRIDE * 2 * LANES)
    kvc = _compress(kr, w["cw1"], w["cpe"], w["cw2"], B, nc)
    qaux1, qaux2, kaux, caux, ex = consts
    acc, sel, un = _nsa1(qw, kvc, ng, w["ov"], qaux1, caux, B, S)
    lst, cnt = _chunk_lists(un, B, S)
    o_nsa = _nsa2(lst, cnt, qw, kv, sel, ng, acc, qaux2, kaux, ex, B, S)
    o_pool = _pool(u_pool, w["pool_w"], w["pool_scale"], B, S, 512)
    o_ssm = _ssm(u_ssm, w["bt"], w["a"], w["cm"], w["d"], w["w_glu"], w["b_glu"], 128)
    x, xb = _merge(o_nsa, o_pool, o_ssm, xb_in, x, w["w_bg"], w["wn"], w["wp"], w["ws"], w["wo"],
                   w["ln1_g"], w["ln1_b"], B, S, 256)
    tm = 256
    ri, rw, cnt = _router(x, w["wr"], w["br"], 512)
    counts = cnt[0, 0:MOE_EXPERTS].astype(jnp.int32)
    padded = (counts + MOE_ROWS - 1) // MOE_ROWS * MOE_ROWS
    pend = jnp.cumsum(padded)
    pstart = pend - padded
    nblk = (2 * T) // MOE_ROWS + MOE_EXPERTS
    eids = jnp.arange(MOE_EXPERTS, dtype=jnp.int32)
    dest = jnp.sum(jnp.where(ri[:, 0:2, None] == eids, pstart, 0), -1) + ri[:, 2:4]
    dest3 = dest.reshape(T // tm, 1, 2 * tm)
    blk_e = jnp.minimum(jnp.sum(pend[None, :] <= jnp.arange(nblk, dtype=jnp.int32)[:, None] * MOE_ROWS, -1),
                        MOE_EXPERTS - 1).astype(jnp.int32)
    nused = (pend[-1:] // MOE_ROWS).astype(jnp.int32)
    xs = _dispatch(dest3, x, xs, tm)
    ys = _experts(blk_e + w["layer"] * MOE_EXPERTS, nused, xs, *expert_w)
    x, xb = _combine(dest3, ys, x, rw, w["ln2_g"], w["ln2_b"], tm)
    return (x, xb, xs), None


def kernel(x, w_in, cmp_pe_k, cmp_pe_v, cmp_w1_k, cmp_w2_k, cmp_w1_v, cmp_w2_v, pool_w, pool_scale, ssm_lam_re, ssm_lam_im, ssm_log_dt, ssm_b_re, ssm_b_im, ssm_c_re, ssm_c_im, ssm_d, ssm_w_glu, ssm_b_glu, w_up_nsa, w_up_pool, w_up_ssm, w_out, ln1_g, ln1_b, router_w_grp, router_b_grp, router_w_exp, router_b_exp, moe_w_gate, moe_w_up, moe_w_down, ln2_g, ln2_b):
    B, S, D = x.shape
    L = w_in.shape[0]
    assert B == SSM_BATCH and D == D_MODEL and S % SEL_CHUNK == 0 and S // SEL_BLOCK <= LANES
    assert Q_TILE == LANES and Q2_TILE % Q_TILE == 0 and SEL_CHUNK % Q2_TILE == 0 and WINDOW % Q2_TILE == 0
    w_a, w_bg, w_c = _prep_in_proj(w_in)
    cw1, cpe, cw2 = _prep_compress(cmp_pe_k, cmp_pe_v, cmp_w1_k, cmp_w2_k, cmp_w1_v, cmp_w2_v)
    bt, a, cm = _prep_ssm(ssm_lam_re, ssm_lam_im, ssm_log_dt, ssm_b_re, ssm_b_im, ssm_c_re, ssm_c_im)
    wr, br = _prep_router(router_w_grp, router_b_grp, router_w_exp, router_b_exp)
    wn = w_up_nsa.reshape(L, KV_GROUPS, GQA, HEAD_DIM, D).transpose(0, 2, 1, 3, 4).reshape(L, HEADS * HEAD_DIM, D)
    row = lambda v: v.reshape(L, 1, -1)
    ws = dict(
        w_a=w_a, w_bg=w_bg, w_c=w_c, cw1=cw1, cpe=cpe, cw2=cw2,
        ov=jnp.broadcast_to(_overlap(S // CMP_STRIDE), (L, S // CMP_STRIDE, LANES)),
        pool_w=pool_w.astype(BF16), pool_scale=row(pool_scale),
        bt=bt, a=a, cm=cm, d=row(ssm_d), w_glu=ssm_w_glu.astype(BF16), b_glu=row(ssm_b_glu),
        wn=wn.astype(BF16), wp=w_up_pool.astype(BF16), ws=w_up_ssm.astype(BF16), wo=w_out.astype(BF16),
        ln1_g=row(ln1_g), ln1_b=row(ln1_b), wr=wr, br=br,
        layer=jnp.arange(L, dtype=jnp.int32),
        ln2_g=row(ln2_g), ln2_b=row(ln2_b))
    xf = x.reshape(B * S, D)
    expert_w = tuple(m.reshape((L * MOE_EXPERTS,) + m.shape[2:]) for m in (moe_w_gate, moe_w_up, moe_w_down))
    xs0 = jnp.zeros((2 * B * S + MOE_EXPERTS * MOE_ROWS, D), F32)
    (xf, _, _), _ = lax.scan(functools.partial(_layer, B, S, _nsa_consts(S), expert_w),
                             (xf, xf.astype(BF16), xs0), ws)
    return xf.reshape(B, S, D)
```

```python
import functools
import math

import jax
import jax.numpy as jnp
from jax import lax
from jax.experimental import pallas as pl
from jax.experimental.pallas import tpu as pltpu

F32 = jnp.float32
BF16 = jnp.bfloat16

D_MODEL = 1024
DEPTH = 4
HEAD_DIM = 64
HEADS = 8
KV_GROUPS = 2
GQA = HEADS // KV_GROUPS
CMP_BLOCK = 32
CMP_STRIDE = 16
CMP_HIDDEN = 128
SEL_BLOCK = 64
SEL_TOPN = 16
WINDOW = 512
FORCE_SCORE = 1.0e4
NEG_INF = -1.0e30
POOL_WINDOWS = (2, 4, 8, 16)
POOL_HALO = 16
SSM_GROUP = 16
SSM_GROUPS = 32
SSM_STATE = 64
SSM_NS = SSM_GROUPS * SSM_STATE
MOE_GROUPS = 4
MOE_EPG = 8
MOE_EXPERTS = 32
MOE_FF = 512
DN_ALPHA = (2 * DEPTH) ** 0.25
LN_EPS = 1e-5

LANES = 128
SUBLANES = 8
VMEM_LIMIT = 48 * 1024 * 1024
Q_TILE = 128
Q2_TILE = 128
SEL_CHUNK = 512
MOE_ROWS = 512
EXP_LANE0 = 32
DMA_UNROLL = 4
SEL_SHIFT = SEL_BLOCK.bit_length() - 1
EPG_SHIFT = MOE_EPG.bit_length() - 1


def _cparams(*sem):
    return pltpu.CompilerParams(dimension_semantics=sem, vmem_limit_bytes=VMEM_LIMIT)


def _gelu(x):
    return 0.5 * x * (1.0 + jnp.tanh(math.sqrt(2.0 / math.pi) * (x + 0.044715 * x * x * x)))


def _sigmoid(x):
    return 1.0 / (1.0 + jnp.exp(-x))


def _dot(a, b):
    return jnp.dot(a, b, preferred_element_type=F32)


def _dot_nt(a, b):
    return lax.dot_general(a, b, (((1,), (1,)), ((), ())), preferred_element_type=F32)


def _layer_norm(h, g, b):
    mu = jnp.mean(h, -1, keepdims=True)
    c = h - mu
    var = jnp.mean(c * c, -1, keepdims=True)
    return c * lax.rsqrt(var + LN_EPS) * g + b


def _proj_kernel(x_ref, w_ref, *out_refs, offs):
    x = x_ref[...]
    for o_ref, (a, n) in zip(out_refs, offs):
        o_ref[...] = _dot(x, w_ref[:, a:a + n]).astype(o_ref.dtype)


def _proj(xb, w, outs, tm):
    T, K = xb.shape
    N = w.shape[1]
    offs, a = [], 0
    for n, _ in outs:
        offs.append((a, n))
        a += n
    assert a == N and T % tm == 0
    return pl.pallas_call(
        functools.partial(_proj_kernel, offs=tuple(offs)),
        grid=(T // tm,),
        in_specs=[pl.BlockSpec((tm, K), lambda i: (i, 0)), pl.BlockSpec((K, N), lambda i: (0, 0))],
        out_specs=[pl.BlockSpec((tm, n), lambda i: (i, 0)) for n, _ in outs],
        out_shape=[jax.ShapeDtypeStruct((T, n), dt) for n, dt in outs],
        compiler_params=_cparams("parallel"),
    )(xb, w)


def _proj_tm_kernel(x_ref, w_ref, o_ref):
    o_ref[...] = _dot(x_ref[...], w_ref[...])


def _proj_time_major(xb, w, B, S, tm):
    K = xb.shape[1]
    N = w.shape[1]
    nt = S // tm
    return pl.pallas_call(
        _proj_tm_kernel,
        grid=(B, nt),
        in_specs=[pl.BlockSpec((tm, K), lambda b, i: (b * nt + i, 0)), pl.BlockSpec((K, N), lambda b, i: (0, 0))],
        out_specs=pl.BlockSpec((tm, N), lambda b, i: (i, b)),
        out_shape=jax.ShapeDtypeStruct((S, B * N), F32),
        compiler_params=_cparams("parallel", "parallel"),
    )(xb, w)


def _compress_kernel(kr_ref, w1_ref, pe_ref, w2_ref, out_ref):
    nc = kr_ref.shape[0]
    hw = 4 * CMP_HIDDEN
    a = _dot(kr_ref[...], w1_ref[...])
    pb = _dot(pe_ref[...], w1_ref[...])
    second = pltpu.roll(a[:, hw:], nc - 1, 0)
    row = lax.broadcasted_iota(jnp.int32, (nc, hw), 0)
    second = jnp.where(row < nc - 1, second, 0.0)
    hid = _gelu(a[:, :hw] + second + pb[0:1, :hw] + pb[1:2, hw:])
    out_ref[0] = _dot(hid.astype(BF16), w2_ref[...]).astype(out_ref.dtype)


def _compress(kr, w1, pe_in, w2, B, nc):
    return pl.pallas_call(
        _compress_kernel,
        grid=(B,),
        in_specs=[pl.BlockSpec((nc, kr.shape[1]), lambda b: (b, 0)),
                  pl.BlockSpec(w1.shape, lambda b: (0, 0)),
                  pl.BlockSpec(pe_in.shape, lambda b: (0, 0)),
                  pl.BlockSpec(w2.shape, lambda b: (0, 0))],
        out_specs=pl.BlockSpec((1, nc, 2 * LANES), lambda b: (b, 0, 0)),
        out_shape=jax.ShapeDtypeStruct((B, nc, 2 * LANES), BF16),
        compiler_params=_cparams("parallel"),
    )(kr, w1, pe_in, w2)


def _slope(h):
    return 2.0 ** (-8.0 * (h + 1) / HEADS)


def _stack_heads(q, g):
    return jnp.concatenate([q[:, (g * GQA + r) * LANES:(g * GQA + r + 1) * LANES] for r in range(GQA)], axis=0)


def _split3_dot(a, b):
    hi = a.astype(BF16)
    r1 = a - hi.astype(F32)
    mid = r1.astype(BF16)
    lo = (r1 - mid.astype(F32)).astype(BF16)
    return _dot(hi, b) + _dot(mid, b) + _dot(lo, b)


def _nsa1_kernel(q_ref, kvc_ref, ng_ref, ov_ref, qaux_ref, caux_ref, acc_ref, sel_ref, un_ref, sc_ref, oc_sc):
    Q = q_ref.shape[0]
    nc = kvc_ref.shape[1]
    qb = pl.program_id(1)
    t0 = qb * Q
    lane = lax.broadcasted_iota(jnp.int32, (Q, LANES), 1)
    tb = jnp.right_shift(t0 + lax.broadcasted_iota(jnp.int32, (Q, LANES), 0), SEL_SHIFT)
    jrow = lax.broadcasted_iota(jnp.int32, (LANES, KV_GROUPS * Q), 0).astype(F32)

    def attend(n):
        q = q_ref[...]
        kc = jnp.concatenate([kvc_ref[0, 0:n, 0:LANES], caux_ref[0:n, :]], axis=1)
        vc = kvc_ref[0, 0:n, LANES:2 * LANES]
        qrow = lax.broadcasted_iota(jnp.int32, (Q, n), 0)
        ccol = lax.broadcasted_iota(jnp.int32, (Q, n), 1)
        bias = jnp.where((t0 + qrow) >= (ccol * CMP_STRIDE + (CMP_BLOCK - 1)), 0.0, NEG_INF)
        bias4 = jnp.concatenate([bias] * GQA, axis=0)
        for g in range(KV_GROUPS):
            qs = jnp.concatenate([_stack_heads(q, g), qaux_ref[g]], axis=1)
            s = _dot_nt(qs, kc) + bias4
            m = jnp.max(s, -1, keepdims=True)
            e = jnp.exp(s - m)
            inv = jnp.where(m > 0.5 * NEG_INF, 1.0 / jnp.maximum(jnp.sum(e, -1, keepdims=True), 1e-30), 0.0)
            p = e * inv
            oc_sc[g] = _dot(p.astype(BF16), vc)
            psum = p[0:Q]
            for r in range(1, GQA):
                psum = psum + p[r * Q:(r + 1) * Q]
            imp = _split3_dot(psum, ov_ref[0:n, :])
            forced = (lane == 0) | (lane == tb) | (lane == tb - 1)
            score = jnp.where(forced, FORCE_SCORE, jnp.where(lane <= tb, imp, -1.0))
            sc_ref[:, g * Q:(g + 1) * Q] = score.T

    step = min(LANES, nc)
    need = (t0 + Q - CMP_BLOCK) // CMP_STRIDE + 1
    variant = (need + step - 1) // step
    for v in range(1, nc // step + 1):
        pl.when(variant == v)(functools.partial(attend, v * step))
    gate = _sigmoid(ng_ref[...])
    outs = [[gate[:, 3 * (g * GQA + r):3 * (g * GQA + r) + 1] * oc_sc[g, r * Q:(r + 1) * Q, :] for r in range(GQA)]
            for g in range(KV_GROUPS)]
    work = sc_ref[...]
    taken = jnp.zeros(work.shape, F32)
    for _ in range(SEL_TOPN):
        best = jnp.max(work, 0, keepdims=True)
        first = jnp.min(jnp.where(work == best, jrow, float(LANES)), 0, keepdims=True)
        hit = jrow == first
        taken = jnp.where(hit, 1.0, taken)
        work = jnp.where(hit, -2.0, work)
    for g in range(KV_GROUPS):
        sel = taken[:, g * Q:(g + 1) * Q].T
        sel = jnp.where(lane <= tb, sel, 0.0)
        sel_ref[:, g * LANES:(g + 1) * LANES] = sel.astype(sel_ref.dtype)
        un_ref[:, g * LANES:(g + 1) * LANES] = jnp.broadcast_to(jnp.max(sel, 0, keepdims=True), (SUBLANES, LANES))
    for r in range(GQA):
        acc_ref[:, r * LANES:(r + 1) * LANES] = jnp.where(lane < HEAD_DIM, outs[0][r], outs[1][r])


def _nsa1(qw, kvc, ng, ov, qaux, caux, B, S):
    nq = S // Q_TILE
    nc = kvc.shape[1]
    T = B * S
    rows = lambda b, i: (b * nq + i, 0)
    return pl.pallas_call(
        _nsa1_kernel,
        grid=(B, nq),
        in_specs=[pl.BlockSpec((Q_TILE, HEADS * LANES), rows),
                  pl.BlockSpec((1, nc, 2 * LANES), lambda b, i: (b, 0, 0)),
                  pl.BlockSpec((Q_TILE, LANES), rows),
                  pl.BlockSpec((nc, LANES), lambda b, i: (0, 0)),
                  pl.BlockSpec(qaux.shape, lambda b, i: (0, 0, 0)),
                  pl.BlockSpec((nc, LANES), lambda b, i: (0, 0))],
        out_specs=[pl.BlockSpec((Q_TILE, GQA * LANES), rows),
                   pl.BlockSpec((Q_TILE, KV_GROUPS * LANES), rows),
                   pl.BlockSpec((SUBLANES, KV_GROUPS * LANES), rows)],
        out_shape=[jax.ShapeDtypeStruct((T, GQA * LANES), F32),
                   jax.ShapeDtypeStruct((T, KV_GROUPS * LANES), BF16),
                   jax.ShapeDtypeStruct((B * nq * SUBLANES, KV_GROUPS * LANES), F32)],
        scratch_shapes=[pltpu.VMEM((LANES, KV_GROUPS * Q_TILE), F32),
                        pltpu.VMEM((KV_GROUPS, GQA * Q_TILE, LANES), F32)],
        compiler_params=_cparams("parallel", "parallel"),
    )(qw, kvc, ng, ov, qaux, caux)


def _nsa2_kernel(lst_ref, cnt_ref, q_ref, ks_ref, vs_ref, kw_ref, vw_ref, sel_ref, ng_ref, acc_ref,
                 qaux_ref, kaux_ref, ex_ref, o_ref, m_sc, l_sc, a_sc, ow_sc, s0_sc, s1_sc, sd_sc):
    Q = q_ref.shape[0]
    C = SEL_CHUNK
    WK = WINDOW + Q
    maxch = ex_ref.shape[1] // C
    tile = pl.program_id(0) * pl.num_programs(1) + pl.program_id(1)
    t0 = pl.program_id(1) * Q
    q = q_ref[...]
    gate = _sigmoid(ng_ref[...])
    lane = lax.broadcasted_iota(jnp.int32, (Q, LANES), 1)
    cdiag = t0 // C
    start = pl.multiple_of(jnp.maximum(t0 - WINDOW, 0), Q)
    wrow = lax.broadcasted_iota(jnp.int32, (Q, WK), 0)
    wcol = lax.broadcasted_iota(jnp.int32, (Q, WK), 1)
    wdist = (t0 + wrow) - (start + wcol)
    wbias = jnp.where((wdist >= 0) & (wdist < WINDOW), 0.0, NEG_INF)
    wbias4 = jnp.concatenate([wbias] * GQA, axis=0)
    qrow = lax.broadcasted_iota(jnp.int32, (Q, C), 0)
    kcol = lax.broadcasted_iota(jnp.int32, (Q, C), 1)
    outs = []
    for g in range(KV_GROUPS):
        qs = jnp.concatenate([_stack_heads(q, g), qaux_ref[g]], axis=1)
        selg = sel_ref[:, g * LANES:(g + 1) * LANES]
        far = cdiag > 0
        selg = jnp.where(lane == 0, selg * jnp.where(far, 0.0, 1.0).astype(selg.dtype), selg)
        k0 = jnp.concatenate([ks_ref[0:SEL_BLOCK, :], kaux_ref[0:SEL_BLOCK, :]], axis=1)
        s0 = _dot_nt(qs, k0) + jnp.where(far, 0.0, NEG_INF)
        m0 = jnp.max(s0, -1, keepdims=True)
        e0 = jnp.exp(s0 - m0)
        m_sc[...] = jnp.broadcast_to(m0, m_sc.shape)
        l_sc[...] = jnp.broadcast_to(jnp.sum(e0, -1, keepdims=True), l_sc.shape)
        a_sc[...] = _dot(e0.astype(BF16), vs_ref[0:SEL_BLOCK, :])

        def scores(c, s_out, qs=qs, selg=selg):
            k0 = pl.multiple_of(c * C, C)
            kk = jnp.concatenate([ks_ref[pl.ds(k0, C), :], kaux_ref[pl.ds(k0, C), :]], axis=1)
            valid = (_dot(selg, ex_ref[:, pl.ds(k0, C)]) > 0.5) & ((t0 + qrow) >= (k0 + kcol))
            bias = jnp.where(valid, 0.0, NEG_INF)
            s_out[...] = _dot_nt(qs, kk) + jnp.concatenate([bias] * GQA, axis=0)

        def accumulate(c, s_in):
            k0 = pl.multiple_of(c * C, C)
            s = s_in[...]
            m_old = m_sc[...]
            m_new = jnp.maximum(m_old, jnp.max(s, -1, keepdims=True))
            alpha = jnp.exp(m_old - m_new)
            e = jnp.exp(s - m_new[:, 0:1])
            l_sc[...] = alpha * l_sc[...] + jnp.sum(e, -1, keepdims=True)
            a_sc[...] = alpha * a_sc[...] + _dot(e.astype(BF16), vs_ref[pl.ds(k0, C), :])
            m_sc[...] = m_new

        base = (tile * KV_GROUPS + g) * maxch
        scores(cdiag, sd_sc)
        scores(lst_ref[base], s0_sc)
        kw = jnp.concatenate([kw_ref[pl.ds(start, WK), :], kaux_ref[pl.ds(start, WK), :]], axis=1)
        sw = _dot_nt(qs, kw) + wbias4
        e = jnp.exp(sw - jnp.max(sw, -1, keepdims=True))
        ow_sc[...] = _dot(e.astype(BF16), vw_ref[pl.ds(start, WK), :]) / jnp.maximum(jnp.sum(e, -1, keepdims=True), 1e-30)

        def step(j, carry, scores=scores, accumulate=accumulate, base=base):
            cur = lst_ref[base + j]
            nxt = lst_ref[base + j + 1]

            @pl.when(j % 2 == 0)
            def _():
                scores(nxt, s1_sc)
                accumulate(cur, s0_sc)

            @pl.when(j % 2 == 1)
            def _():
                scores(nxt, s0_sc)
                accumulate(cur, s1_sc)

            return carry

        lax.fori_loop(0, cnt_ref[tile * KV_GROUPS + g], step, 0)
        accumulate(cdiag, sd_sc)
        o_w = ow_sc[...]
        o_s = a_sc[...] / jnp.maximum(l_sc[...], 1e-30)
        og = []
        for r in range(GQA):
            h = g * GQA + r
            rows = slice(r * Q, (r + 1) * Q)
            og.append(gate[:, 3 * h + 1:3 * h + 2] * o_s[rows] + gate[:, 3 * h + 2:3 * h + 3] * o_w[rows])
        outs.append(og)
    for r in range(GQA):
        cols = slice(r * LANES, (r + 1) * LANES)
        o_ref[:, cols] = (acc_ref[:, cols] + jnp.where(lane < HEAD_DIM, outs[0][r], outs[1][r])).astype(o_ref.dtype)


def _chunk_lists(un, B, S):
    nq = S // Q2_TILE
    maxch = S // SEL_CHUNK
    per = SEL_CHUNK // SEL_BLOCK
    un = un.reshape(B * nq, Q2_TILE // Q_TILE, SUBLANES, KV_GROUPS, LANES)[:, :, 0, :, :maxch * per].max(1)
    un = jnp.where(jnp.arange(maxch * per) == 0, 0.0, un)
    flags = un.reshape(B * nq, KV_GROUPS, maxch, per).max(-1) > 0.5
    cdiag = (jnp.arange(B * nq, dtype=jnp.int32) % nq) * Q2_TILE // SEL_CHUNK
    flags = flags & (jnp.arange(maxch, dtype=jnp.int32)[None, None, :] < cdiag[:, None, None])
    pos = jnp.cumsum(flags.astype(jnp.int32), -1)
    hit = flags[..., None, :] & (pos[..., None, :] == jnp.arange(1, maxch + 1, dtype=jnp.int32)[:, None])
    lst = jnp.sum(jnp.where(hit, jnp.arange(maxch, dtype=jnp.int32), 0), -1)
    return lst.astype(jnp.int32).reshape(-1), pos[..., -1].reshape(-1)


def _nsa2(lst, cnt, qw, kv, sel, ng, acc, qaux, kaux, ex, B, S):
    Q = Q2_TILE
    nq = S // Q
    T = B * S
    rows = lambda b, i, *_: (b * nq + i, 0)
    kvspec = lambda col: pl.BlockSpec((S, LANES), lambda b, i, *_, col=col: (b, col))
    const = lambda shp: pl.BlockSpec(shp, lambda b, i, *_: tuple(0 for _ in shp))
    return pl.pallas_call(
        _nsa2_kernel,
        grid_spec=pltpu.PrefetchScalarGridSpec(
            num_scalar_prefetch=2, grid=(B, nq),
            in_specs=[pl.BlockSpec((Q, HEADS * LANES), rows),
                      kvspec(2), kvspec(3), kvspec(4), kvspec(5),
                      pl.BlockSpec((Q, KV_GROUPS * LANES), rows),
                      pl.BlockSpec((Q, LANES), rows),
                      pl.BlockSpec((Q, GQA * LANES), rows),
                      const(qaux.shape), const(kaux.shape), const(ex.shape)],
            out_specs=pl.BlockSpec((Q, GQA * LANES), rows),
            scratch_shapes=[pltpu.VMEM((GQA * Q, LANES), F32)] * 4
            + [pltpu.VMEM((GQA * Q, SEL_CHUNK), F32)] * 3),
        out_shape=jax.ShapeDtypeStruct((T, GQA * LANES), BF16),
        compiler_params=_cparams("parallel", "parallel"),
    )(lst, cnt, qw, kv, kv, kv, kv, sel, ng, acc, qaux, kaux, ex)


def _nsa_consts(S):
    h = jnp.arange(HEADS, dtype=F32).reshape(KV_GROUPS, GQA, 1, 1)
    slope = jnp.exp2(-8.0 * (h + 1.0) / HEADS)
    lane = jnp.arange(LANES)[None, None, None, :]
    qaux = jnp.where(lane == 0, slope * SEL_BLOCK, jnp.where(lane == 1, slope, 0.0))
    qaux1, qaux2 = (jnp.broadcast_to(qaux, (KV_GROUPS, GQA, n, LANES)).reshape(KV_GROUPS, GQA * n, LANES).astype(BF16)
                    for n in (Q_TILE, Q2_TILE))

    def pos_cols(pos):
        l2 = jnp.arange(LANES)[None, :]
        return jnp.where(l2 == 0, pos[:, None] // SEL_BLOCK, jnp.where(l2 == 1, pos[:, None] % SEL_BLOCK, 0))

    kaux = pos_cols(jnp.arange(S))
    caux = pos_cols(jnp.arange(S // CMP_STRIDE) * CMP_STRIDE + (CMP_BLOCK - 1))
    ex = (jnp.arange(LANES)[:, None] == (jnp.arange(S)[None, :] // SEL_BLOCK))
    return qaux1, qaux2, kaux.astype(BF16), caux.astype(BF16), ex.astype(BF16)


def _pool_kernel(u_ref, halo_ref, w_ref, sc_ref, o_ref, x_sc):
    ts = u_ref.shape[0]
    i = pl.program_id(1)
    x_sc[0:POOL_HALO] = jnp.where(i > 0, halo_ref[...], 0.0)
    x_sc[POOL_HALO:POOL_HALO + ts] = u_ref[...]
    t = i * ts + lax.broadcasted_iota(jnp.int32, (ts, LANES), 0)
    for g, w in enumerate(POOL_WINDOWS):
        cols = slice(g * LANES, (g + 1) * LANES)
        cur = x_sc[POOL_HALO:POOL_HALO + ts, cols]
        acc = cur
        for k in range(1, w):
            acc = acc + x_sc[POOL_HALO - k:POOL_HALO - k + ts, cols]
        cnt = jnp.minimum(t + 1, w).astype(F32)
        pooled = acc / cnt - cur
        o_ref[:, cols] = (_dot(pooled.astype(BF16), w_ref[g]) * sc_ref[:, cols]).astype(o_ref.dtype)


def _pool(u, pool_w, pool_scale, B, S, ts):
    nt = S // ts
    hb = ts // POOL_HALO
    C = u.shape[1]
    return pl.pallas_call(
        _pool_kernel,
        grid=(B, nt),
        in_specs=[pl.BlockSpec((ts, C), lambda b, i: (b * nt + i, 0)),
                  pl.BlockSpec((POOL_HALO, C), lambda b, i: (jnp.maximum((b * nt + i) * hb - 1, 0), 0)),
                  pl.BlockSpec(pool_w.shape, lambda b, i: (0, 0, 0)),
                  pl.BlockSpec((1, C), lambda b, i: (0, 0))],
        out_specs=pl.BlockSpec((ts, C), lambda b, i: (b * nt + i, 0)),
        out_shape=jax.ShapeDtypeStruct((B * S, C), BF16),
        scratch_shapes=[pltpu.VMEM((POOL_HALO + ts, C), F32)],
        compiler_params=_cparams("parallel", "parallel"),
    )(u, u, pool_w, pool_scale)


SSM_BATCH = 4
SSM_PASS = 8
SSM_BLOCKS = SSM_GROUPS * SSM_GROUP // LANES
SSM_BLK_W = 2 * SSM_NS // SSM_BLOCKS


def _ssm_kernel(u_ref, bt_ref, a_ref, c_ref, d_ref, wg_ref, bg_ref, o_ref, xh, hst, io_sc):
    steps = u_ref.shape[0]
    C = u_ref.shape[1] // SSM_BATCH
    rows = steps * SSM_BATCH

    @pl.when(pl.program_id(0) == 0)
    def _():
        hst[...] = jnp.zeros(hst.shape, F32)

    nslab = C // LANES
    for b in range(SSM_BATCH):
        for j in range(nslab):
            io_sc[j, pl.ds(b, steps, stride=SSM_BATCH), :] = u_ref[:, b * C + j * LANES:b * C + (j + 1) * LANES]
    u = jnp.concatenate([io_sc[j] for j in range(nslab)], axis=1)
    ub = u.astype(BF16)
    for k in range(SSM_BLOCKS):
        xh[:, k * SSM_BLK_W:(k + 1) * SSM_BLK_W] = _dot(ub[:, k * LANES:(k + 1) * LANES], bt_ref[k])
    low = lax.broadcasted_iota(jnp.int32, (SUBLANES, LANES), 0) < 4
    nchunk = SSM_NS // LANES
    per_blk = nchunk // SSM_BLOCKS
    for p in range(nchunk // SSM_PASS):
        cols = [p * SSM_PASS + j for j in range(SSM_PASS)]
        start = [(c // per_blk) * SSM_BLK_W + (c % per_blk) * LANES for c in cols]
        re = [slice(s, s + LANES) for s in start]
        im = [slice(s + SSM_BLK_W // 2, s + SSM_BLK_W // 2 + LANES) for s in start]
        ar = [a_ref[:, s] for s in re]
        ai = [a_ref[:, s] for s in im]

        def tile(k, carry, re=re, im=im, ar=ar, ai=ai):
            cr, ci = carry
            r0 = pl.multiple_of(k * SUBLANES, SUBLANES)
            ncr, nci = [], []
            for j in range(SSM_PASS):
                vr = xh[pl.ds(r0, SUBLANES), re[j]]
                vi = xh[pl.ds(r0, SUBLANES), im[j]]
                t1r = ar[j] * cr[j] - ai[j] * ci[j] + vr
                t1i = ar[j] * ci[j] + ai[j] * cr[j] + vi
                pr = pltpu.roll(t1r, 4, 0)
                pi = pltpu.roll(t1i, 4, 0)
                t2r = ar[j] * pr - ai[j] * pi + vr
                t2i = ar[j] * pi + ai[j] * pr + vi
                xh[pl.ds(r0, SUBLANES), re[j]] = jnp.where(low, t1r, t2r)
                xh[pl.ds(r0, SUBLANES), im[j]] = jnp.where(low, t1i, t2i)
                ncr.append(pltpu.roll(t2r, 4, 0))
                nci.append(pltpu.roll(t2i, 4, 0))
            return tuple(ncr), tuple(nci)

        cr0 = tuple(hst[:, s] for s in re)
        ci0 = tuple(hst[:, s] for s in im)
        cr, ci = lax.fori_loop(0, rows // SUBLANES, tile, (cr0, ci0))
        for j in range(SSM_PASS):
            hst[:, re[j]] = cr[j]
            hst[:, im[j]] = ci[j]
    y = jnp.concatenate([_dot(xh[:, k * SSM_BLK_W:(k + 1) * SSM_BLK_W].astype(BF16), c_ref[k])
                         for k in range(SSM_BLOCKS)], axis=1) + d_ref[...] * u
    z = _gelu(y)
    out = z * _sigmoid(_dot(z.astype(BF16), wg_ref[...]) + bg_ref[...])
    for j in range(nslab):
        io_sc[j] = out[:, j * LANES:(j + 1) * LANES]
    for b in range(SSM_BATCH):
        for j in range(nslab):
            o_ref[:, b * C + j * LANES:b * C + (j + 1) * LANES] = (
                io_sc[j, pl.ds(b, steps, stride=SSM_BATCH), :].astype(o_ref.dtype))


def _ssm(u, bt, a_b, cmat, d_skip, w_glu, b_glu, steps):
    S, W = u.shape
    rows = steps * SSM_BATCH
    const = lambda shp: pl.BlockSpec(shp, lambda i: tuple(0 for _ in shp))
    return pl.pallas_call(
        _ssm_kernel,
        grid=(S // steps,),
        in_specs=[pl.BlockSpec((steps, W), lambda i: (i, 0)),
                  const(bt.shape), const(a_b.shape), const(cmat.shape), const(d_skip.shape),
                  const(w_glu.shape), const(b_glu.shape)],
        out_specs=pl.BlockSpec((steps, W), lambda i: (i, 0)),
        out_shape=jax.ShapeDtypeStruct((S, W), BF16),
        scratch_shapes=[pltpu.VMEM((rows, 2 * SSM_NS), F32), pltpu.VMEM((SUBLANES, 2 * SSM_NS), F32),
                        pltpu.VMEM((W // SSM_BATCH // LANES, rows, LANES), F32)],
        compiler_params=_cparams("arbitrary"),
    )(u, bt, a_b, cmat, d_skip, w_glu, b_glu)


def _merge_kernel(on_ref, op_ref, os_ref, xin_ref, x_ref, wbg_ref, wn_ref, wp_ref, ws_ref, wo_ref, g_ref, b_ref,
                  xo_ref, xb_ref):
    D = x_ref.shape[1]
    xin = xin_ref[...]
    m = None
    for k, (o_ref, w_ref) in enumerate(((on_ref, wn_ref), (op_ref, wp_ref), (os_ref, ws_ref))):
        gate = _sigmoid(_dot(xin, wbg_ref[:, k * D:(k + 1) * D]))
        term = gate * _dot(o_ref[...], w_ref[...])
        m = term if m is None else m + term
    h = DN_ALPHA * x_ref[...] + _dot(m.astype(BF16), wo_ref[...])
    y = _layer_norm(h, g_ref[...], b_ref[...])
    xo_ref[...] = y
    xb_ref[...] = y.astype(BF16)


def _merge(o_nsa, o_pool, o_ssm_tm, xb, x, wbg, wn, wp, ws, wo, g, b, B, S, tm):
    nt = S // tm
    T, D = x.shape
    W = o_nsa.shape[1]
    rows = lambda bb, i: (bb * nt + i, 0)
    const = lambda shp: pl.BlockSpec(shp, lambda bb, i: tuple(0 for _ in shp))
    return pl.pallas_call(
        _merge_kernel,
        grid=(B, nt),
        in_specs=[pl.BlockSpec((tm, W), rows), pl.BlockSpec((tm, W), rows),
                  pl.BlockSpec((tm, W), lambda bb, i: (i, bb)),
                  pl.BlockSpec((tm, D), rows), pl.BlockSpec((tm, D), rows),
                  const(wbg.shape), const(wn.shape), const(wp.shape), const(ws.shape), const(wo.shape),
                  const(g.shape), const(b.shape)],
        out_specs=[pl.BlockSpec((tm, D), rows), pl.BlockSpec((tm, D), rows)],
        out_shape=[jax.ShapeDtypeStruct((T, D), F32), jax.ShapeDtypeStruct((T, D), BF16)],
        compiler_params=_cparams("parallel", "parallel"),
    )(o_nsa, o_pool, o_ssm_tm, xb, x, wbg, wn, wp, ws, wo, g, b)


def _router_kernel(x_ref, wr_ref, br_ref, ri_ref, rw_ref, cnt_ref, carry):
    tm = x_ref.shape[0]

    @pl.when(pl.program_id(0) == 0)
    def _():
        carry[...] = jnp.zeros(carry.shape, F32)

    x = x_ref[...]
    xh = x.astype(BF16)
    xl = (x - xh.astype(F32)).astype(BF16)
    both = _dot(xh, wr_ref[...])
    logits = both[:, 0:LANES] + both[:, LANES:2 * LANES] + _dot(xl, wr_ref[:, 0:LANES]) + br_ref[...]
    lane = lax.broadcasted_iota(jnp.int32, (tm, LANES), 1)
    big = jnp.int32(LANES)
    isg = lane < MOE_GROUPS
    lg = jnp.where(isg, logits, NEG_INF)
    eg = jnp.where(isg, jnp.exp(lg - jnp.max(lg, -1, keepdims=True)), 0.0)
    pgv = jnp.where(isg, eg / jnp.sum(eg, -1, keepdims=True), -1.0)
    p_g = jnp.max(pgv, -1, keepdims=True)
    g_sel = jnp.min(jnp.where(pgv == p_g, lane, big), -1, keepdims=True)
    ine = ((lane >= EXP_LANE0) & (lane < EXP_LANE0 + MOE_EXPERTS)
           & (jnp.right_shift(lane - EXP_LANE0, EPG_SHIFT) == g_sel))
    le = jnp.where(ine, logits, NEG_INF)
    ee = jnp.where(ine, jnp.exp(le - jnp.max(le, -1, keepdims=True)), 0.0)
    p = jnp.where(ine, ee / jnp.sum(ee, -1, keepdims=True), -1.0)
    p1 = jnp.max(p, -1, keepdims=True)
    i1 = jnp.min(jnp.where(p == p1, lane, big), -1, keepdims=True)
    prest = jnp.where(lane == i1, -1.0, p)
    p2 = jnp.max(prest, -1, keepdims=True)
    i2 = jnp.min(jnp.where(prest == p2, lane, big), -1, keepdims=True)
    den = p1 + p2
    w0 = p1 / den * p_g
    w1 = p2 / den * p_g
    e0 = i1 - EXP_LANE0
    e1 = i2 - EXP_LANE0
    is0 = lane == e0
    is1 = lane == e1
    onehot = jnp.where(is0 | is1, 1.0, 0.0)
    r = lax.broadcasted_iota(jnp.int32, (tm, tm), 0)
    c = lax.broadcasted_iota(jnp.int32, (tm, tm), 1)
    tri = jnp.where(r > c, 1.0, 0.0).astype(BF16)
    cum = _dot(tri, onehot.astype(BF16)) + carry[0:1, :]
    rank0 = jnp.sum(jnp.where(is0, cum, 0.0), -1, keepdims=True).astype(jnp.int32)
    rank1 = jnp.sum(jnp.where(is1, cum, 0.0), -1, keepdims=True).astype(jnp.int32)
    carry[...] = carry[...] + jnp.sum(onehot, 0, keepdims=True)
    cnt_ref[...] = carry[...]
    zi = jnp.zeros((tm, LANES), jnp.int32)
    ri_ref[...] = jnp.where(lane == 0, e0, jnp.where(lane == 1, e1, jnp.where(lane == 2, rank0, jnp.where(lane == 3, rank1, zi))))
    rw_ref[...] = jnp.where(lane == 0, w0, jnp.where(lane == 1, w1, 0.0))


def _router(x, wr, br, tm):
    T, D = x.shape
    return pl.pallas_call(
        _router_kernel,
        grid=(T // tm,),
        in_specs=[pl.BlockSpec((tm, D), lambda i: (i, 0)),
                  pl.BlockSpec(wr.shape, lambda i: (0, 0)), pl.BlockSpec(br.shape, lambda i: (0, 0))],
        out_specs=[pl.BlockSpec((tm, LANES), lambda i: (i, 0)), pl.BlockSpec((tm, LANES), lambda i: (i, 0)),
                   pl.BlockSpec((SUBLANES, LANES), lambda i: (0, 0))],
        out_shape=[jax.ShapeDtypeStruct((T, LANES), jnp.int32), jax.ShapeDtypeStruct((T, LANES), F32),
                   jax.ShapeDtypeStruct((SUBLANES, LANES), F32)],
        scratch_shapes=[pltpu.VMEM((SUBLANES, LANES), F32)],
        compiler_params=_cparams("arbitrary"),
    )(x, wr, br)


def _row_copy(src, s, dst, d, sem):
    return pltpu.make_async_copy(src.at[pl.ds(s, 1)], dst.at[pl.ds(d, 1)], sem)


def _dispatch_kernel(dest_ref, x_ref, xs_in, xs_hbm, sem, *, tm):
    del xs_in

    def issue(r, c):
        for k in range(2):
            _row_copy(x_ref, r, xs_hbm, dest_ref[0, 0, 2 * r + k], sem).start(priority=k)
        return c

    lax.fori_loop(0, tm, issue, 0, unroll=DMA_UNROLL)
    pltpu.make_async_copy(xs_hbm.at[pl.ds(0, 2 * tm)], xs_hbm.at[pl.ds(0, 2 * tm)], sem).wait()


def _dispatch(dest3, x, xs0, tm):
    T = x.shape[0]
    return pl.pallas_call(
        functools.partial(_dispatch_kernel, tm=tm),
        grid=(T // tm,),
        in_specs=[pl.BlockSpec((1, 1, 2 * tm), lambda i: (i, 0, 0), memory_space=pltpu.SMEM),
                  pl.BlockSpec((tm, x.shape[1]), lambda i: (i, 0)), pl.BlockSpec(memory_space=pl.ANY)],
        out_specs=pl.BlockSpec(memory_space=pl.ANY),
        out_shape=jax.ShapeDtypeStruct(xs0.shape, xs0.dtype),
        scratch_shapes=[pltpu.SemaphoreType.DMA(())],
        input_output_aliases={2: 0},
        compiler_params=_cparams("arbitrary"),
    )(dest3, x, xs0)


def _expert_kernel(be_ref, nu_ref, xs_ref, wg_ref, wu_ref, wd_ref, ys_ref):
    i = pl.program_id(0)

    @pl.when(i < nu_ref[0])
    def _():
        x = xs_ref[...].astype(BF16)
        a = _dot(x, wg_ref[0].astype(BF16))
        h = a * _sigmoid(a) * _dot(x, wu_ref[0].astype(BF16))
        ys_ref[...] = _dot(h.astype(BF16), wd_ref[0].astype(BF16))

    @pl.when(i >= nu_ref[0])
    def _():
        ys_ref[...] = jnp.zeros(ys_ref.shape, F32)


def _experts(blk_e, nused, xs, wg, wu, wd):
    P, D = xs.shape
    F = wg.shape[2]
    nblk = P // MOE_ROWS
    return pl.pallas_call(
        _expert_kernel,
        grid_spec=pltpu.PrefetchScalarGridSpec(
            num_scalar_prefetch=2, grid=(nblk,),
            in_specs=[pl.BlockSpec((MOE_ROWS, D), lambda i, be, nu: (i, 0)),
                      pl.BlockSpec((1, D, F), lambda i, be, nu: (be[i], 0, 0)),
                      pl.BlockSpec((1, D, F), lambda i, be, nu: (be[i], 0, 0)),
                      pl.BlockSpec((1, F, D), lambda i, be, nu: (be[i], 0, 0))],
            out_specs=pl.BlockSpec((MOE_ROWS, D), lambda i, be, nu: (i, 0))),
        out_shape=jax.ShapeDtypeStruct((P, D), F32),
        compiler_params=_cparams("arbitrary"),
    )(blk_e, nused, xs, wg, wu, wd)


def _combine_kernel(dest_ref, ys_hbm, x_ref, rw_ref, g_ref, b_ref, xo_ref, xb_ref, buf, sem, *, tm):
    def issue(r, c):
        for k in range(2):
            _row_copy(ys_hbm, dest_ref[0, 0, 2 * r + k], buf.at[k], r, sem).start(priority=k)
        return c

    lax.fori_loop(0, tm, issue, 0, unroll=DMA_UNROLL)
    for k in range(2):
        pltpu.make_async_copy(ys_hbm.at[pl.ds(0, tm)], buf.at[k], sem).wait()
    w = rw_ref[...]
    h = DN_ALPHA * x_ref[...] + w[:, 0:1] * buf[0] + w[:, 1:2] * buf[1]
    y = _layer_norm(h, g_ref[...], b_ref[...])
    xo_ref[...] = y
    xb_ref[...] = y.astype(BF16)


def _combine(dest3, ys, x, rw, g, b, tm):
    T, D = x.shape
    return pl.pallas_call(
        functools.partial(_combine_kernel, tm=tm),
        grid=(T // tm,),
        in_specs=[pl.BlockSpec((1, 1, 2 * tm), lambda i: (i, 0, 0), memory_space=pltpu.SMEM),
                  pl.BlockSpec(memory_space=pl.ANY),
                  pl.BlockSpec((tm, D), lambda i: (i, 0)), pl.BlockSpec((tm, LANES), lambda i: (i, 0)),
                  pl.BlockSpec(g.shape, lambda i: (0, 0)), pl.BlockSpec(b.shape, lambda i: (0, 0))],
        out_specs=[pl.BlockSpec((tm, D), lambda i: (i, 0)), pl.BlockSpec((tm, D), lambda i: (i, 0))],
        out_shape=[jax.ShapeDtypeStruct((T, D), F32), jax.ShapeDtypeStruct((T, D), BF16)],
        scratch_shapes=[pltpu.VMEM((2, tm, D), F32), pltpu.SemaphoreType.DMA(())],
        compiler_params=_cparams("arbitrary"),
    )(dest3, ys, x, rw, g, b)


def _prep_in_proj(w_in):
    L, D, _ = w_in.shape
    w_in = w_in.astype(BF16)
    nw = HEADS * HEAD_DIM
    kvw = 6 * KV_GROUPS * HEAD_DIM
    o = 0
    wq = w_in[..., o:o + nw]; o += nw
    wkv = w_in[..., o:o + kvw]; o += kvw
    wng = w_in[..., o:o + 3 * HEADS]; o += 3 * HEADS
    wpool = w_in[..., o:o + D // 2]; o += D // 2
    wssm = w_in[..., o:o + D // 2]; o += D // 2
    wbg = w_in[..., o:]
    wq = wq.reshape(L, D, KV_GROUPS, GQA, HEAD_DIM) * (HEAD_DIM ** -0.5)
    z = jnp.zeros_like(wq[:, :, 0])
    wq_wide = jnp.stack([jnp.concatenate([wq[:, :, 0], z], -1), jnp.concatenate([z, wq[:, :, 1]], -1)], 2)
    wq_wide = wq_wide.reshape(L, D, HEADS * LANES)
    wng = jnp.pad(wng, ((0, 0), (0, 0), (0, LANES - 3 * HEADS)))
    w_a = jnp.concatenate([wq_wide, wkv, wng, wpool], -1).astype(BF16)
    return w_a, wbg.astype(BF16), wssm.astype(BF16)


def _prep_compress(pe_k, pe_v, w1k, w2k, w1v, w2v):
    L = pe_k.shape[0]
    half = CMP_BLOCK // 2
    cw = 2 * KV_GROUPS * HEAD_DIM
    eye_s = jnp.eye(2, dtype=F32)
    eye_g = jnp.eye(KV_GROUPS, dtype=F32)
    w1r = jnp.stack([w1k, w1v], 1).reshape(L, 2, 2, half, HEAD_DIM, CMP_HIDDEN)
    w1 = jnp.einsum('aspldh,st,gj->alsgdptjh', w1r, eye_s, eye_g)
    per = jnp.stack([pe_k, pe_v], 1).reshape(L, 2, 2, half, HEAD_DIM)
    pe = jnp.broadcast_to(per.transpose(0, 2, 3, 1, 4)[:, :, :, :, None, :], (L, 2, half, 2, KV_GROUPS, HEAD_DIM))
    pe = jnp.pad(pe.reshape(L, 2, half * cw), ((0, 0), (0, SUBLANES - 2), (0, 0)))
    w2 = jnp.einsum('ashd,st,gj->asghtjd', jnp.stack([w2k, w2v], 1), eye_s, eye_g)
    w1 = w1.reshape(L, half * cw, 2 * 2 * KV_GROUPS * CMP_HIDDEN)
    w2 = w2.reshape(L, 2 * KV_GROUPS * CMP_HIDDEN, 2 * KV_GROUPS * HEAD_DIM)
    return w1.astype(BF16), pe.astype(BF16), w2.astype(BF16)


def _overlap(nc):
    i0 = jnp.arange(nc)[:, None] * CMP_STRIDE
    j0 = jnp.arange(LANES)[None, :] * SEL_BLOCK
    return ((i0 < j0 + SEL_BLOCK) & (i0 + CMP_BLOCK > j0)).astype(BF16)


def _prep_ssm(lam_re, lam_im, log_dt, b_re, b_im, c_re, c_im):
    L = lam_re.shape[0]
    dt = jnp.exp(log_dt)[..., None]
    mag = jnp.exp(lam_re * dt)
    ar, ai = mag * jnp.cos(lam_im * dt), mag * jnp.sin(lam_im * dt)
    den = lam_re * lam_re + lam_im * lam_im
    cr = ((ar - 1.0) * lam_re + ai * lam_im) / den
    ci = (ai * lam_re - (ar - 1.0) * lam_im) / den
    bbr = cr[..., None] * b_re - ci[..., None] * b_im
    bbi = cr[..., None] * b_im + ci[..., None] * b_re
    nb = SSM_BLOCKS
    gb = SSM_GROUPS // nb
    half = SSM_BLK_W // 2
    eye = jnp.eye(gb, dtype=F32)
    bt = jnp.concatenate([jnp.einsum('lkgph,gj->lkghjp', m.reshape(L, nb, gb, SSM_STATE, SSM_GROUP), eye)
                          .reshape(L, nb, LANES, half) for m in (bbr, bbi)], -1)
    cm = jnp.concatenate([jnp.einsum('lkghp,gj->lkgpjh', m.reshape(L, nb, gb, SSM_GROUP, SSM_STATE), eye)
                          .reshape(L, nb, half, LANES) for m in (c_re, -c_im)], 2)
    a = jnp.concatenate([ar.reshape(L, nb, half), ai.reshape(L, nb, half)], -1).reshape(L, 1, 2 * SSM_NS)
    a = jnp.broadcast_to(a, (L, SUBLANES, 2 * SSM_NS))
    return bt.astype(BF16), a, cm.astype(BF16)


def _prep_router(w_grp, b_grp, w_exp, b_exp):
    L, D, _ = w_grp.shape
    wr = jnp.zeros((L, D, LANES), F32).at[:, :, 0:MOE_GROUPS].set(w_grp)
    wr = wr.at[:, :, EXP_LANE0:EXP_LANE0 + MOE_EXPERTS].set(w_exp)
    br = jnp.zeros((L, 1, LANES), F32).at[:, 0, 0:MOE_GROUPS].set(b_grp)
    br = br.at[:, 0, EXP_LANE0:EXP_LANE0 + MOE_EXPERTS].set(b_exp)
    hi = wr.astype(BF16)
    lo = (wr - hi.astype(F32)).astype(BF16)
    return jnp.concatenate([hi, lo], -1), br


def _layer(B, S, consts, expert_w, carry, w):
    x, xb, xs = carry
    T, D = x.shape
    nc = S // CMP_STRIDE
    xb_in = xb
    qw, kv, ng, u_pool = _proj(xb, w["w_a"], ((HEADS * LANES, BF16), (6 * LANES, BF16), (LANES, F32), (D // 2, F32)), 512)
    u_ssm = _proj_time_major(xb, w["w_c"], B, S, 512)
    kr = kv[:, 0:2 * LANES].reshape(T // CMP_STRIDE, CMP_STRIDE * 2 * LANES)
    kvc = _compress(kr, w["cw1"], w["cpe"], w["cw2"], B, nc)
    qaux1, qaux2, kaux, caux, ex = consts
    acc, sel, un = _nsa1(qw, kvc, ng, w["ov"], qaux1, caux, B, S)
    lst, cnt = _chunk_lists(un, B, S)
    o_nsa = _nsa2(lst, cnt, qw, kv, sel, ng, acc, qaux2, kaux, ex, B, S)
    o_pool = _pool(u_pool, w["pool_w"], w["pool_scale"], B, S, 512)
    o_ssm = _ssm(u_ssm, w["bt"], w["a"], w["cm"], w["d"], w["w_glu"], w["b_glu"], 128)
    x, xb = _merge(o_nsa, o_pool, o_ssm, xb_in, x, w["w_bg"], w["wn"], w["wp"], w["ws"], w["wo"],
                   w["ln1_g"], w["ln1_b"], B, S, 256)
    tm = 256
    ri, rw, cnt = _router(x, w["wr"], w["br"], 512)
    counts = cnt[0, 0:MOE_EXPERTS].astype(jnp.int32)
    padded = (counts + MOE_ROWS - 1) // MOE_ROWS * MOE_ROWS
    pend = jnp.cumsum(padded)
    pstart = pend - padded
    nblk = (2 * T) // MOE_ROWS + MOE_EXPERTS
    eids = jnp.arange(MOE_EXPERTS, dtype=jnp.int32)
    dest = jnp.sum(jnp.where(ri[:, 0:2, None] == eids, pstart, 0), -1) + ri[:, 2:4]
    dest3 = dest.reshape(T // tm, 1, 2 * tm)
    blk_e = jnp.minimum(jnp.sum(pend[None, :] <= jnp.arange(nblk, dtype=jnp.int32)[:, None] * MOE_ROWS, -1),
                        MOE_EXPERTS - 1).astype(jnp.int32)
    nused = (pend[-1:] // MOE_ROWS).astype(jnp.int32)
    xs = _dispatch(dest3, x, xs, tm)
    ys = _experts(blk_e + w["layer"] * MOE_EXPERTS, nused, xs, *expert_w)
    x, xb = _combine(dest3, ys, x, rw, w["ln2_g"], w["ln2_b"], tm)
    return (x, xb, xs), None


def kernel(x, w_in, cmp_pe_k, cmp_pe_v, cmp_w1_k, cmp_w2_k, cmp_w1_v, cmp_w2_v, pool_w, pool_scale, ssm_lam_re, ssm_lam_im, ssm_log_dt, ssm_b_re, ssm_b_im, ssm_c_re, ssm_c_im, ssm_d, ssm_w_glu, ssm_b_glu, w_up_nsa, w_up_pool, w_up_ssm, w_out, ln1_g, ln1_b, router_w_grp, router_b_grp, router_w_exp, router_b_exp, moe_w_gate, moe_w_up, moe_w_down, ln2_g, ln2_b):
    B, S, D = x.shape
    L = w_in.shape[0]
    assert B == SSM_BATCH and D == D_MODEL and S % SEL_CHUNK == 0 and S // SEL_BLOCK <= LANES
    assert Q_TILE == LANES and Q2_TILE % Q_TILE == 0 and SEL_CHUNK % Q2_TILE == 0 and WINDOW % Q2_TILE == 0
    w_a, w_bg, w_c = _prep_in_proj(w_in)
    cw1, cpe, cw2 = _prep_compress(cmp_pe_k, cmp_pe_v, cmp_w1_k, cmp_w2_k, cmp_w1_v, cmp_w2_v)
    bt, a, cm = _prep_ssm(ssm_lam_re, ssm_lam_im, ssm_log_dt, ssm_b_re, ssm_b_im, ssm_c_re, ssm_c_im)
    wr, br = _prep_router(router_w_grp, router_b_grp, router_w_exp, router_b_exp)
    wn = w_up_nsa.reshape(L, KV_GROUPS, GQA, HEAD_DIM, D).transpose(0, 2, 1, 3, 4).reshape(L, HEADS * HEAD_DIM, D)
    row = lambda v: v.reshape(L, 1, -1)
    ws = dict(
        w_a=w_a, w_bg=w_bg, w_c=w_c, cw1=cw1, cpe=cpe, cw2=cw2,
        ov=jnp.broadcast_to(_overlap(S // CMP_STRIDE), (L, S // CMP_STRIDE, LANES)),
        pool_w=pool_w.astype(BF16), pool_scale=row(pool_scale),
        bt=bt, a=a, cm=cm, d=row(ssm_d), w_glu=ssm_w_glu.astype(BF16), b_glu=row(ssm_b_glu),
        wn=wn.astype(BF16), wp=w_up_pool.astype(BF16), ws=w_up_ssm.astype(BF16), wo=w_out.astype(BF16),
        ln1_g=row(ln1_g), ln1_b=row(ln1_b), wr=wr, br=br,
        layer=jnp.arange(L, dtype=jnp.int32),
        ln2_g=row(ln2_g), ln2_b=row(ln2_b))
    xf = x.reshape(B * S, D)
    expert_w = tuple(m.reshape((L * MOE_EXPERTS,) + m.shape[2:]) for m in (moe_w_gate, moe_w_up, moe_w_down))
    xs0 = jnp.zeros((2 * B * S + MOE_EXPERTS * MOE_ROWS, D), F32)
    (xf, _, _), _ = lax.scan(functools.partial(_layer, B, S, _nsa_consts(S), expert_w),
                             (xf, xf.astype(BF16), xs0), ws)
    return xf.reshape(B, S, D)
```

```python
import functools
import math

import jax
import jax.numpy as jnp
from jax import lax
from jax.experimental import pallas as pl
from jax.experimental.pallas import tpu as pltpu

F32 = jnp.float32
BF16 = jnp.bfloat16

D_MODEL = 1024
DEPTH = 4
HEAD_DIM = 64
HEADS = 8
KV_GROUPS = 2
GQA = HEADS // KV_GROUPS
CMP_BLOCK = 32
CMP_STRIDE = 16
CMP_HIDDEN = 128
SEL_BLOCK = 64
SEL_TOPN = 16
WINDOW = 512
FORCE_SCORE = 1.0e4
NEG_INF = -1.0e30
POOL_WINDOWS = (2, 4, 8, 16)
POOL_HALO = 16
SSM_GROUP = 16
SSM_GROUPS = 32
SSM_STATE = 64
SSM_NS = SSM_GROUPS * SSM_STATE
MOE_GROUPS = 4
MOE_EPG = 8
MOE_EXPERTS = 32
MOE_FF = 512
DN_ALPHA = (2 * DEPTH) ** 0.25
LN_EPS = 1e-5

LANES = 128
SUBLANES = 8
VMEM_LIMIT = 48 * 1024 * 1024
Q_TILE = 128
Q2_TILE = 128
SEL_CHUNK = 512
MOE_ROWS = 512
EXP_LANE0 = 32
DMA_UNROLL = 4
SEL_SHIFT = SEL_BLOCK.bit_length() - 1
EPG_SHIFT = MOE_EPG.bit_length() - 1


def _cparams(*sem):
    return pltpu.CompilerParams(dimension_semantics=sem, vmem_limit_bytes=VMEM_LIMIT)


def _gelu(x):
    return 0.5 * x * (1.0 + jnp.tanh(math.sqrt(2.0 / math.pi) * (x + 0.044715 * x * x * x)))


def _sigmoid(x):
    return 1.0 / (1.0 + jnp.exp(-x))


def _dot(a, b):
    return jnp.dot(a, b, preferred_element_type=F32)


def _dot_nt(a, b):
    return lax.dot_general(a, b, (((1,), (1,)), ((), ())), preferred_element_type=F32)


def _layer_norm(h, g, b):
    mu = jnp.mean(h, -1, keepdims=True)
    c = h - mu
    var = jnp.mean(c * c, -1, keepdims=True)
    return c * lax.rsqrt(var + LN_EPS) * g + b


def _proj_kernel(x_ref, w_ref, *out_refs, offs):
    x = x_ref[...]
    for o_ref, (a, n) in zip(out_refs, offs):
        o_ref[...] = _dot(x, w_ref[:, a:a + n]).astype(o_ref.dtype)


def _proj(xb, w, outs, tm):
    T, K = xb.shape
    N = w.shape[1]
    offs, a = [], 0
    for n, _ in outs:
        offs.append((a, n))
        a += n
    assert a == N and T % tm == 0
    return pl.pallas_call(
        functools.partial(_proj_kernel, offs=tuple(offs)),
        grid=(T // tm,),
        in_specs=[pl.BlockSpec((tm, K), lambda i: (i, 0)), pl.BlockSpec((K, N), lambda i: (0, 0))],
        out_specs=[pl.BlockSpec((tm, n), lambda i: (i, 0)) for n, _ in outs],
        out_shape=[jax.ShapeDtypeStruct((T, n), dt) for n, dt in outs],
        compiler_params=_cparams("parallel"),
    )(xb, w)


def _proj_tm_kernel(x_ref, w_ref, o_ref):
    o_ref[...] = _dot(x_ref[...], w_ref[...])


def _proj_time_major(xb, w, B, S, tm):
    K = xb.shape[1]
    N = w.shape[1]
    nt = S // tm
    return pl.pallas_call(
        _proj_tm_kernel,
        grid=(B, nt),
        in_specs=[pl.BlockSpec((tm, K), lambda b, i: (b * nt + i, 0)), pl.BlockSpec((K, N), lambda b, i: (0, 0))],
        out_specs=pl.BlockSpec((tm, N), lambda b, i: (i, b)),
        out_shape=jax.ShapeDtypeStruct((S, B * N), F32),
        compiler_params=_cparams("parallel", "parallel"),
    )(xb, w)


def _compress_kernel(kr_ref, w1_ref, pe_ref, w2_ref, out_ref):
    nc = kr_ref.shape[0]
    hw = 4 * CMP_HIDDEN
    a = _dot(kr_ref[...], w1_ref[...])
    pb = _dot(pe_ref[...], w1_ref[...])
    second = pltpu.roll(a[:, hw:], nc - 1, 0)
    row = lax.broadcasted_iota(jnp.int32, (nc, hw), 0)
    second = jnp.where(row < nc - 1, second, 0.0)
    hid = _gelu(a[:, :hw] + second + pb[0:1, :hw] + pb[1:2, hw:])
    out_ref[0] = _dot(hid.astype(BF16), w2_ref[...]).astype(out_ref.dtype)


def _compress(kr, w1, pe_in, w2, B, nc):
    return pl.pallas_call(
        _compress_kernel,
        grid=(B,),
        in_specs=[pl.BlockSpec((nc, kr.shape[1]), lambda b: (b, 0)),
                  pl.BlockSpec(w1.shape, lambda b: (0, 0)),
                  pl.BlockSpec(pe_in.shape, lambda b: (0, 0)),
                  pl.BlockSpec(w2.shape, lambda b: (0, 0))],
        out_specs=pl.BlockSpec((1, nc, 2 * LANES), lambda b: (b, 0, 0)),
        out_shape=jax.ShapeDtypeStruct((B, nc, 2 * LANES), BF16),
        compiler_params=_cparams("parallel"),
    )(kr, w1, pe_in, w2)


def _slope(h):
    return 2.0 ** (-8.0 * (h + 1) / HEADS)


def _stack_heads(q, g):
    return jnp.concatenate([q[:, (g * GQA + r) * LANES:(g * GQA + r + 1) * LANES] for r in range(GQA)], axis=0)


def _split3_dot(a, b):
    hi = a.astype(BF16)
    r1 = a - hi.astype(F32)
    mid = r1.astype(BF16)
    lo = (r1 - mid.astype(F32)).astype(BF16)
    return _dot(hi, b) + _dot(mid, b) + _dot(lo, b)


def _nsa1_kernel(q_ref, kvc_ref, ng_ref, ov_ref, qaux_ref, caux_ref, acc_ref, sel_ref, un_ref, sc_ref, oc_sc):
    Q = q_ref.shape[0]
    nc = kvc_ref.shape[1]
    qb = pl.program_id(1)
    t0 = qb * Q
    lane = lax.broadcasted_iota(jnp.int32, (Q, LANES), 1)
    tb = jnp.right_shift(t0 + lax.broadcasted_iota(jnp.int32, (Q, LANES), 0), SEL_SHIFT)
    jrow = lax.broadcasted_iota(jnp.int32, (LANES, KV_GROUPS * Q), 0).astype(F32)

    def attend(n):
        q = q_ref[...]
        kc = jnp.concatenate([kvc_ref[0, 0:n, 0:LANES], caux_ref[0:n, :]], axis=1)
        vc = kvc_ref[0, 0:n, LANES:2 * LANES]
        qrow = lax.broadcasted_iota(jnp.int32, (Q, n), 0)
        ccol = lax.broadcasted_iota(jnp.int32, (Q, n), 1)
        bias = jnp.where((t0 + qrow) >= (ccol * CMP_STRIDE + (CMP_BLOCK - 1)), 0.0, NEG_INF)
        bias4 = jnp.concatenate([bias] * GQA, axis=0)
        for g in range(KV_GROUPS):
            qs = jnp.concatenate([_stack_heads(q, g), qaux_ref[g]], axis=1)
            s = _dot_nt(qs, kc) + bias4
            m = jnp.max(s, -1, keepdims=True)
            e = jnp.exp(s - m)
            inv = jnp.where(m > 0.5 * NEG_INF, 1.0 / jnp.maximum(jnp.sum(e, -1, keepdims=True), 1e-30), 0.0)
            p = e * inv
            oc_sc[g] = _dot(p.astype(BF16), vc)
            psum = p[0:Q]
            for r in range(1, GQA):
                psum = psum + p[r * Q:(r + 1) * Q]
            imp = _split3_dot(psum, ov_ref[0:n, :])
            forced = (lane == 0) | (lane == tb) | (lane == tb - 1)
            score = jnp.where(forced, FORCE_SCORE, jnp.where(lane <= tb, imp, -1.0))
            sc_ref[:, g * Q:(g + 1) * Q] = score.T

    step = min(LANES, nc)
    need = (t0 + Q - CMP_BLOCK) // CMP_STRIDE + 1
    variant = (need + step - 1) // step
    for v in range(1, nc // step + 1):
        pl.when(variant == v)(functools.partial(attend, v * step))
    gate = _sigmoid(ng_ref[...])
    outs = [[gate[:, 3 * (g * GQA + r):3 * (g * GQA + r) + 1] * oc_sc[g, r * Q:(r + 1) * Q, :] for r in range(GQA)]
            for g in range(KV_GROUPS)]
    work = sc_ref[...]
    taken = jnp.zeros(work.shape, F32)
    for _ in range(SEL_TOPN):
        best = jnp.max(work, 0, keepdims=True)
        first = jnp.min(jnp.where(work == best, jrow, float(LANES)), 0, keepdims=True)
        hit = jrow == first
        taken = jnp.where(hit, 1.0, taken)
        work = jnp.where(hit, -2.0, work)
    for g in range(KV_GROUPS):
        sel = taken[:, g * Q:(g + 1) * Q].T
        sel = jnp.where(lane <= tb, sel, 0.0)
        sel_ref[:, g * LANES:(g + 1) * LANES] = sel.astype(sel_ref.dtype)
        un_ref[:, g * LANES:(g + 1) * LANES] = jnp.broadcast_to(jnp.max(sel, 0, keepdims=True), (SUBLANES, LANES))
    for r in range(GQA):
        acc_ref[:, r * LANES:(r + 1) * LANES] = jnp.where(lane < HEAD_DIM, outs[0][r], outs[1][r])


def _nsa1(qw, kvc, ng, ov, qaux, caux, B, S):
    nq = S // Q_TILE
    nc = kvc.shape[1]
    T = B * S
    rows = lambda b, i: (b * nq + i, 0)
    return pl.pallas_call(
        _nsa1_kernel,
        grid=(B, nq),
        in_specs=[pl.BlockSpec((Q_TILE, HEADS * LANES), rows),
                  pl.BlockSpec((1, nc, 2 * LANES), lambda b, i: (b, 0, 0)),
                  pl.BlockSpec((Q_TILE, LANES), rows),
                  pl.BlockSpec((nc, LANES), lambda b, i: (0, 0)),
                  pl.BlockSpec(qaux.shape, lambda b, i: (0, 0, 0)),
                  pl.BlockSpec((nc, LANES), lambda b, i: (0, 0))],
        out_specs=[pl.BlockSpec((Q_TILE, GQA * LANES), rows),
                   pl.BlockSpec((Q_TILE, KV_GROUPS * LANES), rows),
                   pl.BlockSpec((SUBLANES, KV_GROUPS * LANES), rows)],
        out_shape=[jax.ShapeDtypeStruct((T, GQA * LANES), F32),
                   jax.ShapeDtypeStruct((T, KV_GROUPS * LANES), BF16),
                   jax.ShapeDtypeStruct((B * nq * SUBLANES, KV_GROUPS * LANES), F32)],
        scratch_shapes=[pltpu.VMEM((LANES, KV_GROUPS * Q_TILE), F32),
                        pltpu.VMEM((KV_GROUPS, GQA * Q_TILE, LANES), F32)],
        compiler_params=_cparams("parallel", "parallel"),
    )(qw, kvc, ng, ov, qaux, caux)


def _nsa2_kernel(lst_ref, cnt_ref, q_ref, ks_ref, vs_ref, kw_ref, vw_ref, sel_ref, ng_ref, acc_ref,
                 qaux_ref, kaux_ref, ex_ref, o_ref, m_sc, l_sc, a_sc, ow_sc, s0_sc, s1_sc, sd_sc):
    Q = q_ref.shape[0]
    C = SEL_CHUNK
    WK = WINDOW + Q
    maxch = ex_ref.shape[1] // C
    tile = pl.program_id(0) * pl.num_programs(1) + pl.program_id(1)
    t0 = pl.program_id(1) * Q
    q = q_ref[...]
    gate = _sigmoid(ng_ref[...])
    lane = lax.broadcasted_iota(jnp.int32, (Q, LANES), 1)
    cdiag = t0 // C
    start = pl.multiple_of(jnp.maximum(t0 - WINDOW, 0), Q)
    wrow = lax.broadcasted_iota(jnp.int32, (Q, WK), 0)
    wcol = lax.broadcasted_iota(jnp.int32, (Q, WK), 1)
    wdist = (t0 + wrow) - (start + wcol)
    wbias = jnp.where((wdist >= 0) & (wdist < WINDOW), 0.0, NEG_INF)
    wbias4 = jnp.concatenate([wbias] * GQA, axis=0)
    qrow = lax.broadcasted_iota(jnp.int32, (Q, C), 0)
    kcol = lax.broadcasted_iota(jnp.int32, (Q, C), 1)
    outs = []
    for g in range(KV_GROUPS):
        qs = jnp.concatenate([_stack_heads(q, g), qaux_ref[g]], axis=1)
        selg = sel_ref[:, g * LANES:(g + 1) * LANES]
        far = cdiag > 0
        selg = jnp.where(lane == 0, selg * jnp.where(far, 0.0, 1.0).astype(selg.dtype), selg)
        k0 = jnp.concatenate([ks_ref[0:SEL_BLOCK, :], kaux_ref[0:SEL_BLOCK, :]], axis=1)
        s0 = _dot_nt(qs, k0) + jnp.where(far, 0.0, NEG_INF)
        m0 = jnp.max(s0, -1, keepdims=True)
        e0 = jnp.exp(s0 - m0)
        m_sc[...] = jnp.broadcast_to(m0, m_sc.shape)
        l_sc[...] = jnp.broadcast_to(jnp.sum(e0, -1, keepdims=True), l_sc.shape)
        a_sc[...] = _dot(e0.astype(BF16), vs_ref[0:SEL_BLOCK, :])

        def scores(c, s_out, qs=qs, selg=selg):
            k0 = pl.multiple_of(c * C, C)
            kk = jnp.concatenate([ks_ref[pl.ds(k0, C), :], kaux_ref[pl.ds(k0, C), :]], axis=1)
            valid = (_dot(selg, ex_ref[:, pl.ds(k0, C)]) > 0.5) & ((t0 + qrow) >= (k0 + kcol))
            bias = jnp.where(valid, 0.0, NEG_INF)
            s_out[...] = _dot_nt(qs, kk) + jnp.concatenate([bias] * GQA, axis=0)

        def accumulate(c, s_in):
            k0 = pl.multiple_of(c * C, C)
            s = s_in[...]
            m_old = m_sc[...]
            m_new = jnp.maximum(m_old, jnp.max(s, -1, keepdims=True))
            alpha = jnp.exp(m_old - m_new)
            e = jnp.exp(s - m_new[:, 0:1])
            l_sc[...] = alpha * l_sc[...] + jnp.sum(e, -1, keepdims=True)
            a_sc[...] = alpha * a_sc[...] + _dot(e.astype(BF16), vs_ref[pl.ds(k0, C), :])
            m_sc[...] = m_new

        base = (tile * KV_GROUPS + g) * maxch
        scores(cdiag, sd_sc)
        scores(lst_ref[base], s0_sc)
        kw = jnp.concatenate([kw_ref[pl.ds(start, WK), :], kaux_ref[pl.ds(start, WK), :]], axis=1)
        sw = _dot_nt(qs, kw) + wbias4
        e = jnp.exp(sw - jnp.max(sw, -1, keepdims=True))
        ow_sc[...] = _dot(e.astype(BF16), vw_ref[pl.ds(start, WK), :]) / jnp.maximum(jnp.sum(e, -1, keepdims=True), 1e-30)

        def step(j, carry, scores=scores, accumulate=accumulate, base=base):
            cur = lst_ref[base + j]
            nxt = lst_ref[base + j + 1]

            @pl.when(j % 2 == 0)
            def _():
                scores(nxt, s1_sc)
                accumulate(cur, s0_sc)

            @pl.when(j % 2 == 1)
            def _():
                scores(nxt, s0_sc)
                accumulate(cur, s1_sc)

            return carry

        lax.fori_loop(0, cnt_ref[tile * KV_GROUPS + g], step, 0)
        accumulate(cdiag, sd_sc)
        o_w = ow_sc[...]
        o_s = a_sc[...] / jnp.maximum(l_sc[...], 1e-30)
        og = []
        for r in range(GQA):
            h = g * GQA + r
            rows = slice(r * Q, (r + 1) * Q)
            og.append(gate[:, 3 * h + 1:3 * h + 2] * o_s[rows] + gate[:, 3 * h + 2:3 * h + 3] * o_w[rows])
        outs.append(og)
    for r in range(GQA):
        cols = slice(r * LANES, (r + 1) * LANES)
        o_ref[:, cols] = (acc_ref[:, cols] + jnp.where(lane < HEAD_DIM, outs[0][r], outs[1][r])).astype(o_ref.dtype)


def _chunk_lists(un, B, S):
    nq = S // Q2_TILE
    maxch = S // SEL_CHUNK
    per = SEL_CHUNK // SEL_BLOCK
    un = un.reshape(B * nq, Q2_TILE // Q_TILE, SUBLANES, KV_GROUPS, LANES)[:, :, 0, :, :maxch * per].max(1)
    un = jnp.where(jnp.arange(maxch * per) == 0, 0.0, un)
    flags = un.reshape(B * nq, KV_GROUPS, maxch, per).max(-1) > 0.5
    cdiag = (jnp.arange(B * nq, dtype=jnp.int32) % nq) * Q2_TILE // SEL_CHUNK
    flags = flags & (jnp.arange(maxch, dtype=jnp.int32)[None, None, :] < cdiag[:, None, None])
    pos = jnp.cumsum(flags.astype(jnp.int32), -1)
    hit = flags[..., None, :] & (pos[..., None, :] == jnp.arange(1, maxch + 1, dtype=jnp.int32)[:, None])
    lst = jnp.sum(jnp.where(hit, jnp.arange(maxch, dtype=jnp.int32), 0), -1)
    return lst.astype(jnp.int32).reshape(-1), pos[..., -1].reshape(-1)


def _nsa2(lst, cnt, qw, kv, sel, ng, acc, qaux, kaux, ex, B, S):
    Q = Q2_TILE
    nq = S // Q
    T = B * S
    rows = lambda b, i, *_: (b * nq + i, 0)
    kvspec = lambda col: pl.BlockSpec((S, LANES), lambda b, i, *_, col=col: (b, col))
    const = lambda shp: pl.BlockSpec(shp, lambda b, i, *_: tuple(0 for _ in shp))
    return pl.pallas_call(
        _nsa2_kernel,
        grid_spec=pltpu.PrefetchScalarGridSpec(
            num_scalar_prefetch=2, grid=(B, nq),
            in_specs=[pl.BlockSpec((Q, HEADS * LANES), rows),
                      kvspec(2), kvspec(3), kvspec(4), kvspec(5),
                      pl.BlockSpec((Q, KV_GROUPS * LANES), rows),
                      pl.BlockSpec((Q, LANES), rows),
                      pl.BlockSpec((Q, GQA * LANES), rows),
                      const(qaux.shape), const(kaux.shape), const(ex.shape)],
            out_specs=pl.BlockSpec((Q, GQA * LANES), rows),
            scratch_shapes=[pltpu.VMEM((GQA * Q, LANES), F32)] * 4
            + [pltpu.VMEM((GQA * Q, SEL_CHUNK), F32)] * 3),
        out_shape=jax.ShapeDtypeStruct((T, GQA * LANES), BF16),
        compiler_params=_cparams("parallel", "parallel"),
    )(lst, cnt, qw, kv, kv, kv, kv, sel, ng, acc, qaux, kaux, ex)


def _nsa_consts(S):
    h = jnp.arange(HEADS, dtype=F32).reshape(KV_GROUPS, GQA, 1, 1)
    slope = jnp.exp2(-8.0 * (h + 1.0) / HEADS)
    lane = jnp.arange(LANES)[None, None, None, :]
    qaux = jnp.where(lane == 0, slope * SEL_BLOCK, jnp.where(lane == 1, slope, 0.0))
    qaux1, qaux2 = (jnp.broadcast_to(qaux, (KV_GROUPS, GQA, n, LANES)).reshape(KV_GROUPS, GQA * n, LANES).astype(BF16)
                    for n in (Q_TILE, Q2_TILE))

    def pos_cols(pos):
        l2 = jnp.arange(LANES)[None, :]
        return jnp.where(l2 == 0, pos[:, None] // SEL_BLOCK, jnp.where(l2 == 1, pos[:, None] % SEL_BLOCK, 0))

    kaux = pos_cols(jnp.arange(S))
    caux = pos_cols(jnp.arange(S // CMP_STRIDE) * CMP_STRIDE + (CMP_BLOCK - 1))
    ex = (jnp.arange(LANES)[:, None] == (jnp.arange(S)[None, :] // SEL_BLOCK))
    return qaux1, qaux2, kaux.astype(BF16), caux.astype(BF16), ex.astype(BF16)


def _pool_kernel(u_ref, halo_ref, w_ref, sc_ref, o_ref, x_sc):
    ts = u_ref.shape[0]
    i = pl.program_id(1)
    x_sc[0:POOL_HALO] = jnp.where(i > 0, halo_ref[...], 0.0)
    x_sc[POOL_HALO:POOL_HALO + ts] = u_ref[...]
    t = i * ts + lax.broadcasted_iota(jnp.int32, (ts, LANES), 0)
    for g, w in enumerate(POOL_WINDOWS):
        cols = slice(g * LANES, (g + 1) * LANES)
        cur = x_sc[POOL_HALO:POOL_HALO + ts, cols]
        acc = cur
        for k in range(1, w):
            acc = acc + x_sc[POOL_HALO - k:POOL_HALO - k + ts, cols]
        cnt = jnp.minimum(t + 1, w).astype(F32)
        pooled = acc / cnt - cur
        o_ref[:, cols] = (_dot(pooled.astype(BF16), w_ref[g]) * sc_ref[:, cols]).astype(o_ref.dtype)


def _pool(u, pool_w, pool_scale, B, S, ts):
    nt = S // ts
    hb = ts // POOL_HALO
    C = u.shape[1]
    return pl.pallas_call(
        _pool_kernel,
        grid=(B, nt),
        in_specs=[pl.BlockSpec((ts, C), lambda b, i: (b * nt + i, 0)),
                  pl.BlockSpec((POOL_HALO, C), lambda b, i: (jnp.maximum((b * nt + i) * hb - 1, 0), 0)),
                  pl.BlockSpec(pool_w.shape, lambda b, i: (0, 0, 0)),
                  pl.BlockSpec((1, C), lambda b, i: (0, 0))],
        out_specs=pl.BlockSpec((ts, C), lambda b, i: (b * nt + i, 0)),
        out_shape=jax.ShapeDtypeStruct((B * S, C), BF16),
        scratch_shapes=[pltpu.VMEM((POOL_HALO + ts, C), F32)],
        compiler_params=_cparams("parallel", "parallel"),
    )(u, u, pool_w, pool_scale)


SSM_BATCH = 4
SSM_PASS = 8
SSM_BLOCKS = SSM_GROUPS * SSM_GROUP // LANES
SSM_BLK_W = 2 * SSM_NS // SSM_BLOCKS


def _ssm_kernel(u_ref, bt_ref, a_ref, c_ref, d_ref, wg_ref, bg_ref, o_ref, xh, hst, io_sc):
    steps = u_ref.shape[0]
    C = u_ref.shape[1] // SSM_BATCH
    rows = steps * SSM_BATCH

    @pl.when(pl.program_id(0) == 0)
    def _():
        hst[...] = jnp.zeros(hst.shape, F32)

    nslab = C // LANES
    for b in range(SSM_BATCH):
        for j in range(nslab):
            io_sc[j, pl.ds(b, steps, stride=SSM_BATCH), :] = u_ref[:, b * C + j * LANES:b * C + (j + 1) * LANES]
    u = jnp.concatenate([io_sc[j] for j in range(nslab)], axis=1)
    ub = u.astype(BF16)
    for k in range(SSM_BLOCKS):
        xh[:, k * SSM_BLK_W:(k + 1) * SSM_BLK_W] = _dot(ub[:, k * LANES:(k + 1) * LANES], bt_ref[k])
    low = lax.broadcasted_iota(jnp.int32, (SUBLANES, LANES), 0) < 4
    nchunk = SSM_NS // LANES
    per_blk = nchunk // SSM_BLOCKS
    for p in range(nchunk // SSM_PASS):
        cols = [p * SSM_PASS + j for j in range(SSM_PASS)]
        start = [(c // per_blk) * SSM_BLK_W + (c % per_blk) * LANES for c in cols]
        re = [slice(s, s + LANES) for s in start]
        im = [slice(s + SSM_BLK_W // 2, s + SSM_BLK_W // 2 + LANES) for s in start]
        ar = [a_ref[:, s] for s in re]
        ai = [a_ref[:, s] for s in im]

        def tile(k, carry, re=re, im=im, ar=ar, ai=ai):
            cr, ci = carry
            r0 = pl.multiple_of(k * SUBLANES, SUBLANES)
            ncr, nci = [], []
            for j in range(SSM_PASS):
                vr = xh[pl.ds(r0, SUBLANES), re[j]]
                vi = xh[pl.ds(r0, SUBLANES), im[j]]
                t1r = ar[j] * cr[j] - ai[j] * ci[j] + vr
                t1i = ar[j] * ci[j] + ai[j] * cr[j] + vi
                pr = pltpu.roll(t1r, 4, 0)
                pi = pltpu.roll(t1i, 4, 0)
                t2r = ar[j] * pr - ai[j] * pi + vr
                t2i = ar[j] * pi + ai[j] * pr + vi
                xh[pl.ds(r0, SUBLANES), re[j]] = jnp.where(low, t1r, t2r)
                xh[pl.ds(r0, SUBLANES), im[j]] = jnp.where(low, t1i, t2i)
                ncr.append(pltpu.roll(t2r, 4, 0))
                nci.append(pltpu.roll(t2i, 4, 0))
            return tuple(ncr), tuple(nci)

        cr0 = tuple(hst[:, s] for s in re)
        ci0 = tuple(hst[:, s] for s in im)
        cr, ci = lax.fori_loop(0, rows // SUBLANES, tile, (cr0, ci0))
        for j in range(SSM_PASS):
            hst[:, re[j]] = cr[j]
            hst[:, im[j]] = ci[j]
    y = jnp.concatenate([_dot(xh[:, k * SSM_BLK_W:(k + 1) * SSM_BLK_W].astype(BF16), c_ref[k])
                         for k in range(SSM_BLOCKS)], axis=1) + d_ref[...] * u
    z = _gelu(y)
    out = z * _sigmoid(_dot(z.astype(BF16), wg_ref[...]) + bg_ref[...])
    for j in range(nslab):
        io_sc[j] = out[:, j * LANES:(j + 1) * LANES]
    for b in range(SSM_BATCH):
        for j in range(nslab):
            o_ref[:, b * C + j * LANES:b * C + (j + 1) * LANES] = (
                io_sc[j, pl.ds(b, steps, stride=SSM_BATCH), :].astype(o_ref.dtype))


def _ssm(u, bt, a_b, cmat, d_skip, w_glu, b_glu, steps):
    S, W = u.shape
    rows = steps * SSM_BATCH
    const = lambda shp: pl.BlockSpec(shp, lambda i: tuple(0 for _ in shp))
    return pl.pallas_call(
        _ssm_kernel,
        grid=(S // steps,),
        in_specs=[pl.BlockSpec((steps, W), lambda i: (i, 0)),
                  const(bt.shape), const(a_b.shape), const(cmat.shape), const(d_skip.shape),
                  const(w_glu.shape), const(b_glu.shape)],
        out_specs=pl.BlockSpec((steps, W), lambda i: (i, 0)),
        out_shape=jax.ShapeDtypeStruct((S, W), BF16),
        scratch_shapes=[pltpu.VMEM((rows, 2 * SSM_NS), F32), pltpu.VMEM((SUBLANES, 2 * SSM_NS), F32),
                        pltpu.VMEM((W // SSM_BATCH // LANES, rows, LANES), F32)],
        compiler_params=_cparams("arbitrary"),
    )(u, bt, a_b, cmat, d_skip, w_glu, b_glu)


def _merge_kernel(on_ref, op_ref, os_ref, xin_ref, x_ref, wbg_ref, wn_ref, wp_ref, ws_ref, wo_ref, g_ref, b_ref,
                  xo_ref, xb_ref):
    D = x_ref.shape[1]
    xin = xin_ref[...]
    m = None
    for k, (o_ref, w_ref) in enumerate(((on_ref, wn_ref), (op_ref, wp_ref), (os_ref, ws_ref))):
        gate = _sigmoid(_dot(xin, wbg_ref[:, k * D:(k + 1) * D]))
        term = gate * _dot(o_ref[...], w_ref[...])
        m = term if m is None else m + term
    h = DN_ALPHA * x_ref[...] + _dot(m.astype(BF16), wo_ref[...])
    y = _layer_norm(h, g_ref[...], b_ref[...])
    xo_ref[...] = y
    xb_ref[...] = y.astype(BF16)


def _merge(o_nsa, o_pool, o_ssm_tm, xb, x, wbg, wn, wp, ws, wo, g, b, B, S, tm):
    nt = S // tm
    T, D = x.shape
    W = o_nsa.shape[1]
    rows = lambda bb, i: (bb * nt + i, 0)
    const = lambda shp: pl.BlockSpec(shp, lambda bb, i: tuple(0 for _ in shp))
    return pl.pallas_call(
        _merge_kernel,
        grid=(B, nt),
        in_specs=[pl.BlockSpec((tm, W), rows), pl.BlockSpec((tm, W), rows),
                  pl.BlockSpec((tm, W), lambda bb, i: (i, bb)),
                  pl.BlockSpec((tm, D), rows), pl.BlockSpec((tm, D), rows),
                  const(wbg.shape), const(wn.shape), const(wp.shape), const(ws.shape), const(wo.shape),
                  const(g.shape), const(b.shape)],
        out_specs=[pl.BlockSpec((tm, D), rows), pl.BlockSpec((tm, D), rows)],
        out_shape=[jax.ShapeDtypeStruct((T, D), F32), jax.ShapeDtypeStruct((T, D), BF16)],
        compiler_params=_cparams("parallel", "parallel"),
    )(o_nsa, o_pool, o_ssm_tm, xb, x, wbg, wn, wp, ws, wo, g, b)


def _router_kernel(x_ref, wr_ref, br_ref, ri_ref, rw_ref, cnt_ref, carry):
    tm = x_ref.shape[0]

    @pl.when(pl.program_id(0) == 0)
    def _():
        carry[...] = jnp.zeros(carry.shape, F32)

    x = x_ref[...]
    xh = x.astype(BF16)
    xl = (x - xh.astype(F32)).astype(BF16)
    both = _dot(xh, wr_ref[...])
    logits = both[:, 0:LANES] + both[:, LANES:2 * LANES] + _dot(xl, wr_ref[:, 0:LANES]) + br_ref[...]
    lane = lax.broadcasted_iota(jnp.int32, (tm, LANES), 1)
    big = jnp.int32(LANES)
    isg = lane < MOE_GROUPS
    lg = jnp.where(isg, logits, NEG_INF)
    eg = jnp.where(isg, jnp.exp(lg - jnp.max(lg, -1, keepdims=True)), 0.0)
    pgv = jnp.where(isg, eg / jnp.sum(eg, -1, keepdims=True), -1.0)
    p_g = jnp.max(pgv, -1, keepdims=True)
    g_sel = jnp.min(jnp.where(pgv == p_g, lane, big), -1, keepdims=True)
    ine = ((lane >= EXP_LANE0) & (lane < EXP_LANE0 + MOE_EXPERTS)
           & (jnp.right_shift(lane - EXP_LANE0, EPG_SHIFT) == g_sel))
    le = jnp.where(ine, logits, NEG_INF)
    ee = jnp.where(ine, jnp.exp(le - jnp.max(le, -1, keepdims=True)), 0.0)
    p = jnp.where(ine, ee / jnp.sum(ee, -1, keepdims=True), -1.0)
    p1 = jnp.max(p, -1, keepdims=True)
    i1 = jnp.min(jnp.where(p == p1, lane, big), -1, keepdims=True)
    prest = jnp.where(lane == i1, -1.0, p)
    p2 = jnp.max(prest, -1, keepdims=True)
    i2 = jnp.min(jnp.where(prest == p2, lane, big), -1, keepdims=True)
    den = p1 + p2
    w0 = p1 / den * p_g
    w1 = p2 / den * p_g
    e0 = i1 - EXP_LANE0
    e1 = i2 - EXP_LANE0
    is0 = lane == e0
    is1 = lane == e1
    onehot = jnp.where(is0 | is1, 1.0, 0.0)
    r = lax.broadcasted_iota(jnp.int32, (tm, tm), 0)
    c = lax.broadcasted_iota(jnp.int32, (tm, tm), 1)
    tri = jnp.where(r > c, 1.0, 0.0).astype(BF16)
    cum = _dot(tri, onehot.astype(BF16)) + carry[0:1, :]
    rank0 = jnp.sum(jnp.where(is0, cum, 0.0), -1, keepdims=True).astype(jnp.int32)
    rank1 = jnp.sum(jnp.where(is1, cum, 0.0), -1, keepdims=True).astype(jnp.int32)
    carry[...] = carry[...] + jnp.sum(onehot, 0, keepdims=True)
    cnt_ref[...] = carry[...]
    zi = jnp.zeros((tm, LANES), jnp.int32)
    ri_ref[...] = jnp.where(lane == 0, e0, jnp.where(lane == 1, e1, jnp.where(lane == 2, rank0, jnp.where(lane == 3, rank1, zi))))
    rw_ref[...] = jnp.where(lane == 0, w0, jnp.where(lane == 1, w1, 0.0))


def _router(x, wr, br, tm):
    T, D = x.shape
    return pl.pallas_call(
        _router_kernel,
        grid=(T // tm,),
        in_specs=[pl.BlockSpec((tm, D), lambda i: (i, 0)),
                  pl.BlockSpec(wr.shape, lambda i: (0, 0)), pl.BlockSpec(br.shape, lambda i: (0, 0))],
        out_specs=[pl.BlockSpec((tm, LANES), lambda i: (i, 0)), pl.BlockSpec((tm, LANES), lambda i: (i, 0)),
                   pl.BlockSpec((SUBLANES, LANES), lambda i: (0, 0))],
        out_shape=[jax.ShapeDtypeStruct((T, LANES), jnp.int32), jax.ShapeDtypeStruct((T, LANES), F32),
                   jax.ShapeDtypeStruct((SUBLANES, LANES), F32)],
        scratch_shapes=[pltpu.VMEM((SUBLANES, LANES), F32)],
        compiler_params=_cparams("arbitrary"),
    )(x, wr, br)


def _row_copy(src, s, dst, d, sem):
    return pltpu.make_async_copy(src.at[pl.ds(s, 1)], dst.at[pl.ds(d, 1)], sem)


def _dispatch_kernel(dest_ref, x_ref, xs_in, xs_hbm, sem, *, tm):
    del xs_in

    def issue(r, c):
        for k in range(2):
            _row_copy(x_ref, r, xs_hbm, dest_ref[0, 0, 2 * r + k], sem).start(priority=k)
        return c

    lax.fori_loop(0, tm, issue, 0, unroll=DMA_UNROLL)
    pltpu.make_async_copy(xs_hbm.at[pl.ds(0, 2 * tm)], xs_hbm.at[pl.ds(0, 2 * tm)], sem).wait()


def _dispatch(dest3, x, xs0, tm):
    T = x.shape[0]
    return pl.pallas_call(
        functools.partial(_dispatch_kernel, tm=tm),
        grid=(T // tm,),
        in_specs=[pl.BlockSpec((1, 1, 2 * tm), lambda i: (i, 0, 0), memory_space=pltpu.SMEM),
                  pl.BlockSpec((tm, x.shape[1]), lambda i: (i, 0)), pl.BlockSpec(memory_space=pl.ANY)],
        out_specs=pl.BlockSpec(memory_space=pl.ANY),
        out_shape=jax.ShapeDtypeStruct(xs0.shape, xs0.dtype),
        scratch_shapes=[pltpu.SemaphoreType.DMA(())],
        input_output_aliases={2: 0},
        compiler_params=_cparams("arbitrary"),
    )(dest3, x, xs0)


def _expert_kernel(be_ref, nu_ref, xs_ref, wg_ref, wu_ref, wd_ref, ys_ref):
    i = pl.program_id(0)

    @pl.when(i < nu_ref[0])
    def _():
        x = xs_ref[...].astype(BF16)
        a = _dot(x, wg_ref[0].astype(BF16))
        h = a * _sigmoid(a) * _dot(x, wu_ref[0].astype(BF16))
        ys_ref[...] = _dot(h.astype(BF16), wd_ref[0].astype(BF16))

    @pl.when(i >= nu_ref[0])
    def _():
        ys_ref[...] = jnp.zeros(ys_ref.shape, F32)


def _experts(blk_e, nused, xs, wg, wu, wd):
    P, D = xs.shape
    F = wg.shape[2]
    nblk = P // MOE_ROWS
    return pl.pallas_call(
        _expert_kernel,
        grid_spec=pltpu.PrefetchScalarGridSpec(
            num_scalar_prefetch=2, grid=(nblk,),
            in_specs=[pl.BlockSpec((MOE_ROWS, D), lambda i, be, nu: (i, 0)),
                      pl.BlockSpec((1, D, F), lambda i, be, nu: (be[i], 0, 0)),
                      pl.BlockSpec((1, D, F), lambda i, be, nu: (be[i], 0, 0)),
                      pl.BlockSpec((1, F, D), lambda i, be, nu: (be[i], 0, 0))],
            out_specs=pl.BlockSpec((MOE_ROWS, D), lambda i, be, nu: (i, 0))),
        out_shape=jax.ShapeDtypeStruct((P, D), F32),
        compiler_params=_cparams("arbitrary"),
    )(blk_e, nused, xs, wg, wu, wd)


def _combine_kernel(dest_ref, ys_hbm, x_ref, rw_ref, g_ref, b_ref, xo_ref, xb_ref, buf, sem, *, tm):
    def issue(r, c):
        for k in range(2):
            _row_copy(ys_hbm, dest_ref[0, 0, 2 * r + k], buf.at[k], r, sem).start(priority=k)
        return c

    lax.fori_loop(0, tm, issue, 0, unroll=DMA_UNROLL)
    for k in range(2):
        pltpu.make_async_copy(ys_hbm.at[pl.ds(0, tm)], buf.at[k], sem).wait()
    w = rw_ref[...]
    h = DN_ALPHA * x_ref[...] + w[:, 0:1] * buf[0] + w[:, 1:2] * buf[1]
    y = _layer_norm(h, g_ref[...], b_ref[...])
    xo_ref[...] = y
    xb_ref[...] = y.astype(BF16)


def _combine(dest3, ys, x, rw, g, b, tm):
    T, D = x.shape
    return pl.pallas_call(
        functools.partial(_combine_kernel, tm=tm),
        grid=(T // tm,),
        in_specs=[pl.BlockSpec((1, 1, 2 * tm), lambda i: (i, 0, 0), memory_space=pltpu.SMEM),
                  pl.BlockSpec(memory_space=pl.ANY),
                  pl.BlockSpec((tm, D), lambda i: (i, 0)), pl.BlockSpec((tm, LANES), lambda i: (i, 0)),
                  pl.BlockSpec(g.shape, lambda i: (0, 0)), pl.BlockSpec(b.shape, lambda i: (0, 0))],
        out_specs=[pl.BlockSpec((tm, D), lambda i: (i, 0)), pl.BlockSpec((tm, D), lambda i: (i, 0))],
        out_shape=[jax.ShapeDtypeStruct((T, D), F32), jax.ShapeDtypeStruct((T, D), BF16)],
        scratch_shapes=[pltpu.VMEM((2, tm, D), F32), pltpu.SemaphoreType.DMA(())],
        compiler_params=_cparams("arbitrary"),
    )(dest3, ys, x, rw, g, b)


def _prep_in_proj(w_in):
    L, D, _ = w_in.shape
    w_in = w_in.astype(BF16)
    nw = HEADS * HEAD_DIM
    kvw = 6 * KV_GROUPS * HEAD_DIM
    o = 0
    wq = w_in[..., o:o + nw]; o += nw
    wkv = w_in[..., o:o + kvw]; o += kvw
    wng = w_in[..., o:o + 3 * HEADS]; o += 3 * HEADS
    wpool = w_in[..., o:o + D // 2]; o += D // 2
    wssm = w_in[..., o:o + D // 2]; o += D // 2
    wbg = w_in[..., o:]
    wq = wq.reshape(L, D, KV_GROUPS, GQA, HEAD_DIM) * (HEAD_DIM ** -0.5)
    z = jnp.zeros_like(wq[:, :, 0])
    wq_wide = jnp.stack([jnp.concatenate([wq[:, :, 0], z], -1), jnp.concatenate([z, wq[:, :, 1]], -1)], 2)
    wq_wide = wq_wide.reshape(L, D, HEADS * LANES)
    wng = jnp.pad(wng, ((0, 0), (0, 0), (0, LANES - 3 * HEADS)))
    w_a = jnp.concatenate([wq_wide, wkv, wng, wpool], -1).astype(BF16)
    return w_a, wbg.astype(BF16), wssm.astype(BF16)


def _prep_compress(pe_k, pe_v, w1k, w2k, w1v, w2v):
    L = pe_k.shape[0]
    half = CMP_BLOCK // 2
    cw = 2 * KV_GROUPS * HEAD_DIM
    eye_s = jnp.eye(2, dtype=F32)
    eye_g = jnp.eye(KV_GROUPS, dtype=F32)
    w1r = jnp.stack([w1k, w1v], 1).reshape(L, 2, 2, half, HEAD_DIM, CMP_HIDDEN)
    w1 = jnp.einsum('aspldh,st,gj->alsgdptjh', w1r, eye_s, eye_g)
    per = jnp.stack([pe_k, pe_v], 1).reshape(L, 2, 2, half, HEAD_DIM)
    pe = jnp.broadcast_to(per.transpose(0, 2, 3, 1, 4)[:, :, :, :, None, :], (L, 2, half, 2, KV_GROUPS, HEAD_DIM))
    pe = jnp.pad(pe.reshape(L, 2, half * cw), ((0, 0), (0, SUBLANES - 2), (0, 0)))
    w2 = jnp.einsum('ashd,st,gj->asghtjd', jnp.stack([w2k, w2v], 1), eye_s, eye_g)
    w1 = w1.reshape(L, half * cw, 2 * 2 * KV_GROUPS * CMP_HIDDEN)
    w2 = w2.reshape(L, 2 * KV_GROUPS * CMP_HIDDEN, 2 * KV_GROUPS * HEAD_DIM)
    return w1.astype(BF16), pe.astype(BF16), w2.astype(BF16)


def _overlap(nc):
    i0 = jnp.arange(nc)[:, None] * CMP_STRIDE
    j0 = jnp.arange(LANES)[None, :] * SEL_BLOCK
    return ((i0 < j0 + SEL_BLOCK) & (i0 + CMP_BLOCK > j0)).astype(BF16)


def _prep_ssm(lam_re, lam_im, log_dt, b_re, b_im, c_re, c_im):
    L = lam_re.shape[0]
    dt = jnp.exp(log_dt)[..., None]
    mag = jnp.exp(lam_re * dt)
    ar, ai = mag * jnp.cos(lam_im * dt), mag * jnp.sin(lam_im * dt)
    den = lam_re * lam_re + lam_im * lam_im
    cr = ((ar - 1.0) * lam_re + ai * lam_im) / den
    ci = (ai * lam_re - (ar - 1.0) * lam_im) / den
    bbr = cr[..., None] * b_re - ci[..., None] * b_im
    bbi = cr[..., None] * b_im + ci[..., None] * b_re
    nb = SSM_BLOCKS
    gb = SSM_GROUPS // nb
    half = SSM_BLK_W // 2
    eye = jnp.eye(gb, dtype=F32)
    bt = jnp.concatenate([jnp.einsum('lkgph,gj->lkghjp', m.reshape(L, nb, gb, SSM_STATE, SSM_GROUP), eye)
                          .reshape(L, nb, LANES, half) for m in (bbr, bbi)], -1)
    cm = jnp.concatenate([jnp.einsum('lkghp,gj->lkgpjh', m.reshape(L, nb, gb, SSM_GROUP, SSM_STATE), eye)
                          .reshape(L, nb, half, LANES) for m in (c_re, -c_im)], 2)
    a = jnp.concatenate([ar.reshape(L, nb, half), ai.reshape(L, nb, half)], -1).reshape(L, 1, 2 * SSM_NS)
    a = jnp.broadcast_to(a, (L, SUBLANES, 2 * SSM_NS))
    return bt.astype(BF16), a, cm.astype(BF16)


def _prep_router(w_grp, b_grp, w_exp, b_exp):
    L, D, _ = w_grp.shape
    wr = jnp.zeros((L, D, LANES), F32).at[:, :, 0:MOE_GROUPS].set(w_grp)
    wr = wr.at[:, :, EXP_LANE0:EXP_LANE0 + MOE_EXPERTS].set(w_exp)
    br = jnp.zeros((L, 1, LANES), F32).at[:, 0, 0:MOE_GROUPS].set(b_grp)
    br = br.at[:, 0, EXP_LANE0:EXP_LANE0 + MOE_EXPERTS].set(b_exp)
    hi = wr.astype(BF16)
    lo = (wr - hi.astype(F32)).astype(BF16)
    return jnp.concatenate([hi, lo], -1), br


def _layer(B, S, consts, expert_w, carry, w):
    x, xb, xs = carry
    T, D = x.shape
    nc = S // CMP_STRIDE
    xb_in = xb
    qw, kv, ng, u_pool = _proj(xb, w["w_a"], ((HEADS * LANES, BF16), (6 * LANES, BF16), (LANES, F32), (D // 2, F32)), 512)
    u_ssm = _proj_time_major(xb, w["w_c"], B, S, 512)
    kr = kv[:, 0:2 * LANES].reshape(T // CMP_STRIDE, CMP_STRIDE * 2 * LANES)
    kvc = _compress(kr, w["cw1"], w["cpe"], w["cw2"], B, nc)
    qaux1, qaux2, kaux, caux, ex = consts
    acc, sel, un = _nsa1(qw, kvc, ng, w["ov"], qaux1, caux, B, S)
    lst, cnt = _chunk_lists(un, B, S)
    o_nsa = _nsa2(lst, cnt, qw, kv, sel, ng, acc, qaux2, kaux, ex, B, S)
    o_pool = _pool(u_pool, w["pool_w"], w["pool_scale"], B, S, 512)
    o_ssm = _ssm(u_ssm, w["bt"], w["a"], w["cm"], w["d"], w["w_glu"], w["b_glu"], 128)
    x, xb = _merge(o_nsa, o_pool, o_ssm, xb_in, x, w["w_bg"], w["wn"], w["wp"], w["ws"], w["wo"],
                   w["ln1_g"], w["ln1_b"], B, S, 512)
    tm = 512
    ri, rw, cnt = _router(x, w["wr"], w["br"], 512)
    counts = cnt[0, 0:MOE_EXPERTS].astype(jnp.int32)
    padded = (counts + MOE_ROWS - 1) // MOE_ROWS * MOE_ROWS
    pend = jnp.cumsum(padded)
    pstart = pend - padded
    nblk = (2 * T) // MOE_ROWS + MOE_EXPERTS
    eids = jnp.arange(MOE_EXPERTS, dtype=jnp.int32)
    dest = jnp.sum(jnp.where(ri[:, 0:2, None] == eids, pstart, 0), -1) + ri[:, 2:4]
    dest3 = dest.reshape(T // tm, 1, 2 * tm)
    blk_e = jnp.minimum(jnp.sum(pend[None, :] <= jnp.arange(nblk, dtype=jnp.int32)[:, None] * MOE_ROWS, -1),
                        MOE_EXPERTS - 1).astype(jnp.int32)
    nused = (pend[-1:] // MOE_ROWS).astype(jnp.int32)
    xs = _dispatch(dest3, x, xs, tm)
    ys = _experts(blk_e + w["layer"] * MOE_EXPERTS, nused, xs, *expert_w)
    x, xb = _combine(dest3, ys, x, rw, w["ln2_g"], w["ln2_b"], tm)
    return (x, xb, xs), None


def kernel(x, w_in, cmp_pe_k, cmp_pe_v, cmp_w1_k, cmp_w2_k, cmp_w1_v, cmp_w2_v, pool_w, pool_scale, ssm_lam_re, ssm_lam_im, ssm_log_dt, ssm_b_re, ssm_b_im, ssm_c_re, ssm_c_im, ssm_d, ssm_w_glu, ssm_b_glu, w_up_nsa, w_up_pool, w_up_ssm, w_out, ln1_g, ln1_b, router_w_grp, router_b_grp, router_w_exp, router_b_exp, moe_w_gate, moe_w_up, moe_w_down, ln2_g, ln2_b):
    B, S, D = x.shape
    L = w_in.shape[0]
    assert B == SSM_BATCH and D == D_MODEL and S % SEL_CHUNK == 0 and S // SEL_BLOCK <= LANES
    assert Q_TILE == LANES and Q2_TILE % Q_TILE == 0 and SEL_CHUNK % Q2_TILE == 0 and WINDOW % Q2_TILE == 0
    w_a, w_bg, w_c = _prep_in_proj(w_in)
    cw1, cpe, cw2 = _prep_compress(cmp_pe_k, cmp_pe_v, cmp_w1_k, cmp_w2_k, cmp_w1_v, cmp_w2_v)
    bt, a, cm = _prep_ssm(ssm_lam_re, ssm_lam_im, ssm_log_dt, ssm_b_re, ssm_b_im, ssm_c_re, ssm_c_im)
    wr, br = _prep_router(router_w_grp, router_b_grp, router_w_exp, router_b_exp)
    wn = w_up_nsa.reshape(L, KV_GROUPS, GQA, HEAD_DIM, D).transpose(0, 2, 1, 3, 4).reshape(L, HEADS * HEAD_DIM, D)
    row = lambda v: v.reshape(L, 1, -1)
    ws = dict(
        w_a=w_a, w_bg=w_bg, w_c=w_c, cw1=cw1, cpe=cpe, cw2=cw2,
        ov=jnp.broadcast_to(_overlap(S // CMP_STRIDE), (L, S // CMP_STRIDE, LANES)),
        pool_w=pool_w.astype(BF16), pool_scale=row(pool_scale),
        bt=bt, a=a, cm=cm, d=row(ssm_d), w_glu=ssm_w_glu.astype(BF16), b_glu=row(ssm_b_glu),
        wn=wn.astype(BF16), wp=w_up_pool.astype(BF16), ws=w_up_ssm.astype(BF16), wo=w_out.astype(BF16),
        ln1_g=row(ln1_g), ln1_b=row(ln1_b), wr=wr, br=br,
        layer=jnp.arange(L, dtype=jnp.int32),
        ln2_g=row(ln2_g), ln2_b=row(ln2_b))
    xf = x.reshape(B * S, D)
    expert_w = tuple(m.reshape((L * MOE_EXPERTS,) + m.shape[2:]) for m in (moe_w_gate, moe_w_up, moe_w_down))
    xs0 = jnp.zeros((2 * B * S + MOE_EXPERTS * MOE_ROWS, D), F32)
    (xf, _, _), _ = lax.scan(functools.partial(_layer, B, S, _nsa_consts(S), expert_w),
                             (xf, xf.astype(BF16), xs0), ws)
    return xf.reshape(B, S, D)
```

```python
import functools
import math

import jax
import jax.numpy as jnp
from jax import lax
from jax.experimental import pallas as pl
from jax.experimental.pallas import tpu as pltpu

F32 = jnp.float32
BF16 = jnp.bfloat16

D_MODEL = 1024
DEPTH = 4
HEAD_DIM = 64
HEADS = 8
KV_GROUPS = 2
GQA = HEADS // KV_GROUPS
CMP_BLOCK = 32
CMP_STRIDE = 16
CMP_HIDDEN = 128
SEL_BLOCK = 64
SEL_TOPN = 16
WINDOW = 512
FORCE_SCORE = 1.0e4
NEG_INF = -1.0e30
POOL_WINDOWS = (2, 4, 8, 16)
POOL_HALO = 16
SSM_GROUP = 16
SSM_GROUPS = 32
SSM_STATE = 64
SSM_NS = SSM_GROUPS * SSM_STATE
MOE_GROUPS = 4
MOE_EPG = 8
MOE_EXPERTS = 32
MOE_FF = 512
DN_ALPHA = (2 * DEPTH) ** 0.25
LN_EPS = 1e-5

LANES = 128
SUBLANES = 8
VMEM_LIMIT = 48 * 1024 * 1024
Q_TILE = 128
Q2_TILE = 128
SEL_CHUNK = 512
MOE_ROWS = 512
EXP_LANE0 = 32
DMA_UNROLL = 4
SEL_SHIFT = SEL_BLOCK.bit_length() - 1
EPG_SHIFT = MOE_EPG.bit_length() - 1


def _cparams(*sem):
    return pltpu.CompilerParams(dimension_semantics=sem, vmem_limit_bytes=VMEM_LIMIT)


def _gelu(x):
    return 0.5 * x * (1.0 + jnp.tanh(math.sqrt(2.0 / math.pi) * (x + 0.044715 * x * x * x)))


def _sigmoid(x):
    return 1.0 / (1.0 + jnp.exp(-x))


def _dot(a, b):
    return jnp.dot(a, b, preferred_element_type=F32)


def _dot_nt(a, b):
    return lax.dot_general(a, b, (((1,), (1,)), ((), ())), preferred_element_type=F32)


def _layer_norm(h, g, b):
    mu = jnp.mean(h, -1, keepdims=True)
    c = h - mu
    var = jnp.mean(c * c, -1, keepdims=True)
    return c * lax.rsqrt(var + LN_EPS) * g + b


def _proj_kernel(x_ref, w_ref, *out_refs, offs):
    x = x_ref[...]
    for o_ref, (a, n) in zip(out_refs, offs):
        o_ref[...] = _dot(x, w_ref[:, a:a + n]).astype(o_ref.dtype)


def _proj(xb, w, outs, tm):
    T, K = xb.shape
    N = w.shape[1]
    offs, a = [], 0
    for n, _ in outs:
        offs.append((a, n))
        a += n
    assert a == N and T % tm == 0
    return pl.pallas_call(
        functools.partial(_proj_kernel, offs=tuple(offs)),
        grid=(T // tm,),
        in_specs=[pl.BlockSpec((tm, K), lambda i: (i, 0)), pl.BlockSpec((K, N), lambda i: (0, 0))],
        out_specs=[pl.BlockSpec((tm, n), lambda i: (i, 0)) for n, _ in outs],
        out_shape=[jax.ShapeDtypeStruct((T, n), dt) for n, dt in outs],
        compiler_params=_cparams("parallel"),
    )(xb, w)


def _proj_tm_kernel(x_ref, w_ref, o_ref):
    o_ref[...] = _dot(x_ref[...], w_ref[...])


def _proj_time_major(xb, w, B, S, tm):
    K = xb.shape[1]
    N = w.shape[1]
    nt = S // tm
    return pl.pallas_call(
        _proj_tm_kernel,
        grid=(B, nt),
        in_specs=[pl.BlockSpec((tm, K), lambda b, i: (b * nt + i, 0)), pl.BlockSpec((K, N), lambda b, i: (0, 0))],
        out_specs=pl.BlockSpec((tm, N), lambda b, i: (i, b)),
        out_shape=jax.ShapeDtypeStruct((S, B * N), F32),
        compiler_params=_cparams("parallel", "parallel"),
    )(xb, w)


def _compress_kernel(kr_ref, w1_ref, pe_ref, w2_ref, out_ref):
    nc = kr_ref.shape[0]
    hw = 4 * CMP_HIDDEN
    a = _dot(kr_ref[...], w1_ref[...])
    pb = _dot(pe_ref[...], w1_ref[...])
    second = pltpu.roll(a[:, hw:], nc - 1, 0)
    row = lax.broadcasted_iota(jnp.int32, (nc, hw), 0)
    second = jnp.where(row < nc - 1, second, 0.0)
    hid = _gelu(a[:, :hw] + second + pb[0:1, :hw] + pb[1:2, hw:])
    out_ref[0] = _dot(hid.astype(BF16), w2_ref[...]).astype(out_ref.dtype)


def _compress(kr, w1, pe_in, w2, B, nc):
    return pl.pallas_call(
        _compress_kernel,
        grid=(B,),
        in_specs=[pl.BlockSpec((nc, kr.shape[1]), lambda b: (b, 0)),
                  pl.BlockSpec(w1.shape, lambda b: (0, 0)),
                  pl.BlockSpec(pe_in.shape, lambda b: (0, 0)),
                  pl.BlockSpec(w2.shape, lambda b: (0, 0))],
        out_specs=pl.BlockSpec((1, nc, 2 * LANES), lambda b: (b, 0, 0)),
        out_shape=jax.ShapeDtypeStruct((B, nc, 2 * LANES), BF16),
        compiler_params=_cparams("parallel"),
    )(kr, w1, pe_in, w2)


def _slope(h):
    return 2.0 ** (-8.0 * (h + 1) / HEADS)


def _stack_heads(q, g):
    return jnp.concatenate([q[:, (g * GQA + r) * LANES:(g * GQA + r + 1) * LANES] for r in range(GQA)], axis=0)


def _split3_dot(a, b):
    hi = a.astype(BF16)
    r1 = a - hi.astype(F32)
    mid = r1.astype(BF16)
    lo = (r1 - mid.astype(F32)).astype(BF16)
    return _dot(hi, b) + _dot(mid, b) + _dot(lo, b)


def _nsa1_kernel(q_ref, kvc_ref, ng_ref, ov_ref, qaux_ref, caux_ref, acc_ref, sel_ref, un_ref, sc_ref, oc_sc):
    Q = q_ref.shape[0]
    nc = kvc_ref.shape[1]
    qb = pl.program_id(1)
    t0 = qb * Q
    lane = lax.broadcasted_iota(jnp.int32, (Q, LANES), 1)
    tb = jnp.right_shift(t0 + lax.broadcasted_iota(jnp.int32, (Q, LANES), 0), SEL_SHIFT)
    jrow = lax.broadcasted_iota(jnp.int32, (LANES, KV_GROUPS * Q), 0).astype(F32)

    def attend(n):
        q = q_ref[...]
        kc = jnp.concatenate([kvc_ref[0, 0:n, 0:LANES], caux_ref[0:n, :]], axis=1)
        vc = kvc_ref[0, 0:n, LANES:2 * LANES]
        qrow = lax.broadcasted_iota(jnp.int32, (Q, n), 0)
        ccol = lax.broadcasted_iota(jnp.int32, (Q, n), 1)
        bias = jnp.where((t0 + qrow) >= (ccol * CMP_STRIDE + (CMP_BLOCK - 1)), 0.0, NEG_INF)
        bias4 = jnp.concatenate([bias] * GQA, axis=0)
        for g in range(KV_GROUPS):
            qs = jnp.concatenate([_stack_heads(q, g), qaux_ref[g]], axis=1)
            s = _dot_nt(qs, kc) + bias4
            m = jnp.max(s, -1, keepdims=True)
            e = jnp.exp(s - m)
            inv = jnp.where(m > 0.5 * NEG_INF, 1.0 / jnp.maximum(jnp.sum(e, -1, keepdims=True), 1e-30), 0.0)
            p = e * inv
            oc_sc[g] = _dot(p.astype(BF16), vc)
            psum = p[0:Q]
            for r in range(1, GQA):
                psum = psum + p[r * Q:(r + 1) * Q]
            imp = _split3_dot(psum, ov_ref[0:n, :])
            forced = (lane == 0) | (lane == tb) | (lane == tb - 1)
            score = jnp.where(forced, FORCE_SCORE, jnp.where(lane <= tb, imp, -1.0))
            sc_ref[:, g * Q:(g + 1) * Q] = score.T

    step = min(LANES, nc)
    need = (t0 + Q - CMP_BLOCK) // CMP_STRIDE + 1
    variant = (need + step - 1) // step
    for v in range(1, nc // step + 1):
        pl.when(variant == v)(functools.partial(attend, v * step))
    gate = _sigmoid(ng_ref[...])
    outs = [[gate[:, 3 * (g * GQA + r):3 * (g * GQA + r) + 1] * oc_sc[g, r * Q:(r + 1) * Q, :] for r in range(GQA)]
            for g in range(KV_GROUPS)]
    work = sc_ref[...]
    taken = jnp.zeros(work.shape, F32)
    for _ in range(SEL_TOPN):
        best = jnp.max(work, 0, keepdims=True)
        first = jnp.min(jnp.where(work == best, jrow, float(LANES)), 0, keepdims=True)
        hit = jrow == first
        taken = jnp.where(hit, 1.0, taken)
        work = jnp.where(hit, -2.0, work)
    for g in range(KV_GROUPS):
        sel = taken[:, g * Q:(g + 1) * Q].T
        sel = jnp.where(lane <= tb, sel, 0.0)
        sel_ref[:, g * LANES:(g + 1) * LANES] = sel.astype(sel_ref.dtype)
        un_ref[:, g * LANES:(g + 1) * LANES] = jnp.broadcast_to(jnp.max(sel, 0, keepdims=True), (SUBLANES, LANES))
    for r in range(GQA):
        acc_ref[:, r * LANES:(r + 1) * LANES] = jnp.where(lane < HEAD_DIM, outs[0][r], outs[1][r])


def _nsa1(qw, kvc, ng, ov, qaux, caux, B, S):
    nq = S // Q_TILE
    nc = kvc.shape[1]
    T = B * S
    rows = lambda b, i: (b * nq + i, 0)
    return pl.pallas_call(
        _nsa1_kernel,
        grid=(B, nq),
        in_specs=[pl.BlockSpec((Q_TILE, HEADS * LANES), rows),
                  pl.BlockSpec((1, nc, 2 * LANES), lambda b, i: (b, 0, 0)),
                  pl.BlockSpec((Q_TILE, LANES), rows),
                  pl.BlockSpec((nc, LANES), lambda b, i: (0, 0)),
                  pl.BlockSpec(qaux.shape, lambda b, i: (0, 0, 0)),
                  pl.BlockSpec((nc, LANES), lambda b, i: (0, 0))],
        out_specs=[pl.BlockSpec((Q_TILE, GQA * LANES), rows),
                   pl.BlockSpec((Q_TILE, KV_GROUPS * LANES), rows),
                   pl.BlockSpec((SUBLANES, KV_GROUPS * LANES), rows)],
        out_shape=[jax.ShapeDtypeStruct((T, GQA * LANES), F32),
                   jax.ShapeDtypeStruct((T, KV_GROUPS * LANES), BF16),
                   jax.ShapeDtypeStruct((B * nq * SUBLANES, KV_GROUPS * LANES), F32)],
        scratch_shapes=[pltpu.VMEM((LANES, KV_GROUPS * Q_TILE), F32),
                        pltpu.VMEM((KV_GROUPS, GQA * Q_TILE, LANES), F32)],
        compiler_params=_cparams("parallel", "parallel"),
    )(qw, kvc, ng, ov, qaux, caux)


def _nsa2_kernel(lst_ref, cnt_ref, q_ref, ks_ref, vs_ref, kw_ref, vw_ref, sel_ref, ng_ref, acc_ref,
                 qaux_ref, kaux_ref, ex_ref, o_ref, m_sc, l_sc, a_sc, ow_sc, s0_sc, s1_sc, sd_sc):
    Q = q_ref.shape[0]
    C = SEL_CHUNK
    WK = WINDOW + Q
    maxch = ex_ref.shape[1] // C
    tile = pl.program_id(0) * pl.num_programs(1) + pl.program_id(1)
    t0 = pl.program_id(1) * Q
    q = q_ref[...]
    gate = _sigmoid(ng_ref[...])
    lane = lax.broadcasted_iota(jnp.int32, (Q, LANES), 1)
    cdiag = t0 // C
    start = pl.multiple_of(jnp.maximum(t0 - WINDOW, 0), Q)
    wrow = lax.broadcasted_iota(jnp.int32, (Q, WK), 0)
    wcol = lax.broadcasted_iota(jnp.int32, (Q, WK), 1)
    wdist = (t0 + wrow) - (start + wcol)
    wbias = jnp.where((wdist >= 0) & (wdist < WINDOW), 0.0, NEG_INF)
    wbias4 = jnp.concatenate([wbias] * GQA, axis=0)
    qrow = lax.broadcasted_iota(jnp.int32, (Q, C), 0)
    kcol = lax.broadcasted_iota(jnp.int32, (Q, C), 1)
    outs = []
    for g in range(KV_GROUPS):
        qs = jnp.concatenate([_stack_heads(q, g), qaux_ref[g]], axis=1)
        selg = sel_ref[:, g * LANES:(g + 1) * LANES]
        far = cdiag > 0
        selg = jnp.where(lane == 0, selg * jnp.where(far, 0.0, 1.0).astype(selg.dtype), selg)
        k0 = jnp.concatenate([ks_ref[0:SEL_BLOCK, :], kaux_ref[0:SEL_BLOCK, :]], axis=1)
        s0 = _dot_nt(qs, k0) + jnp.where(far, 0.0, NEG_INF)
        m0 = jnp.max(s0, -1, keepdims=True)
        e0 = jnp.exp(s0 - m0)
        m_sc[...] = jnp.broadcast_to(m0, m_sc.shape)
        l_sc[...] = jnp.broadcast_to(jnp.sum(e0, -1, keepdims=True), l_sc.shape)
        a_sc[...] = _dot(e0.astype(BF16), vs_ref[0:SEL_BLOCK, :])

        def scores(c, s_out, qs=qs, selg=selg):
            k0 = pl.multiple_of(c * C, C)
            kk = jnp.concatenate([ks_ref[pl.ds(k0, C), :], kaux_ref[pl.ds(k0, C), :]], axis=1)
            valid = (_dot(selg, ex_ref[:, pl.ds(k0, C)]) > 0.5) & ((t0 + qrow) >= (k0 + kcol))
            bias = jnp.where(valid, 0.0, NEG_INF)
            s_out[...] = _dot_nt(qs, kk) + jnp.concatenate([bias] * GQA, axis=0)

        def accumulate(c, s_in):
            k0 = pl.multiple_of(c * C, C)
            es, alphas = [], []
            for r in range(GQA):
                rows = slice(r * Q, (r + 1) * Q)
                s = s_in[rows, :]
                m_old = m_sc[rows]
                m_new = jnp.maximum(m_old, jnp.max(s, -1, keepdims=True))
                alpha = jnp.exp(m_old - m_new)
                e = jnp.exp(s - m_new[:, 0:1])
                l_sc[rows] = alpha * l_sc[rows] + jnp.sum(e, -1, keepdims=True)
                m_sc[rows] = m_new
                es.append(e.astype(BF16))
                alphas.append(alpha)
            pv = _dot(jnp.concatenate(es, axis=0), vs_ref[pl.ds(k0, C), :])
            a_sc[...] = jnp.concatenate(alphas, axis=0) * a_sc[...] + pv

        base = (tile * KV_GROUPS + g) * maxch
        scores(cdiag, sd_sc)
        scores(lst_ref[base], s0_sc)
        kw = jnp.concatenate([kw_ref[pl.ds(start, WK), :], kaux_ref[pl.ds(start, WK), :]], axis=1)
        sw = _dot_nt(qs, kw) + wbias4
        e = jnp.exp(sw - jnp.max(sw, -1, keepdims=True))
        ow_sc[...] = _dot(e.astype(BF16), vw_ref[pl.ds(start, WK), :]) / jnp.maximum(jnp.sum(e, -1, keepdims=True), 1e-30)

        def step(j, carry, scores=scores, accumulate=accumulate, base=base):
            cur = lst_ref[base + j]
            nxt = lst_ref[base + j + 1]

            @pl.when(j % 2 == 0)
            def _():
                scores(nxt, s1_sc)
                accumulate(cur, s0_sc)

            @pl.when(j % 2 == 1)
            def _():
                scores(nxt, s0_sc)
                accumulate(cur, s1_sc)

            return carry

        lax.fori_loop(0, cnt_ref[tile * KV_GROUPS + g], step, 0)
        accumulate(cdiag, sd_sc)
        o_w = ow_sc[...]
        o_s = a_sc[...] / jnp.maximum(l_sc[...], 1e-30)
        og = []
        for r in range(GQA):
            h = g * GQA + r
            rows = slice(r * Q, (r + 1) * Q)
            og.append(gate[:, 3 * h + 1:3 * h + 2] * o_s[rows] + gate[:, 3 * h + 2:3 * h + 3] * o_w[rows])
        outs.append(og)
    for r in range(GQA):
        cols = slice(r * LANES, (r + 1) * LANES)
        o_ref[:, cols] = (acc_ref[:, cols] + jnp.where(lane < HEAD_DIM, outs[0][r], outs[1][r])).astype(o_ref.dtype)


def _chunk_lists(un, B, S):
    nq = S // Q2_TILE
    maxch = S // SEL_CHUNK
    per = SEL_CHUNK // SEL_BLOCK
    un = un.reshape(B * nq, Q2_TILE // Q_TILE, SUBLANES, KV_GROUPS, LANES)[:, :, 0, :, :maxch * per].max(1)
    un = jnp.where(jnp.arange(maxch * per) == 0, 0.0, un)
    flags = un.reshape(B * nq, KV_GROUPS, maxch, per).max(-1) > 0.5
    cdiag = (jnp.arange(B * nq, dtype=jnp.int32) % nq) * Q2_TILE // SEL_CHUNK
    flags = flags & (jnp.arange(maxch, dtype=jnp.int32)[None, None, :] < cdiag[:, None, None])
    pos = jnp.cumsum(flags.astype(jnp.int32), -1)
    hit = flags[..., None, :] & (pos[..., None, :] == jnp.arange(1, maxch + 1, dtype=jnp.int32)[:, None])
    lst = jnp.sum(jnp.where(hit, jnp.arange(maxch, dtype=jnp.int32), 0), -1)
    return lst.astype(jnp.int32).reshape(-1), pos[..., -1].reshape(-1)


def _nsa2(lst, cnt, qw, kv, sel, ng, acc, qaux, kaux, ex, B, S):
    Q = Q2_TILE
    nq = S // Q
    T = B * S
    rows = lambda b, i, *_: (b * nq + i, 0)
    kvspec = lambda col: pl.BlockSpec((S, LANES), lambda b, i, *_, col=col: (b, col))
    const = lambda shp: pl.BlockSpec(shp, lambda b, i, *_: tuple(0 for _ in shp))
    return pl.pallas_call(
        _nsa2_kernel,
        grid_spec=pltpu.PrefetchScalarGridSpec(
            num_scalar_prefetch=2, grid=(B, nq),
            in_specs=[pl.BlockSpec((Q, HEADS * LANES), rows),
                      kvspec(2), kvspec(3), kvspec(4), kvspec(5),
                      pl.BlockSpec((Q, KV_GROUPS * LANES), rows),
                      pl.BlockSpec((Q, LANES), rows),
                      pl.BlockSpec((Q, GQA * LANES), rows),
                      const(qaux.shape), const(kaux.shape), const(ex.shape)],
            out_specs=pl.BlockSpec((Q, GQA * LANES), rows),
            scratch_shapes=[pltpu.VMEM((GQA * Q, LANES), F32)] * 4
            + [pltpu.VMEM((GQA * Q, SEL_CHUNK), F32)] * 3),
        out_shape=jax.ShapeDtypeStruct((T, GQA * LANES), BF16),
        compiler_params=_cparams("parallel", "parallel"),
    )(lst, cnt, qw, kv, kv, kv, kv, sel, ng, acc, qaux, kaux, ex)


def _nsa_consts(S):
    h = jnp.arange(HEADS, dtype=F32).reshape(KV_GROUPS, GQA, 1, 1)
    slope = jnp.exp2(-8.0 * (h + 1.0) / HEADS)
    lane = jnp.arange(LANES)[None, None, None, :]
    qaux = jnp.where(lane == 0, slope * SEL_BLOCK, jnp.where(lane == 1, slope, 0.0))
    qaux1, qaux2 = (jnp.broadcast_to(qaux, (KV_GROUPS, GQA, n, LANES)).reshape(KV_GROUPS, GQA * n, LANES).astype(BF16)
                    for n in (Q_TILE, Q2_TILE))

    def pos_cols(pos):
        l2 = jnp.arange(LANES)[None, :]
        return jnp.where(l2 == 0, pos[:, None] // SEL_BLOCK, jnp.where(l2 == 1, pos[:, None] % SEL_BLOCK, 0))

    kaux = pos_cols(jnp.arange(S))
    caux = pos_cols(jnp.arange(S // CMP_STRIDE) * CMP_STRIDE + (CMP_BLOCK - 1))
    ex = (jnp.arange(LANES)[:, None] == (jnp.arange(S)[None, :] // SEL_BLOCK))
    return qaux1, qaux2, kaux.astype(BF16), caux.astype(BF16), ex.astype(BF16)


def _pool_kernel(u_ref, halo_ref, w_ref, sc_ref, o_ref, x_sc):
    ts = u_ref.shape[0]
    i = pl.program_id(1)
    x_sc[0:POOL_HALO] = jnp.where(i > 0, halo_ref[...], 0.0)
    x_sc[POOL_HALO:POOL_HALO + ts] = u_ref[...]
    t = i * ts + lax.broadcasted_iota(jnp.int32, (ts, LANES), 0)
    for g, w in enumerate(POOL_WINDOWS):
        cols = slice(g * LANES, (g + 1) * LANES)
        cur = x_sc[POOL_HALO:POOL_HALO + ts, cols]
        acc = cur
        for k in range(1, w):
            acc = acc + x_sc[POOL_HALO - k:POOL_HALO - k + ts, cols]
        cnt = jnp.minimum(t + 1, w).astype(F32)
        pooled = acc / cnt - cur
        o_ref[:, cols] = (_dot(pooled.astype(BF16), w_ref[g]) * sc_ref[:, cols]).astype(o_ref.dtype)


def _pool(u, pool_w, pool_scale, B, S, ts):
    nt = S // ts
    hb = ts // POOL_HALO
    C = u.shape[1]
    return pl.pallas_call(
        _pool_kernel,
        grid=(B, nt),
        in_specs=[pl.BlockSpec((ts, C), lambda b, i: (b * nt + i, 0)),
                  pl.BlockSpec((POOL_HALO, C), lambda b, i: (jnp.maximum((b * nt + i) * hb - 1, 0), 0)),
                  pl.BlockSpec(pool_w.shape, lambda b, i: (0, 0, 0)),
                  pl.BlockSpec((1, C), lambda b, i: (0, 0))],
        out_specs=pl.BlockSpec((ts, C), lambda b, i: (b * nt + i, 0)),
        out_shape=jax.ShapeDtypeStruct((B * S, C), BF16),
        scratch_shapes=[pltpu.VMEM((POOL_HALO + ts, C), F32)],
        compiler_params=_cparams("parallel", "parallel"),
    )(u, u, pool_w, pool_scale)


SSM_BATCH = 4
SSM_PASS = 8
SSM_BLOCKS = SSM_GROUPS * SSM_GROUP // LANES
SSM_BLK_W = 2 * SSM_NS // SSM_BLOCKS


def _ssm_kernel(u_ref, bt_ref, a_ref, c_ref, d_ref, wg_ref, bg_ref, o_ref, xh, hst, io_sc):
    steps = u_ref.shape[0]
    C = u_ref.shape[1] // SSM_BATCH
    rows = steps * SSM_BATCH

    @pl.when(pl.program_id(0) == 0)
    def _():
        hst[...] = jnp.zeros(hst.shape, F32)

    nslab = C // LANES
    for b in range(SSM_BATCH):
        for j in range(nslab):
            io_sc[j, pl.ds(b, steps, stride=SSM_BATCH), :] = u_ref[:, b * C + j * LANES:b * C + (j + 1) * LANES]
    u = jnp.concatenate([io_sc[j] for j in range(nslab)], axis=1)
    ub = u.astype(BF16)
    for k in range(SSM_BLOCKS):
        xh[:, k * SSM_BLK_W:(k + 1) * SSM_BLK_W] = _dot(ub[:, k * LANES:(k + 1) * LANES], bt_ref[k])
    low = lax.broadcasted_iota(jnp.int32, (SUBLANES, LANES), 0) < 4
    nchunk = SSM_NS // LANES
    per_blk = nchunk // SSM_BLOCKS
    for p in range(nchunk // SSM_PASS):
        cols = [p * SSM_PASS + j for j in range(SSM_PASS)]
        start = [(c // per_blk) * SSM_BLK_W + (c % per_blk) * LANES for c in cols]
        re = [slice(s, s + LANES) for s in start]
        im = [slice(s + SSM_BLK_W // 2, s + SSM_BLK_W // 2 + LANES) for s in start]
        ar = [a_ref[:, s] for s in re]
        ai = [a_ref[:, s] for s in im]

        def tile(k, carry, re=re, im=im, ar=ar, ai=ai):
            cr, ci = carry
            r0 = pl.multiple_of(k * SUBLANES, SUBLANES)
            ncr, nci = [], []
            for j in range(SSM_PASS):
                vr = xh[pl.ds(r0, SUBLANES), re[j]]
                vi = xh[pl.ds(r0, SUBLANES), im[j]]
                t1r = ar[j] * cr[j] - ai[j] * ci[j] + vr
                t1i = ar[j] * ci[j] + ai[j] * cr[j] + vi
                pr = pltpu.roll(t1r, 4, 0)
                pi = pltpu.roll(t1i, 4, 0)
                t2r = ar[j] * pr - ai[j] * pi + vr
                t2i = ar[j] * pi + ai[j] * pr + vi
                xh[pl.ds(r0, SUBLANES), re[j]] = jnp.where(low, t1r, t2r)
                xh[pl.ds(r0, SUBLANES), im[j]] = jnp.where(low, t1i, t2i)
                ncr.append(pltpu.roll(t2r, 4, 0))
                nci.append(pltpu.roll(t2i, 4, 0))
            return tuple(ncr), tuple(nci)

        cr0 = tuple(hst[:, s] for s in re)
        ci0 = tuple(hst[:, s] for s in im)
        cr, ci = lax.fori_loop(0, rows // SUBLANES, tile, (cr0, ci0))
        for j in range(SSM_PASS):
            hst[:, re[j]] = cr[j]
            hst[:, im[j]] = ci[j]
    y = jnp.concatenate([_dot(xh[:, k * SSM_BLK_W:(k + 1) * SSM_BLK_W].astype(BF16), c_ref[k])
                         for k in range(SSM_BLOCKS)], axis=1) + d_ref[...] * u
    z = _gelu(y)
    out = z * _sigmoid(_dot(z.astype(BF16), wg_ref[...]) + bg_ref[...])
    for j in range(nslab):
        io_sc[j] = out[:, j * LANES:(j + 1) * LANES]
    for b in range(SSM_BATCH):
        for j in range(nslab):
            o_ref[:, b * C + j * LANES:b * C + (j + 1) * LANES] = (
                io_sc[j, pl.ds(b, steps, stride=SSM_BATCH), :].astype(o_ref.dtype))


def _ssm(u, bt, a_b, cmat, d_skip, w_glu, b_glu, steps):
    S, W = u.shape
    rows = steps * SSM_BATCH
    const = lambda shp: pl.BlockSpec(shp, lambda i: tuple(0 for _ in shp))
    return pl.pallas_call(
        _ssm_kernel,
        grid=(S // steps,),
        in_specs=[pl.BlockSpec((steps, W), lambda i: (i, 0)),
                  const(bt.shape), const(a_b.shape), const(cmat.shape), const(d_skip.shape),
                  const(w_glu.shape), const(b_glu.shape)],
        out_specs=pl.BlockSpec((steps, W), lambda i: (i, 0)),
        out_shape=jax.ShapeDtypeStruct((S, W), BF16),
        scratch_shapes=[pltpu.VMEM((rows, 2 * SSM_NS), F32), pltpu.VMEM((SUBLANES, 2 * SSM_NS), F32),
                        pltpu.VMEM((W // SSM_BATCH // LANES, rows, LANES), F32)],
        compiler_params=_cparams("arbitrary"),
    )(u, bt, a_b, cmat, d_skip, w_glu, b_glu)


def _merge_kernel(on_ref, op_ref, os_ref, xin_ref, x_ref, wbg_ref, wn_ref, wp_ref, ws_ref, wo_ref, g_ref, b_ref,
                  xo_ref, xb_ref):
    D = x_ref.shape[1]
    xin = xin_ref[...]
    m = None
    for k, (o_ref, w_ref) in enumerate(((on_ref, wn_ref), (op_ref, wp_ref), (os_ref, ws_ref))):
        gate = _sigmoid(_dot(xin, wbg_ref[:, k * D:(k + 1) * D]))
        term = gate * _dot(o_ref[...], w_ref[...])
        m = term if m is None else m + term
    h = DN_ALPHA * x_ref[...] + _dot(m.astype(BF16), wo_ref[...])
    y = _layer_norm(h, g_ref[...], b_ref[...])
    xo_ref[...] = y
    xb_ref[...] = y.astype(BF16)


def _merge(o_nsa, o_pool, o_ssm_tm, xb, x, wbg, wn, wp, ws, wo, g, b, B, S, tm):
    nt = S // tm
    T, D = x.shape
    W = o_nsa.shape[1]
    rows = lambda bb, i: (bb * nt + i, 0)
    const = lambda shp: pl.BlockSpec(shp, lambda bb, i: tuple(0 for _ in shp))
    return pl.pallas_call(
        _merge_kernel,
        grid=(B, nt),
        in_specs=[pl.BlockSpec((tm, W), rows), pl.BlockSpec((tm, W), rows),
                  pl.BlockSpec((tm, W), lambda bb, i: (i, bb)),
                  pl.BlockSpec((tm, D), rows), pl.BlockSpec((tm, D), rows),
                  const(wbg.shape), const(wn.shape), const(wp.shape), const(ws.shape), const(wo.shape),
                  const(g.shape), const(b.shape)],
        out_specs=[pl.BlockSpec((tm, D), rows), pl.BlockSpec((tm, D), rows)],
        out_shape=[jax.ShapeDtypeStruct((T, D), F32), jax.ShapeDtypeStruct((T, D), BF16)],
        compiler_params=_cparams("parallel", "parallel"),
    )(o_nsa, o_pool, o_ssm_tm, xb, x, wbg, wn, wp, ws, wo, g, b)


def _router_kernel(x_ref, wr_ref, br_ref, ri_ref, rw_ref, cnt_ref, carry):
    tm = x_ref.shape[0]

    @pl.when(pl.program_id(0) == 0)
    def _():
        carry[...] = jnp.zeros(carry.shape, F32)

    x = x_ref[...]
    xh = x.astype(BF16)
    xl = (x - xh.astype(F32)).astype(BF16)
    both = _dot(xh, wr_ref[...])
    logits = both[:, 0:LANES] + both[:, LANES:2 * LANES] + _dot(xl, wr_ref[:, 0:LANES]) + br_ref[...]
    lane = lax.broadcasted_iota(jnp.int32, (tm, LANES), 1)
    big = jnp.int32(LANES)
    isg = lane < MOE_GROUPS
    lg = jnp.where(isg, logits, NEG_INF)
    eg = jnp.where(isg, jnp.exp(lg - jnp.max(lg, -1, keepdims=True)), 0.0)
    pgv = jnp.where(isg, eg / jnp.sum(eg, -1, keepdims=True), -1.0)
    p_g = jnp.max(pgv, -1, keepdims=True)
    g_sel = jnp.min(jnp.where(pgv == p_g, lane, big), -1, keepdims=True)
    ine = ((lane >= EXP_LANE0) & (lane < EXP_LANE0 + MOE_EXPERTS)
           & (jnp.right_shift(lane - EXP_LANE0, EPG_SHIFT) == g_sel))
    le = jnp.where(ine, logits, NEG_INF)
    ee = jnp.where(ine, jnp.exp(le - jnp.max(le, -1, keepdims=True)), 0.0)
    p = jnp.where(ine, ee / jnp.sum(ee, -1, keepdims=True), -1.0)
    p1 = jnp.max(p, -1, keepdims=True)
    i1 = jnp.min(jnp.where(p == p1, lane, big), -1, keepdims=True)
    prest = jnp.where(lane == i1, -1.0, p)
    p2 = jnp.max(prest, -1, keepdims=True)
    i2 = jnp.min(jnp.where(prest == p2, lane, big), -1, keepdims=True)
    den = p1 + p2
    w0 = p1 / den * p_g
    w1 = p2 / den * p_g
    e0 = i1 - EXP_LANE0
    e1 = i2 - EXP_LANE0
    is0 = lane == e0
    is1 = lane == e1
    onehot = jnp.where(is0 | is1, 1.0, 0.0)
    r = lax.broadcasted_iota(jnp.int32, (tm, tm), 0)
    c = lax.broadcasted_iota(jnp.int32, (tm, tm), 1)
    tri = jnp.where(r > c, 1.0, 0.0).astype(BF16)
    cum = _dot(tri, onehot.astype(BF16)) + carry[0:1, :]
    rank0 = jnp.sum(jnp.where(is0, cum, 0.0), -1, keepdims=True).astype(jnp.int32)
    rank1 = jnp.sum(jnp.where(is1, cum, 0.0), -1, keepdims=True).astype(jnp.int32)
    carry[...] = carry[...] + jnp.sum(onehot, 0, keepdims=True)
    cnt_ref[...] = carry[...]
    zi = jnp.zeros((tm, LANES), jnp.int32)
    ri_ref[...] = jnp.where(lane == 0, e0, jnp.where(lane == 1, e1, jnp.where(lane == 2, rank0, jnp.where(lane == 3, rank1, zi))))
    rw_ref[...] = jnp.where(lane == 0, w0, jnp.where(lane == 1, w1, 0.0))


def _router(x, wr, br, tm):
    T, D = x.shape
    return pl.pallas_call(
        _router_kernel,
        grid=(T // tm,),
        in_specs=[pl.BlockSpec((tm, D), lambda i: (i, 0)),
                  pl.BlockSpec(wr.shape, lambda i: (0, 0)), pl.BlockSpec(br.shape, lambda i: (0, 0))],
        out_specs=[pl.BlockSpec((tm, LANES), lambda i: (i, 0)), pl.BlockSpec((tm, LANES), lambda i: (i, 0)),
                   pl.BlockSpec((SUBLANES, LANES), lambda i: (0, 0))],
        out_shape=[jax.ShapeDtypeStruct((T, LANES), jnp.int32), jax.ShapeDtypeStruct((T, LANES), F32),
                   jax.ShapeDtypeStruct((SUBLANES, LANES), F32)],
        scratch_shapes=[pltpu.VMEM((SUBLANES, LANES), F32)],
        compiler_params=_cparams("arbitrary"),
    )(x, wr, br)


def _row_copy(src, s, dst, d, sem):
    return pltpu.make_async_copy(src.at[pl.ds(s, 1)], dst.at[pl.ds(d, 1)], sem)


def _dispatch_kernel(dest_ref, x_ref, xs_in, xs_hbm, sem, *, tm):
    del xs_in

    def issue(r, c):
        for k in range(2):
            _row_copy(x_ref, r, xs_hbm, dest_ref[0, 0, 2 * r + k], sem).start(priority=k)
        return c

    lax.fori_loop(0, tm, issue, 0, unroll=DMA_UNROLL)
    pltpu.make_async_copy(xs_hbm.at[pl.ds(0, 2 * tm)], xs_hbm.at[pl.ds(0, 2 * tm)], sem).wait()


def _dispatch(dest3, x, xs0, tm):
    T = x.shape[0]
    return pl.pallas_call(
        functools.partial(_dispatch_kernel, tm=tm),
        grid=(T // tm,),
        in_specs=[pl.BlockSpec((1, 1, 2 * tm), lambda i: (i, 0, 0), memory_space=pltpu.SMEM),
                  pl.BlockSpec((tm, x.shape[1]), lambda i: (i, 0)), pl.BlockSpec(memory_space=pl.ANY)],
        out_specs=pl.BlockSpec(memory_space=pl.ANY),
        out_shape=jax.ShapeDtypeStruct(xs0.shape, xs0.dtype),
        scratch_shapes=[pltpu.SemaphoreType.DMA(())],
        input_output_aliases={2: 0},
        compiler_params=_cparams("arbitrary"),
    )(dest3, x, xs0)


def _expert_kernel(be_ref, nu_ref, xs_ref, wg_ref, wu_ref, wd_ref, ys_ref):
    i = pl.program_id(0)

    @pl.when(i < nu_ref[0])
    def _():
        x = xs_ref[...].astype(BF16)
        a = _dot(x, wg_ref[0].astype(BF16))
        h = a * _sigmoid(a) * _dot(x, wu_ref[0].astype(BF16))
        ys_ref[...] = _dot(h.astype(BF16), wd_ref[0].astype(BF16))

    @pl.when(i >= nu_ref[0])
    def _():
        ys_ref[...] = jnp.zeros(ys_ref.shape, F32)


def _experts(blk_e, nused, xs, wg, wu, wd):
    P, D = xs.shape
    F = wg.shape[2]
    nblk = P // MOE_ROWS
    return pl.pallas_call(
        _expert_kernel,
        grid_spec=pltpu.PrefetchScalarGridSpec(
            num_scalar_prefetch=2, grid=(nblk,),
            in_specs=[pl.BlockSpec((MOE_ROWS, D), lambda i, be, nu: (i, 0)),
                      pl.BlockSpec((1, D, F), lambda i, be, nu: (be[i], 0, 0)),
                      pl.BlockSpec((1, D, F), lambda i, be, nu: (be[i], 0, 0)),
                      pl.BlockSpec((1, F, D), lambda i, be, nu: (be[i], 0, 0))],
            out_specs=pl.BlockSpec((MOE_ROWS, D), lambda i, be, nu: (i, 0))),
        out_shape=jax.ShapeDtypeStruct((P, D), F32),
        compiler_params=_cparams("arbitrary"),
    )(blk_e, nused, xs, wg, wu, wd)


def _combine_kernel(dest_ref, ys_hbm, x_ref, rw_ref, g_ref, b_ref, xo_ref, xb_ref, buf, sem, *, tm):
    def issue(r, c):
        for k in range(2):
            _row_copy(ys_hbm, dest_ref[0, 0, 2 * r + k], buf.at[k], r, sem).start(priority=k)
        return c

    lax.fori_loop(0, tm, issue, 0, unroll=DMA_UNROLL)
    for k in range(2):
        pltpu.make_async_copy(ys_hbm.at[pl.ds(0, tm)], buf.at[k], sem).wait()
    w = rw_ref[...]
    h = DN_ALPHA * x_ref[...] + w[:, 0:1] * buf[0] + w[:, 1:2] * buf[1]
    y = _layer_norm(h, g_ref[...], b_ref[...])
    xo_ref[...] = y
    xb_ref[...] = y.astype(BF16)


def _combine(dest3, ys, x, rw, g, b, tm):
    T, D = x.shape
    return pl.pallas_call(
        functools.partial(_combine_kernel, tm=tm),
        grid=(T // tm,),
        in_specs=[pl.BlockSpec((1, 1, 2 * tm), lambda i: (i, 0, 0), memory_space=pltpu.SMEM),
                  pl.BlockSpec(memory_space=pl.ANY),
                  pl.BlockSpec((tm, D), lambda i: (i, 0)), pl.BlockSpec((tm, LANES), lambda i: (i, 0)),
                  pl.BlockSpec(g.shape, lambda i: (0, 0)), pl.BlockSpec(b.shape, lambda i: (0, 0))],
        out_specs=[pl.BlockSpec((tm, D), lambda i: (i, 0)), pl.BlockSpec((tm, D), lambda i: (i, 0))],
        out_shape=[jax.ShapeDtypeStruct((T, D), F32), jax.ShapeDtypeStruct((T, D), BF16)],
        scratch_shapes=[pltpu.VMEM((2, tm, D), F32), pltpu.SemaphoreType.DMA(())],
        compiler_params=_cparams("arbitrary"),
    )(dest3, ys, x, rw, g, b)


def _prep_in_proj(w_in):
    L, D, _ = w_in.shape
    w_in = w_in.astype(BF16)
    nw = HEADS * HEAD_DIM
    kvw = 6 * KV_GROUPS * HEAD_DIM
    o = 0
    wq = w_in[..., o:o + nw]; o += nw
    wkv = w_in[..., o:o + kvw]; o += kvw
    wng = w_in[..., o:o + 3 * HEADS]; o += 3 * HEADS
    wpool = w_in[..., o:o + D // 2]; o += D // 2
    wssm = w_in[..., o:o + D // 2]; o += D // 2
    wbg = w_in[..., o:]
    wq = wq.reshape(L, D, KV_GROUPS, GQA, HEAD_DIM) * (HEAD_DIM ** -0.5)
    z = jnp.zeros_like(wq[:, :, 0])
    wq_wide = jnp.stack([jnp.concatenate([wq[:, :, 0], z], -1), jnp.concatenate([z, wq[:, :, 1]], -1)], 2)
    wq_wide = wq_wide.reshape(L, D, HEADS * LANES)
    wng = jnp.pad(wng, ((0, 0), (0, 0), (0, LANES - 3 * HEADS)))
    w_a = jnp.concatenate([wq_wide, wkv, wng, wpool], -1).astype(BF16)
    return w_a, wbg.astype(BF16), wssm.astype(BF16)


def _prep_compress(pe_k, pe_v, w1k, w2k, w1v, w2v):
    L = pe_k.shape[0]
    half = CMP_BLOCK // 2
    cw = 2 * KV_GROUPS * HEAD_DIM
    eye_s = jnp.eye(2, dtype=F32)
    eye_g = jnp.eye(KV_GROUPS, dtype=F32)
    w1r = jnp.stack([w1k, w1v], 1).reshape(L, 2, 2, half, HEAD_DIM, CMP_HIDDEN)
    w1 = jnp.einsum('aspldh,st,gj->alsgdptjh', w1r, eye_s, eye_g)
    per = jnp.stack([pe_k, pe_v], 1).reshape(L, 2, 2, half, HEAD_DIM)
    pe = jnp.broadcast_to(per.transpose(0, 2, 3, 1, 4)[:, :, :, :, None, :], (L, 2, half, 2, KV_GROUPS, HEAD_DIM))
    pe = jnp.pad(pe.reshape(L, 2, half * cw), ((0, 0), (0, SUBLANES - 2), (0, 0)))
    w2 = jnp.einsum('ashd,st,gj->asghtjd', jnp.stack([w2k, w2v], 1), eye_s, eye_g)
    w1 = w1.reshape(L, half * cw, 2 * 2 * KV_GROUPS * CMP_HIDDEN)
    w2 = w2.reshape(L, 2 * KV_GROUPS * CMP_HIDDEN, 2 * KV_GROUPS * HEAD_DIM)
    return w1.astype(BF16), pe.astype(BF16), w2.astype(BF16)


def _overlap(nc):
    i0 = jnp.arange(nc)[:, None] * CMP_STRIDE
    j0 = jnp.arange(LANES)[None, :] * SEL_BLOCK
    return ((i0 < j0 + SEL_BLOCK) & (i0 + CMP_BLOCK > j0)).astype(BF16)


def _prep_ssm(lam_re, lam_im, log_dt, b_re, b_im, c_re, c_im):
    L = lam_re.shape[0]
    dt = jnp.exp(log_dt)[..., None]
    mag = jnp.exp(lam_re * dt)
    ar, ai = mag * jnp.cos(lam_im * dt), mag * jnp.sin(lam_im * dt)
    den = lam_re * lam_re + lam_im * lam_im
    cr = ((ar - 1.0) * lam_re + ai * lam_im) / den
    ci = (ai * lam_re - (ar - 1.0) * lam_im) / den
    bbr = cr[..., None] * b_re - ci[..., None] * b_im
    bbi = cr[..., None] * b_im + ci[..., None] * b_re
    nb = SSM_BLOCKS
    gb = SSM_GROUPS // nb
    half = SSM_BLK_W // 2
    eye = jnp.eye(gb, dtype=F32)
    bt = jnp.concatenate([jnp.einsum('lkgph,gj->lkghjp', m.reshape(L, nb, gb, SSM_STATE, SSM_GROUP), eye)
                          .reshape(L, nb, LANES, half) for m in (bbr, bbi)], -1)
    cm = jnp.concatenate([jnp.einsum('lkghp,gj->lkgpjh', m.reshape(L, nb, gb, SSM_GROUP, SSM_STATE), eye)
                          .reshape(L, nb, half, LANES) for m in (c_re, -c_im)], 2)
    a = jnp.concatenate([ar.reshape(L, nb, half), ai.reshape(L, nb, half)], -1).reshape(L, 1, 2 * SSM_NS)
    a = jnp.broadcast_to(a, (L, SUBLANES, 2 * SSM_NS))
    return bt.astype(BF16), a, cm.astype(BF16)


def _prep_router(w_grp, b_grp, w_exp, b_exp):
    L, D, _ = w_grp.shape
    wr = jnp.zeros((L, D, LANES), F32).at[:, :, 0:MOE_GROUPS].set(w_grp)
    wr = wr.at[:, :, EXP_LANE0:EXP_LANE0 + MOE_EXPERTS].set(w_exp)
    br = jnp.zeros((L, 1, LANES), F32).at[:, 0, 0:MOE_GROUPS].set(b_grp)
    br = br.at[:, 0, EXP_LANE0:EXP_LANE0 + MOE_EXPERTS].set(b_exp)
    hi = wr.astype(BF16)
    lo = (wr - hi.astype(F32)).astype(BF16)
    return jnp.concatenate([hi, lo], -1), br


def _layer(B, S, consts, expert_w, carry, w):
    x, xb, xs = carry
    T, D = x.shape
    nc = S // CMP_STRIDE
    xb_in = xb
    qw, kv, ng, u_pool = _proj(xb, w["w_a"], ((HEADS * LANES, BF16), (6 * LANES, BF16), (LANES, F32), (D // 2, F32)), 512)
    u_ssm = _proj_time_major(xb, w["w_c"], B, S, 512)
    kr = kv[:, 0:2 * LANES].reshape(T // CMP_STRIDE, CMP_STRIDE * 2 * LANES)
    kvc = _compress(kr, w["cw1"], w["cpe"], w["cw2"], B, nc)
    qaux1, qaux2, kaux, caux, ex = consts
    acc, sel, un = _nsa1(qw, kvc, ng, w["ov"], qaux1, caux, B, S)
    lst, cnt = _chunk_lists(un, B, S)
    o_nsa = _nsa2(lst, cnt, qw, kv, sel, ng, acc, qaux2, kaux, ex, B, S)
    o_pool = _pool(u_pool, w["pool_w"], w["pool_scale"], B, S, 512)
    o_ssm = _ssm(u_ssm, w["bt"], w["a"], w["cm"], w["d"], w["w_glu"], w["b_glu"], 128)
    x, xb = _merge(o_nsa, o_pool, o_ssm, xb_in, x, w["w_bg"], w["wn"], w["wp"], w["ws"], w["wo"],
                   w["ln1_g"], w["ln1_b"], B, S, 512)
    tm = 512
    ri, rw, cnt = _router(x, w["wr"], w["br"], 512)
    counts = cnt[0, 0:MOE_EXPERTS].astype(jnp.int32)
    padded = (counts + MOE_ROWS - 1) // MOE_ROWS * MOE_ROWS
    pend = jnp.cumsum(padded)
    pstart = pend - padded
    nblk = (2 * T) // MOE_ROWS + MOE_EXPERTS
    eids = jnp.arange(MOE_EXPERTS, dtype=jnp.int32)
    dest = jnp.sum(jnp.where(ri[:, 0:2, None] == eids, pstart, 0), -1) + ri[:, 2:4]
    dest3 = dest.reshape(T // tm, 1, 2 * tm)
    blk_e = jnp.minimum(jnp.sum(pend[None, :] <= jnp.arange(nblk, dtype=jnp.int32)[:, None] * MOE_ROWS, -1),
                        MOE_EXPERTS - 1).astype(jnp.int32)
    nused = (pend[-1:] // MOE_ROWS).astype(jnp.int32)
    xs = _dispatch(dest3, x, xs, tm)
    ys = _experts(blk_e + w["layer"] * MOE_EXPERTS, nused, xs, *expert_w)
    x, xb = _combine(dest3, ys, x, rw, w["ln2_g"], w["ln2_b"], tm)
    return (x, xb, xs), None


def kernel(x, w_in, cmp_pe_k, cmp_pe_v, cmp_w1_k, cmp_w2_k, cmp_w1_v, cmp_w2_v, pool_w, pool_scale, ssm_lam_re, ssm_lam_im, ssm_log_dt, ssm_b_re, ssm_b_im, ssm_c_re, ssm_c_im, ssm_d, ssm_w_glu, ssm_b_glu, w_up_nsa, w_up_pool, w_up_ssm, w_out, ln1_g, ln1_b, router_w_grp, router_b_grp, router_w_exp, router_b_exp, moe_w_gate, moe_w_up, moe_w_down, ln2_g, ln2_b):
    B, S, D = x.shape
    L = w_in.shape[0]
    assert B == SSM_BATCH and D == D_MODEL and S % SEL_CHUNK == 0 and S // SEL_BLOCK <= LANES
    assert Q_TILE == LANES and Q2_TILE % Q_TILE == 0 and SEL_CHUNK % Q2_TILE == 0 and WINDOW % Q2_TILE == 0
    w_a, w_bg, w_c = _prep_in_proj(w_in)
    cw1, cpe, cw2 = _prep_compress(cmp_pe_k, cmp_pe_v, cmp_w1_k, cmp_w2_k, cmp_w1_v, cmp_w2_v)
    bt, a, cm = _prep_ssm(ssm_lam_re, ssm_lam_im, ssm_log_dt, ssm_b_re, ssm_b_im, ssm_c_re, ssm_c_im)
    wr, br = _prep_router(router_w_grp, router_b_grp, router_w_exp, router_b_exp)
    wn = w_up_nsa.reshape(L, KV_GROUPS, GQA, HEAD_DIM, D).transpose(0, 2, 1, 3, 4).reshape(L, HEADS * HEAD_DIM, D)
    row = lambda v: v.reshape(L, 1, -1)
    ws = dict(
        w_a=w_a, w_bg=w_bg, w_c=w_c, cw1=cw1, cpe=cpe, cw2=cw2,
        ov=jnp.broadcast_to(_overlap(S // CMP_STRIDE), (L, S // CMP_STRIDE, LANES)),
        pool_w=pool_w.astype(BF16), pool_scale=row(pool_scale),
        bt=bt, a=a, cm=cm, d=row(ssm_d), w_glu=ssm_w_glu.astype(BF16), b_glu=row(ssm_b_glu),
        wn=wn.astype(BF16), wp=w_up_pool.astype(BF16), ws=w_up_ssm.astype(BF16), wo=w_out.astype(BF16),
        ln1_g=row(ln1_g), ln1_b=row(ln1_b), wr=wr, br=br,
        layer=jnp.arange(L, dtype=jnp.int32),
        ln2_g=row(ln2_g), ln2_b=row(ln2_b))
    xf = x.reshape(B * S, D)
    expert_w = tuple(m.reshape((L * MOE_EXPERTS,) + m.shape[2:]) for m in (moe_w_gate, moe_w_up, moe_w_down))
    xs0 = jnp.zeros((2 * B * S + MOE_EXPERTS * MOE_ROWS, D), F32)
    (xf, _, _), _ = lax.scan(functools.partial(_layer, B, S, _nsa_consts(S), expert_w),
                             (xf, xf.astype(BF16), xs0), ws)
    return xf.reshape(B, S, D)
```
